```python
import jax, jax.numpy as jnp
from jax import lax
import numpy as np

D_MODEL = 1024
BATCH = 8
SEQ = 16384
DEPTH = 4

CHUNK = 64
HEAD_DIM = 64
N_HEADS_SB = 4
N_HEADS_CH = 8
CONV_CH = 256
CONV_WIDTH = 3
N_PREV_CHUNKS = 8
BAND = (N_PREV_CHUNKS + 1) * CHUNK
REL_CLIP = 128
Q_BLOCK = 128
D_SB = N_HEADS_SB * HEAD_DIM
D_CH = N_HEADS_CH * HEAD_DIM
D_MIX = D_SB + D_CH + CONV_CH
D_IN = 3 * D_SB + 3 * D_CH + 3 * CONV_CH
N_OUT_GROUPS = D_MIX // HEAD_DIM
D_FF = ((8 * D_MODEL // 3 + 255) // 256) * 256
EPS = 1e-6

kernel_name = "hybrid_stickbreak_chunkattn_shortconv_block"


def rmsnorm(x, w):
    xf = x.astype(jnp.float32)
    y = xf * lax.rsqrt(jnp.mean(xf * xf, axis=-1, keepdims=True) + EPS)
    return (y * w.astype(jnp.float32)).astype(x.dtype)


def group_rmsnorm(y, w):
    b, s, _ = y.shape
    yg = y.astype(jnp.float32).reshape(b, s, N_OUT_GROUPS, HEAD_DIM)
    yg = yg * lax.rsqrt(jnp.mean(yg * yg, axis=-1, keepdims=True) + EPS)
    return (yg.reshape(b, s, D_MIX) * w.astype(jnp.float32)).astype(y.dtype)


def to_heads(t):
    b, s, _ = t.shape
    return t.reshape(b, s, -1, HEAD_DIM).transpose(0, 2, 1, 3)


def from_heads(t):
    b, h, s, d = t.shape
    return t.transpose(0, 2, 1, 3).reshape(b, s, h * d)


def stick_breaking_attention(q, k, v):
    b, h, s, d = q.shape
    n = s // Q_BLOCK
    qb = (q.astype(jnp.float32) * (d ** -0.5)).reshape(b, h, n, Q_BLOCK, d)
    kb = k.astype(jnp.float32).reshape(b, h, n, Q_BLOCK, d)
    vb = v.astype(jnp.float32).reshape(b, h, n, Q_BLOCK, d)
    idx = jnp.arange(Q_BLOCK)
    after_mat = (idx[:, None] > idx[None, :]).astype(jnp.float32)
    diag_mask = idx[None, :] < idx[:, None]
    out = jnp.zeros((b, h, n, Q_BLOCK, d), jnp.float32)
    carry = jnp.zeros((b, h, n, Q_BLOCK), jnp.float32)
    for o in range(n):
        z = jnp.einsum('bhnqd,bhnkd->bhnqk', qb[:, :, o:], kb[:, :, :n - o])
        log_rest = jax.nn.log_sigmoid(-z)
        if o == 0:
            log_rest = jnp.where(diag_mask, log_rest, 0.0)
        after = jnp.einsum('bhnqj,js->bhnqs', log_rest, after_mat) + carry[:, :, o:, :, None]
        w = jnp.exp(jax.nn.log_sigmoid(z) + after)
        if o == 0:
            w = jnp.where(diag_mask, w, 0.0)
        out = out.at[:, :, o:].add(jnp.einsum('bhnqk,bhnkd->bhnqd', w, vb[:, :, :n - o]))
        carry = carry.at[:, :, o:].add(jnp.sum(log_rest, axis=-1))
    return out.reshape(b, h, s, d).astype(v.dtype)


def chunked_relpos_attention(q, k, v, rel_bias):
    b, h, s, d = q.shape
    nc = s // CHUNK
    pad = N_PREV_CHUNKS * CHUNK
    qc = (q.astype(jnp.float32) * (d ** -0.5)).reshape(b, h, nc, CHUNK, d)
    kpad = jnp.pad(k, ((0, 0), (0, 0), (pad, 0), (0, 0))).reshape(b, h, nc + N_PREV_CHUNKS, CHUNK, d)
    vpad = jnp.pad(v, ((0, 0), (0, 0), (pad, 0), (0, 0))).reshape(b, h, nc + N_PREV_CHUNKS, CHUNK, d)
    band_idx = jnp.arange(nc)[:, None] + jnp.arange(N_PREV_CHUNKS + 1)[None, :]
    kb = kpad[:, :, band_idx].reshape(b, h, nc, BAND, d).astype(jnp.float32)
    vb = vpad[:, :, band_idx].reshape(b, h, nc, BAND, d).astype(jnp.float32)
    scores = jnp.einsum('bhcqd,bhckd->bhcqk', qc, kb)
    p = jnp.arange(CHUNK)[:, None]
    m = jnp.arange(BAND)[None, :]
    rel = jnp.clip(N_PREV_CHUNKS * CHUNK + p - m, -REL_CLIP, REL_CLIP) + REL_CLIP
    bias = rel_bias.astype(jnp.float32)[:, rel]
    valid = jnp.repeat(band_idx >= N_PREV_CHUNKS, CHUNK, axis=1)
    scores = jnp.where(valid[None, None, :, None, :], scores + bias[None, :, None], -jnp.inf)
    probs = jax.nn.softmax(scores, axis=-1)
    out = jnp.einsum('bhcqk,bhckd->bhcqd', probs, vb)
    return out.reshape(b, h, s, d).astype(v.dtype)


def short_conv_mixer(gate_b, gate_c, xc, conv_w):
    hc = gate_c * xc
    y = lax.conv_general_dilated(
        hc, conv_w[:, None, :].astype(hc.dtype), window_strides=(1,),
        padding=[(CONV_WIDTH - 1, 0)], dimension_numbers=('NWC', 'WIO', 'NWC'),
        feature_group_count=CONV_CH)
    return gate_b * y


def swiglu(x, w_gate, w_up, w_down):
    return (jax.nn.silu(x @ w_gate) * (x @ w_up)) @ w_down


def _fwd_setup_inputs(seed: int = 0) -> dict:
    key = jax.random.key(seed)
    ks = jax.random.split(key, 14)
    f32 = jnp.float32

    def gain(k, n):
        return 1.0 + 0.02 * jax.random.normal(k, (DEPTH, n), f32)

    return {
        "x": jax.random.normal(ks[0], (BATCH, SEQ, D_MODEL), f32),
        "attn_norm_w": gain(ks[1], D_MODEL),
        "w_in": jax.random.normal(ks[2], (DEPTH, D_MODEL, D_IN), f32) * D_MODEL ** -0.5,
        "q_norm_w": gain(ks[3], HEAD_DIM),
        "k_norm_w": gain(ks[4], HEAD_DIM),
        "rel_bias": 0.1 * jax.random.normal(ks[5], (DEPTH, N_HEADS_CH, 2 * REL_CLIP + 1), f32),
        "conv_w": jax.random.normal(ks[6], (DEPTH, CONV_WIDTH, CONV_CH), f32) * CONV_WIDTH ** -0.5,
        "out_norm_w": gain(ks[7], D_MIX),
        "w_out": jax.random.normal(ks[8], (DEPTH, D_MIX, D_MODEL), f32) * D_MIX ** -0.5,
        "ffn_norm_w": gain(ks[9], D_MODEL),
        "w_gate": jax.random.normal(ks[10], (DEPTH, D_MODEL, D_FF), f32) * D_MODEL ** -0.5,
        "w_up": jax.random.normal(ks[11], (DEPTH, D_MODEL, D_FF), f32) * D_MODEL ** -0.5,
        "w_down": jax.random.normal(ks[12], (DEPTH, D_FF, D_MODEL), f32) * D_FF ** -0.5,
    }


def _fwd_reference(x, attn_norm_w, w_in, q_norm_w, k_norm_w, rel_bias, conv_w, out_norm_w,
              w_out, ffn_norm_w, w_gate, w_up, w_down):
    widths = [D_SB] * 3 + [D_CH] * 3 + [CONV_CH] * 3
    split_points = [int(v) for v in np.cumsum(widths)[:-1]]
    for l in range(DEPTH):
        h = rmsnorm(x, attn_norm_w[l])
        proj = h @ w_in[l]
        (q_a, k_a, v_a, q_b, k_b, v_b, g_b, g_c, x_c) = jnp.split(proj, split_points, axis=-1)
        y_sb = from_heads(stick_breaking_attention(to_heads(q_a), to_heads(k_a), to_heads(v_a)))
        qh = rmsnorm(to_heads(q_b), q_norm_w[l])
        kh = rmsnorm(to_heads(k_b), k_norm_w[l])
        y_ch = from_heads(chunked_relpos_attention(qh, kh, to_heads(v_b), rel_bias[l]))
        y_cv = short_conv_mixer(g_b, g_c, x_c, conv_w[l])
        y = group_rmsnorm(jnp.concatenate([y_sb, y_ch, y_cv], axis=-1), out_norm_w[l])
        x = x + y @ w_out[l]
        x = x + swiglu(rmsnorm(x, ffn_norm_w[l]), w_gate[l], w_up[l], w_down[l])
    return x


import jax as _jax
import jax.numpy as _jnp

TWIN_FORMAT = 'train_step'
FWD_PARAMS = ['x', 'attn_norm_w', 'w_in', 'q_norm_w', 'k_norm_w', 'rel_bias', 'conv_w', 'out_norm_w', 'w_out', 'ffn_norm_w', 'w_gate', 'w_up', 'w_down']
TWIN_WEIGHTS = ['attn_norm_w', 'w_in', 'q_norm_w', 'k_norm_w', 'rel_bias', 'conv_w', 'out_norm_w', 'w_out', 'ffn_norm_w', 'w_gate', 'w_up', 'w_down']
TWIN_DIFF_INPUT = 'x'
TWIN_INPUTS = ['x', 'attn_norm_w', 'w_in', 'q_norm_w', 'k_norm_w', 'rel_bias', 'conv_w', 'out_norm_w', 'w_out', 'ffn_norm_w', 'w_gate', 'w_up', 'w_down', 'loss_target', 'm_attn_norm_w', 'm_w_in', 'm_q_norm_w', 'm_k_norm_w', 'm_rel_bias', 'm_conv_w', 'm_out_norm_w', 'm_w_out', 'm_ffn_norm_w', 'm_w_gate', 'm_w_up', 'm_w_down', 'v_attn_norm_w', 'v_w_in', 'v_q_norm_w', 'v_k_norm_w', 'v_rel_bias', 'v_conv_w', 'v_out_norm_w', 'v_w_out', 'v_ffn_norm_w', 'v_w_gate', 'v_w_up', 'v_w_down']
TWIN_OUTPUTS = ['loss', 'grad_x', 'grad_attn_norm_w', 'grad_w_in', 'grad_q_norm_w', 'grad_k_norm_w', 'grad_rel_bias', 'grad_conv_w', 'grad_out_norm_w', 'grad_w_out', 'grad_ffn_norm_w', 'grad_w_gate', 'grad_w_up', 'grad_w_down', 'delta_attn_norm_w', 'delta_w_in', 'delta_q_norm_w', 'delta_k_norm_w', 'delta_rel_bias', 'delta_conv_w', 'delta_out_norm_w', 'delta_w_out', 'delta_ffn_norm_w', 'delta_w_gate', 'delta_w_up', 'delta_w_down', 'new_m_attn_norm_w', 'new_m_w_in', 'new_m_q_norm_w', 'new_m_k_norm_w', 'new_m_rel_bias', 'new_m_conv_w', 'new_m_out_norm_w', 'new_m_w_out', 'new_m_ffn_norm_w', 'new_m_w_gate', 'new_m_w_up', 'new_m_w_down', 'new_v_attn_norm_w', 'new_v_w_in', 'new_v_q_norm_w', 'new_v_k_norm_w', 'new_v_rel_bias', 'new_v_conv_w', 'new_v_out_norm_w', 'new_v_w_out', 'new_v_ffn_norm_w', 'new_v_w_gate', 'new_v_w_up', 'new_v_w_down']
TWIN_LEAF_KINDS = {'loss': 'loss', 'grad_x': 'grad_x', 'grad_attn_norm_w': 'grad_w', 'grad_w_in': 'grad_w', 'grad_q_norm_w': 'grad_w', 'grad_k_norm_w': 'grad_w', 'grad_rel_bias': 'grad_w', 'grad_conv_w': 'grad_w', 'grad_out_norm_w': 'grad_w', 'grad_w_out': 'grad_w', 'grad_ffn_norm_w': 'grad_w', 'grad_w_gate': 'grad_w', 'grad_w_up': 'grad_w', 'grad_w_down': 'grad_w', 'delta_attn_norm_w': 'delta_w', 'delta_w_in': 'delta_w', 'delta_q_norm_w': 'delta_w', 'delta_k_norm_w': 'delta_w', 'delta_rel_bias': 'delta_w', 'delta_conv_w': 'delta_w', 'delta_out_norm_w': 'delta_w', 'delta_w_out': 'delta_w', 'delta_ffn_norm_w': 'delta_w', 'delta_w_gate': 'delta_w', 'delta_w_up': 'delta_w', 'delta_w_down': 'delta_w', 'new_m_attn_norm_w': 'new_m', 'new_m_w_in': 'new_m', 'new_m_q_norm_w': 'new_m', 'new_m_k_norm_w': 'new_m', 'new_m_rel_bias': 'new_m', 'new_m_conv_w': 'new_m', 'new_m_out_norm_w': 'new_m', 'new_m_w_out': 'new_m', 'new_m_ffn_norm_w': 'new_m', 'new_m_w_gate': 'new_m', 'new_m_w_up': 'new_m', 'new_m_w_down': 'new_m', 'new_v_attn_norm_w': 'new_v', 'new_v_w_in': 'new_v', 'new_v_q_norm_w': 'new_v', 'new_v_k_norm_w': 'new_v', 'new_v_rel_bias': 'new_v', 'new_v_conv_w': 'new_v', 'new_v_out_norm_w': 'new_v', 'new_v_w_out': 'new_v', 'new_v_ffn_norm_w': 'new_v', 'new_v_w_gate': 'new_v', 'new_v_w_up': 'new_v', 'new_v_w_down': 'new_v'}


def _forward(args):
    return _fwd_reference(*[args[k] for k in FWD_PARAMS])


def _output_shape():
    def fwd():
        inp = _fwd_setup_inputs(0)
        return _fwd_reference(*[inp[k] for k in FWD_PARAMS])
    out = _jax.eval_shape(fwd)
    return out.shape, out.dtype

N_MICROBATCH = 1
ADAM_LR = 0.001
ADAM_B1 = 0.9
ADAM_B2 = 0.999
ADAM_EPS = 1e-08
ADAM_WD = 0.01
ADAM_STEP = 10
PER_EXAMPLE_BATCH_AXIS = {'x': 0, 'loss_target': 0}
SHARED_INPUTS = []
_WEIGHT_DTYPES = {'attn_norm_w': _jnp.float32, 'w_in': _jnp.float32, 'q_norm_w': _jnp.float32, 'k_norm_w': _jnp.float32, 'rel_bias': _jnp.float32, 'conv_w': _jnp.float32, 'out_norm_w': _jnp.float32, 'w_out': _jnp.float32, 'ffn_norm_w': _jnp.float32, 'w_gate': _jnp.float32, 'w_up': _jnp.float32, 'w_down': _jnp.float32}
MOMENT_SCALE = {'attn_norm_w': 4.351622e+01, 'w_in': 2.554659e+01, 'q_norm_w': 1.101916e+01, 'k_norm_w': 1.118078e+01, 'rel_bias': 4.828635e+00, 'conv_w': 3.420655e+00, 'out_norm_w': 1.448049e+02, 'w_out': 4.572037e+01, 'ffn_norm_w': 9.774858e+01, 'w_gate': 3.241051e+00, 'w_up': 4.545515e+00, 'w_down': 7.320077e+00}


def _to_microbatches(a, axis):
    t = _jnp.moveaxis(a, axis, 0)
    t = t.reshape((N_MICROBATCH, t.shape[0] // N_MICROBATCH) + t.shape[1:])
    return _jnp.moveaxis(t, 1, axis + 1)


def setup_inputs(seed: int = 0) -> dict:
    inp = _fwd_setup_inputs(seed)
    key = _jax.random.fold_in(_jax.random.key(seed), 7919)
    shape, _ = _output_shape()
    out = dict(inp)
    out["loss_target"] = _jax.random.normal(_jax.random.fold_in(key, 0), shape, _jnp.float32)
    for i, name in enumerate(TWIN_WEIGHTS):
        w = inp[name].astype(_jnp.float32)
        if MOMENT_SCALE is None:
            s = _jnp.sqrt(_jnp.mean(_jnp.square(w)) + 1e-30)
        else:
            s = MOMENT_SCALE[name]
        km, kv = _jax.random.split(_jax.random.fold_in(key, i + 1))
        out[name] = w
        out["m_" + name] = s * _jax.random.normal(km, w.shape, _jnp.float32)
        out["v_" + name] = (s * s) * _jax.random.uniform(kv, w.shape, _jnp.float32, 0.5, 1.5)
    if N_MICROBATCH > 1:
        for name, axis in PER_EXAMPLE_BATCH_AXIS.items():
            out[name] = _to_microbatches(out[name], axis)
    return {'x': out['x'], 'attn_norm_w': out['attn_norm_w'], 'w_in': out['w_in'], 'q_norm_w': out['q_norm_w'], 'k_norm_w': out['k_norm_w'], 'rel_bias': out['rel_bias'], 'conv_w': out['conv_w'], 'out_norm_w': out['out_norm_w'], 'w_out': out['w_out'], 'ffn_norm_w': out['ffn_norm_w'], 'w_gate': out['w_gate'], 'w_up': out['w_up'], 'w_down': out['w_down'], 'loss_target': out['loss_target'], 'm_attn_norm_w': out['m_attn_norm_w'], 'm_w_in': out['m_w_in'], 'm_q_norm_w': out['m_q_norm_w'], 'm_k_norm_w': out['m_k_norm_w'], 'm_rel_bias': out['m_rel_bias'], 'm_conv_w': out['m_conv_w'], 'm_out_norm_w': out['m_out_norm_w'], 'm_w_out': out['m_w_out'], 'm_ffn_norm_w': out['m_ffn_norm_w'], 'm_w_gate': out['m_w_gate'], 'm_w_up': out['m_w_up'], 'm_w_down': out['m_w_down'], 'v_attn_norm_w': out['v_attn_norm_w'], 'v_w_in': out['v_w_in'], 'v_q_norm_w': out['v_q_norm_w'], 'v_k_norm_w': out['v_k_norm_w'], 'v_rel_bias': out['v_rel_bias'], 'v_conv_w': out['v_conv_w'], 'v_out_norm_w': out['v_out_norm_w'], 'v_w_out': out['v_w_out'], 'v_ffn_norm_w': out['v_ffn_norm_w'], 'v_w_gate': out['v_w_gate'], 'v_w_up': out['v_w_up'], 'v_w_down': out['v_w_down']}


def _loss(weights, diff, rest, loss_target):
    with _jax.named_scope("forward"):
        args = {**rest, TWIN_DIFF_INPUT: diff, **{k: w.astype(_WEIGHT_DTYPES[k]) for k, w in weights.items()}}
        y = _forward(args)
    with _jax.named_scope("loss_head"):
        err = _jnp.square(y.astype(_jnp.float32) - loss_target)
        return 0.5 * _jnp.sum(_jnp.mean(err, axis=-1)) if err.ndim else 0.5 * err


def _adamw(w, g, m, v):
    m = ADAM_B1 * m + (1.0 - ADAM_B1) * g
    v = ADAM_B2 * v + (1.0 - ADAM_B2) * _jnp.square(g)
    m_hat = m / (1.0 - ADAM_B1 ** ADAM_STEP)
    v_hat = v / (1.0 - ADAM_B2 ** ADAM_STEP)
    delta = -ADAM_LR * (m_hat / (_jnp.sqrt(v_hat) + ADAM_EPS) + ADAM_WD * w)
    return delta, m, v


def reference(x, attn_norm_w, w_in, q_norm_w, k_norm_w, rel_bias, conv_w, out_norm_w, w_out, ffn_norm_w, w_gate, w_up, w_down, loss_target, m_attn_norm_w, m_w_in, m_q_norm_w, m_k_norm_w, m_rel_bias, m_conv_w, m_out_norm_w, m_w_out, m_ffn_norm_w, m_w_gate, m_w_up, m_w_down, v_attn_norm_w, v_w_in, v_q_norm_w, v_k_norm_w, v_rel_bias, v_conv_w, v_out_norm_w, v_w_out, v_ffn_norm_w, v_w_gate, v_w_up, v_w_down):
    given = dict(x=x, attn_norm_w=attn_norm_w, w_in=w_in, q_norm_w=q_norm_w, k_norm_w=k_norm_w, rel_bias=rel_bias, conv_w=conv_w, out_norm_w=out_norm_w, w_out=w_out, ffn_norm_w=ffn_norm_w, w_gate=w_gate, w_up=w_up, w_down=w_down, loss_target=loss_target, m_attn_norm_w=m_attn_norm_w, m_w_in=m_w_in, m_q_norm_w=m_q_norm_w, m_k_norm_w=m_k_norm_w, m_rel_bias=m_rel_bias, m_conv_w=m_conv_w, m_out_norm_w=m_out_norm_w, m_w_out=m_w_out, m_ffn_norm_w=m_ffn_norm_w, m_w_gate=m_w_gate, m_w_up=m_w_up, m_w_down=m_w_down, v_attn_norm_w=v_attn_norm_w, v_w_in=v_w_in, v_q_norm_w=v_q_norm_w, v_k_norm_w=v_k_norm_w, v_rel_bias=v_rel_bias, v_conv_w=v_conv_w, v_out_norm_w=v_out_norm_w, v_w_out=v_w_out, v_ffn_norm_w=v_ffn_norm_w, v_w_gate=v_w_gate, v_w_up=v_w_up, v_w_down=v_w_down)
    weights = {n: given[n] for n in TWIN_WEIGHTS}
    shared = {n: given[n] for n in SHARED_INPUTS}
    per_example = {n: given[n] for n in ['x']}
    grad_fn = _jax.value_and_grad(_loss, argnums=(0, 1))

    def one_microbatch(ex, loss_target):
        ex = dict(ex)
        diff = ex.pop(TWIN_DIFF_INPUT)
        return grad_fn(weights, diff, {**shared, **ex}, loss_target)

    if N_MICROBATCH == 1:
        loss, (grad_w, grad_x) = one_microbatch(per_example, given["loss_target"])
    else:
        def body(carry, xs):
            loss_sum, grad_sum = carry
            l_k, (gw_k, gx_k) = one_microbatch(xs[0], xs[1])
            with _jax.named_scope("update"):
                return (loss_sum + l_k, _jax.tree.map(_jnp.add, grad_sum, gw_k)), gx_k

        init = (_jnp.zeros((), _jnp.float32), _jax.tree.map(_jnp.zeros_like, weights))
        (loss, grad_w), grad_x = _jax.lax.scan(body, init, (per_example, given["loss_target"]))
    with _jax.named_scope("update"):
        delta_w, new_m, new_v = {}, {}, {}
        for n in TWIN_WEIGHTS:
            delta_w[n], new_m[n], new_v[n] = _adamw(weights[n], grad_w[n], given["m_" + n], given["v_" + n])
    return (loss, grad_x, *[grad_w[n] for n in TWIN_WEIGHTS], *[delta_w[n] for n in TWIN_WEIGHTS],
            *[new_m[n] for n in TWIN_WEIGHTS], *[new_v[n] for n in TWIN_WEIGHTS])
```

```python
import numpy as np
import jax
import jax.numpy as jnp
from jax import lax
from jax.experimental import pallas as pl
from jax.experimental.pallas import tpu as pltpu

F32 = jnp.float32
BF16 = jnp.bfloat16

N_DEV = 8
D_MODEL = 1024
HEAD_DIM = 64
LANES = 128
SB_BLOCK = 128
CH_BLOCK = 128
CH_PAD = 512
CH_BAND = 640
REL_CLIP = 128
N_REL = 2 * REL_CLIP + 1
REL_PAD = 384
D_SB, D_CH, D_CV = 256, 512, 256
COL_QA, COL_KA, COL_VA = 0, 256, 512
COL_QB, COL_KB, COL_VB = 768, 1280, 1792
COL_GB, COL_GC, COL_XC = 2304, 2560, 2816
EPS = 1e-6
NEG = -1e30
SCALE = HEAD_DIM ** -0.5

ADAM_LR, ADAM_B1, ADAM_B2, ADAM_EPS, ADAM_WD, ADAM_STEP = 0.001, 0.9, 0.999, 1e-08, 0.01, 10

VMEM_LIMIT_MIB = 56
MM_WEIGHT_TILE_BYTES = 8 * 2 ** 20
MESH_AXES = ("x", "y", "c")


def _cparams(sem=None):
    return pltpu.CompilerParams(dimension_semantics=sem, vmem_limit_bytes=VMEM_LIMIT_MIB * 2 ** 20)


def _dot(a, b):
    return lax.dot_general(a, b, (((1,), (0,)), ((), ())), preferred_element_type=F32)


def _dot_nt(a, b):
    return lax.dot_general(a, b, (((1,), (1,)), ((), ())), preferred_element_type=F32)


def _dot_tn(a, b):
    return lax.dot_general(a, b, (((0,), (0,)), ((), ())), preferred_element_type=F32)


def _tile(n, cap):
    if n <= cap:
        return n
    best = None
    for t in range(128, cap + 1, 128):
        if n % t == 0:
            best = t
    assert best is not None, (n, cap)
    return best


def _cat(vals):
    return vals[0] if len(vals) == 1 else jnp.concatenate(vals, axis=1)


def _mm(pairs, *, nt, name, res=None, out_dtype=BF16):
    m = pairs[0][0].shape[0]
    n = pairs[0][1].shape[0] if nt else pairs[0][1].shape[1]
    tm = min(512, m)
    tn = n
    while sum(a.shape[1] for a, _ in pairs) * tn * 2 > MM_WEIGHT_TILE_BYTES and tn % 256 == 0:
        tn //= 2
    npairs = len(pairs)

    def body(*refs):
        acc = None
        for p in range(npairs):
            a = refs[2 * p][...].astype(BF16)
            b = refs[2 * p + 1][...]
            d = _dot_nt(a, b) if nt else _dot(a, b)
            acc = d if acc is None else acc + d
        if res is not None:
            acc = acc + refs[2 * npairs][...]
        refs[-1][...] = acc.astype(refs[-1].dtype)

    in_specs, args = [], []
    for a, b in pairs:
        k = a.shape[1]
        in_specs.append(pl.BlockSpec((tm, k), lambda j, i: (i, 0)))
        if nt:
            in_specs.append(pl.BlockSpec((tn, k), lambda j, i: (j, 0)))
        else:
            in_specs.append(pl.BlockSpec((k, tn), lambda j, i: (0, j)))
        args += [a, b]
    if res is not None:
        in_specs.append(pl.BlockSpec((tm, tn), lambda j, i: (i, j)))
        args.append(res)
    return pl.pallas_call(
        body, grid=(n // tn, m // tm), in_specs=in_specs,
        out_specs=pl.BlockSpec((tm, tn), lambda j, i: (i, j)),
        out_shape=jax.ShapeDtypeStruct((m, n), out_dtype),
        compiler_params=_cparams(("parallel", "parallel")), name=name)(*args)


def _mm_tn(a, g, *, name):
    s, k = a.shape
    n = g.shape[1]
    tk, tn, ts = _tile(k, 1408), _tile(n, 1536), min(512, s)

    def body(a_ref, g_ref, o_ref):
        @pl.when(pl.program_id(2) == 0)
        def _():
            o_ref[...] = jnp.zeros_like(o_ref)
        o_ref[...] += _dot_tn(a_ref[...].astype(BF16), g_ref[...].astype(BF16))

    return pl.pallas_call(
        body, grid=(k // tk, n // tn, s // ts),
        in_specs=[pl.BlockSpec((ts, tk), lambda i, j, t: (t, i)),
                  pl.BlockSpec((ts, tn), lambda i, j, t: (t, j))],
        out_specs=pl.BlockSpec((tk, tn), lambda i, j, t: (i, j)),
        out_shape=jax.ShapeDtypeStruct((k, n), F32),
        compiler_params=_cparams(("parallel", "parallel", "arbitrary")), name=name)(a, g)


def _block_diag(width, group):
    r = lax.broadcasted_iota(jnp.int32, (width, width), 0) // group
    c = lax.broadcasted_iota(jnp.int32, (width, width), 1) // group
    return (r == c).astype(BF16)


def _group_mean(v, bd, group):
    hi = v.astype(BF16)
    lo = (v - hi.astype(F32)).astype(BF16)
    return (_dot(hi, bd) + _dot(lo, bd)) * (1.0 / group)


def _norm_fwd(xs, w, *, group, name):
    s = xs[0].shape[0]
    width = sum(x.shape[1] for x in xs)
    tm = min(512, s)
    nx = len(xs)
    grouped = group != width

    def body(*refs):
        x = _cat([r[...].astype(F32) for r in refs[:nx]])
        if grouped:
            ms = _group_mean(x * x, refs[nx + 1][...], group)
        else:
            ms = jnp.mean(x * x, axis=1, keepdims=True)
        refs[-1][...] = (x * lax.rsqrt(ms + EPS) * refs[nx][...]).astype(BF16)

    in_specs = [pl.BlockSpec((tm, x.shape[1]), lambda i: (i, 0)) for x in xs]
    in_specs.append(pl.BlockSpec((1, width), lambda i: (0, 0)))
    args = list(xs) + [w]
    if grouped:
        in_specs.append(pl.BlockSpec((width, width), lambda i: (0, 0)))
        args.append(_block_diag(width, group))
    return pl.pallas_call(
        body, grid=(s // tm,), in_specs=in_specs,
        out_specs=pl.BlockSpec((tm, width), lambda i: (i, 0)),
        out_shape=jax.ShapeDtypeStruct((s, width), BF16),
        compiler_params=_cparams(("parallel",)), name=name)(*args)


def _norm_bwd(xs, w, dy, *, group, name, res=None, out_dtype=F32):
    s = xs[0].shape[0]
    width = sum(x.shape[1] for x in xs)
    tm = min(512, s)
    nx = len(xs)
    grouped = group != width
    has_res = res is not None

    def body(*refs):
        x = _cat([r[...].astype(F32) for r in refs[:nx]])
        w_ref, dy_ref = refs[nx], refs[nx + 1]
        pos = nx + 2
        if has_res:
            res_ref = refs[pos]
            pos += 1
        if grouped:
            bd = refs[pos][...]
            mean = lambda v: _group_mean(v, bd, group)
        else:
            mean = lambda v: jnp.mean(v, axis=1, keepdims=True)
        dx_ref, dw_ref = refs[-2], refs[-1]
        inv = lax.rsqrt(mean(x * x) + EPS)
        xh = x * inv
        d = dy_ref[...].astype(F32)
        g = d * w_ref[...]
        dx = inv * (g - xh * mean(g * xh))
        if has_res:
            dx = dx + res_ref[...]
        dx_ref[...] = dx.astype(dx_ref.dtype)

        @pl.when(pl.program_id(0) == 0)
        def _():
            dw_ref[...] = jnp.zeros_like(dw_ref)
        dw_ref[...] += (d * xh).reshape(tm // 8, 8, width).sum(axis=0)

    in_specs = [pl.BlockSpec((tm, x.shape[1]), lambda i: (i, 0)) for x in xs]
    in_specs += [pl.BlockSpec((1, width), lambda i: (0, 0)), pl.BlockSpec((tm, width), lambda i: (i, 0))]
    args = list(xs) + [w, dy]
    if has_res:
        in_specs.append(pl.BlockSpec((tm, width), lambda i: (i, 0)))
        args.append(res)
    if grouped:
        in_specs.append(pl.BlockSpec((width, width), lambda i: (0, 0)))
        args.append(_block_diag(width, group))
    return pl.pallas_call(
        body, grid=(s // tm,), in_specs=in_specs,
        out_specs=(pl.BlockSpec((tm, width), lambda i: (i, 0)), pl.BlockSpec((8, width), lambda i: (0, 0))),
        out_shape=(jax.ShapeDtypeStruct((s, width), out_dtype), jax.ShapeDtypeStruct((8, width), F32)),
        compiler_params=_cparams(("arbitrary",)), name=name)(*args)


def _head0_lanes(shape):
    return lax.broadcasted_iota(jnp.int32, shape, len(shape) - 1) < HEAD_DIM


def _split_heads(v):
    m0 = _head0_lanes(v.shape)
    zero = jnp.zeros_like(v)
    return jnp.where(m0, v, zero), jnp.where(m0, zero, v)


def _pair_sum(v):
    m0 = _head0_lanes(v.shape)
    s0 = jnp.sum(jnp.where(m0, v, 0.0), axis=1, keepdims=True)
    s1 = jnp.sum(jnp.where(m0, 0.0, v), axis=1, keepdims=True)
    return jnp.where(m0, s0, s1)


def _pair_norm(x, w2):
    inv = lax.rsqrt(_pair_sum(x * x) * (1.0 / HEAD_DIM) + EPS)
    xh = x * inv
    return xh * w2, xh, inv


def _pair_norm_bwd(dy, xh, inv, w2):
    g = dy * w2
    return inv * (g - xh * (_pair_sum(g * xh) * (1.0 / HEAD_DIM)))


def _copy_in(copies):
    for c in copies:
        c.start()
    for c in copies:
        c.wait()


def _neg_softplus(z):
    return -(jnp.maximum(z, 0.0) + jnp.log(1.0 + jnp.exp(-jnp.abs(z))))


def _sb_fwd(proj, hp, *, name):
    s = proj.shape[0]
    nq = s // SB_BLOCK
    assert nq <= LANES
    shape = (SB_BLOCK, SB_BLOCK)

    def body(q_ref, proj_hbm, y_ref, c0_ref, c1_ref, k_scr, v_scr, sems):
        i = pl.program_id(0)

        @pl.when(i == 0)
        def _():
            _copy_in([pltpu.make_async_copy(proj_hbm.at[:, pl.ds(COL_KA + hp * LANES, LANES)], k_scr, sems.at[0]),
                      pltpu.make_async_copy(proj_hbm.at[:, pl.ds(COL_VA + hp * LANES, LANES)], v_scr, sems.at[1])])

        row = lax.broadcasted_iota(jnp.int32, shape, 0)
        col = lax.broadcasted_iota(jnp.int32, shape, 1)
        m0 = col < HEAD_DIM
        earlier = col < row
        later_keys = (row > col).astype(BF16)
        qh = _split_heads(q_ref[...])

        def pair(j, carry, diag):
            c, acc, seen = list(carry[:2]), carry[2], list(carry[3:5])
            rows = pl.ds(pl.multiple_of(j * SB_BLOCK, SB_BLOCK), SB_BLOCK)
            k2, v2 = k_scr[rows, :], v_scr[rows, :]
            outs = []
            for h in range(2):
                seen[h] = jnp.where(col == j, c[h], seen[h])
                z = _dot_nt(qh[h], k2) * SCALE
                lr = _neg_softplus(z)
                lrm = jnp.where(earlier, lr, 0.0) if diag else lr
                after = _dot(lrm.astype(BF16), later_keys) + c[h]
                w = jnp.exp(z + lr + after)
                if diag:
                    w = jnp.where(earlier, w, 0.0)
                outs.append(_dot(w.astype(BF16), v2))
                c[h] = c[h] + jnp.sum(lrm, axis=1, keepdims=True)
            return c[0], c[1], acc + jnp.where(m0, outs[0], outs[1]), seen[0], seen[1]

        zc = jnp.zeros((SB_BLOCK, 1), F32)
        zt = jnp.zeros(shape, F32)
        carry = pair(i, (zc, zc, zt, zt, zt), True)
        carry = lax.fori_loop(0, i, lambda t, cr: pair(i - 1 - t, cr, False), carry)
        y_ref[...] = carry[2].astype(y_ref.dtype)
        c0_ref[...] = carry[3]
        c1_ref[...] = carry[4]

    return pl.pallas_call(
        body, grid=(nq,),
        in_specs=[pl.BlockSpec(shape, lambda i: (i, COL_QA // LANES + hp)), pl.BlockSpec(memory_space=pl.ANY)],
        out_specs=(pl.BlockSpec(shape, lambda i: (i, 0)),) * 3,
        out_shape=(jax.ShapeDtypeStruct((s, LANES), BF16), jax.ShapeDtypeStruct((s, LANES), F32),
                   jax.ShapeDtypeStruct((s, LANES), F32)),
        scratch_shapes=[pltpu.VMEM((s, LANES), BF16), pltpu.VMEM((s, LANES), BF16), pltpu.SemaphoreType.DMA((2,))],
        compiler_params=_cparams(("arbitrary",)), name=name)(proj, proj)


def _sb_bwd(proj, carries, dy_all, hp, *, name):
    s = proj.shape[0]
    nq = s // SB_BLOCK
    shape = (SB_BLOCK, SB_BLOCK)

    def body(q_ref, c0_ref, c1_ref, dy_ref, proj_hbm, dq_ref, dk_hbm, dv_hbm, k_scr, v_scr, dk_scr, dv_scr, sems):
        i = pl.program_id(0)

        @pl.when(i == 0)
        def _():
            _copy_in([pltpu.make_async_copy(proj_hbm.at[:, pl.ds(COL_KA + hp * LANES, LANES)], k_scr, sems.at[0]),
                      pltpu.make_async_copy(proj_hbm.at[:, pl.ds(COL_VA + hp * LANES, LANES)], v_scr, sems.at[1])])
            dk_scr[...] = jnp.zeros_like(dk_scr)
            dv_scr[...] = jnp.zeros_like(dv_scr)

        row = lax.broadcasted_iota(jnp.int32, shape, 0)
        col = lax.broadcasted_iota(jnp.int32, shape, 1)
        m0 = col < HEAD_DIM
        earlier = col < row
        later_keys = (row > col).astype(BF16)
        earlier_keys = (row < col).astype(BF16)
        qh = _split_heads(q_ref[...])
        dyh = _split_heads(dy_ref[...])
        seen = (c0_ref[...], c1_ref[...])

        def pair(j, carry, diag):
            r, dq = list(carry[0:2]), list(carry[2:4])
            rows = pl.ds(pl.multiple_of(j * SB_BLOCK, SB_BLOCK), SB_BLOCK)
            k2, v2 = k_scr[rows, :], v_scr[rows, :]
            dkc = jnp.zeros(shape, F32)
            dvc = jnp.zeros(shape, F32)
            for h in range(2):
                c = jnp.sum(jnp.where(col == j, seen[h], 0.0), axis=1, keepdims=True)
                z = _dot_nt(qh[h], k2) * SCALE
                lr = _neg_softplus(z)
                lrm = jnp.where(earlier, lr, 0.0) if diag else lr
                after = _dot(lrm.astype(BF16), later_keys) + c
                w = jnp.exp(z + lr + after)
                if diag:
                    w = jnp.where(earlier, w, 0.0)
                de = _dot_nt(dyh[h], v2) * w
                before = r[h] + _dot(de.astype(BF16), earlier_keys)
                dz = de - jnp.exp(z + lr) * (de + before)
                if diag:
                    dz = jnp.where(earlier, dz, 0.0)
                dzb = (dz * SCALE).astype(BF16)
                dq[h] = dq[h] + _dot(dzb, k2)
                dkc = dkc + _dot_tn(dzb, qh[h])
                dvc = dvc + _dot_tn(w.astype(BF16), dyh[h])
                r[h] = r[h] + jnp.sum(de, axis=1, keepdims=True)
            dk_scr[rows, :] += dkc
            dv_scr[rows, :] += dvc
            return r[0], r[1], dq[0], dq[1]

        zc = jnp.zeros((SB_BLOCK, 1), F32)
        zq = jnp.zeros(shape, F32)
        carry = lax.fori_loop(0, i, lambda t, cr: pair(t, cr, False), (zc, zc, zq, zq))
        carry = pair(i, carry, True)
        dq_ref[...] = jnp.where(m0, carry[2], carry[3]).astype(dq_ref.dtype)

        @pl.when(i == nq - 1)
        def _():
            _copy_in([pltpu.make_async_copy(dk_scr, dk_hbm, sems.at[0]),
                      pltpu.make_async_copy(dv_scr, dv_hbm, sems.at[1])])

    blk = lambda c0: pl.BlockSpec(shape, lambda i: (i, c0))
    return pl.pallas_call(
        body, grid=(nq,),
        in_specs=[blk(COL_QA // LANES + hp), blk(0), blk(0), blk(hp), pl.BlockSpec(memory_space=pl.ANY)],
        out_specs=(blk(0), pl.BlockSpec(memory_space=pl.ANY), pl.BlockSpec(memory_space=pl.ANY)),
        out_shape=(jax.ShapeDtypeStruct((s, LANES), BF16), jax.ShapeDtypeStruct((s, LANES), F32),
                   jax.ShapeDtypeStruct((s, LANES), F32)),
        scratch_shapes=[pltpu.VMEM((s, LANES), BF16), pltpu.VMEM((s, LANES), BF16),
                        pltpu.VMEM((s, LANES), F32), pltpu.VMEM((s, LANES), F32), pltpu.SemaphoreType.DMA((2,))],
        compiler_params=_cparams(("arbitrary",)), name=name)(proj, carries[0], carries[1], dy_all, proj)


def _rel_index():
    p = np.arange(CH_BLOCK)[:, None]
    m = np.arange(CH_BAND)[None, :]
    return (np.clip(CH_PAD + p - m, -REL_CLIP, REL_CLIP) + REL_CLIP).astype(np.int32)


def _bias_expand(rel_bias, *, name):
    nh = rel_bias.shape[0]
    table = jnp.pad(rel_bias, ((0, 0), (0, REL_PAD - N_REL)))

    def body(tab_ref, rel_ref, out_ref):
        rid = lax.broadcasted_iota(jnp.int32, (REL_PAD, CH_BAND), 0)

        def step(p, _):
            onehot = (rid == rel_ref[pl.ds(p, 1), :]).astype(F32)
            out_ref[p] = lax.dot_general(tab_ref[...], onehot, (((1,), (0,)), ((), ())),
                                         precision=lax.Precision.HIGHEST, preferred_element_type=F32)
            return 0

        lax.fori_loop(0, CH_BLOCK, step, 0)

    out = pl.pallas_call(
        body, out_shape=jax.ShapeDtypeStruct((CH_BLOCK, nh, CH_BAND), F32),
        compiler_params=_cparams(), name=name)(table, jnp.asarray(_rel_index()))
    return out.transpose(1, 0, 2)


def _bias_fold(dbias, *, name):
    nh = dbias.shape[0]

    def body(db_ref, rel_ref, out_ref):
        rid = lax.broadcasted_iota(jnp.int32, (REL_PAD, CH_BAND), 0)

        def step(p, acc):
            onehot = (rid == rel_ref[pl.ds(p, 1), :]).astype(F32)
            return acc + lax.dot_general(db_ref[p], onehot, (((1,), (1,)), ((), ())),
                                         precision=lax.Precision.HIGHEST, preferred_element_type=F32)

        out_ref[...] = lax.fori_loop(0, CH_BLOCK, step, jnp.zeros((nh, REL_PAD), F32))

    out = pl.pallas_call(
        body, out_shape=jax.ShapeDtypeStruct((nh, REL_PAD), F32),
        compiler_params=_cparams(), name=name)(dbias.transpose(1, 0, 2), jnp.asarray(_rel_index()))
    return out[:, :N_REL]


def _band_valid(i):
    row = lax.broadcasted_iota(jnp.int32, (CH_BLOCK, CH_BAND), 0)
    col = lax.broadcasted_iota(jnp.int32, (CH_BLOCK, CH_BAND), 1)
    first = row < CH_BLOCK // 2
    window = (first & (col < CH_BAND - 64)) | (jnp.logical_not(first) & (col >= 64))
    return window & (col + i * CH_BLOCK >= CH_PAD)


def _ch_load(proj_hbm, hp, s, kw_ref, kraw_scr, kn_scr, vp_scr, sems):
    _copy_in([pltpu.make_async_copy(proj_hbm.at[:, pl.ds(COL_KB + hp * LANES, LANES)], kraw_scr, sems.at[0]),
              pltpu.make_async_copy(proj_hbm.at[:, pl.ds(COL_VB + hp * LANES, LANES)],
                                    vp_scr.at[pl.ds(CH_PAD, s), :], sems.at[1])])
    kn_scr[0:CH_PAD, :] = jnp.zeros((CH_PAD, LANES), BF16)
    vp_scr[0:CH_PAD, :] = jnp.zeros((CH_PAD, LANES), BF16)
    rb = min(512, s)

    def step(t, _):
        r = pl.multiple_of(t * rb, rb)
        kn = _pair_norm(kraw_scr[pl.ds(r, rb), :].astype(F32), kw_ref[...])[0]
        kn_scr[pl.ds(CH_PAD + r, rb), :] = kn.astype(BF16)
        return 0

    lax.fori_loop(0, s // rb, step, 0)


def _ch_probs(qh, kb, bias, valid):
    sc = jnp.where(valid, _dot_nt(qh, kb) * SCALE + bias, NEG)
    p = jnp.exp(sc - jnp.max(sc, axis=1, keepdims=True))
    return p, 1.0 / jnp.sum(p, axis=1, keepdims=True)


def _ch_fwd(proj, qw2, kw2, bias, hp, *, name):
    s = proj.shape[0]
    nb = s // CH_BLOCK
    shape = (CH_BLOCK, LANES)

    def body(q_ref, proj_hbm, qw_ref, kw_ref, bias_ref, y_ref, kraw_scr, kn_scr, vp_scr, sems):
        i = pl.program_id(0)

        @pl.when(i == 0)
        def _():
            _ch_load(proj_hbm, hp, s, kw_ref, kraw_scr, kn_scr, vp_scr, sems)

        band = pl.ds(pl.multiple_of(i * CH_BLOCK, CH_BLOCK), CH_BAND)
        kb, vb = kn_scr[band, :], vp_scr[band, :]
        valid = _band_valid(i)
        qn = _pair_norm(q_ref[...].astype(F32), qw_ref[...])[0]
        qh = _split_heads(qn.astype(BF16))
        outs = []
        for h in range(2):
            p, rl = _ch_probs(qh[h], kb, bias_ref[h], valid)
            outs.append(_dot(p.astype(BF16), vb) * rl)
        y_ref[...] = jnp.where(_head0_lanes(shape), outs[0], outs[1]).astype(y_ref.dtype)

    return pl.pallas_call(
        body, grid=(nb,),
        in_specs=[pl.BlockSpec(shape, lambda i: (i, COL_QB // LANES + hp)), pl.BlockSpec(memory_space=pl.ANY),
                  pl.BlockSpec((1, LANES), lambda i: (0, 0)), pl.BlockSpec((1, LANES), lambda i: (0, 0)),
                  pl.BlockSpec((2, CH_BLOCK, CH_BAND), lambda i: (hp, 0, 0))],
        out_specs=pl.BlockSpec(shape, lambda i: (i, 0)),
        out_shape=jax.ShapeDtypeStruct((s, LANES), BF16),
        scratch_shapes=[pltpu.VMEM((s, LANES), BF16), pltpu.VMEM((s + CH_PAD, LANES), BF16),
                        pltpu.VMEM((s + CH_PAD, LANES), BF16), pltpu.SemaphoreType.DMA((2,))],
        compiler_params=_cparams(("arbitrary",)), name=name)(proj, proj, qw2, kw2, bias)


def _ch_bwd(proj, dy_all, qw2, kw2, bias, hp, *, name):
    s = proj.shape[0]
    nb = s // CH_BLOCK
    shape = (CH_BLOCK, LANES)
    rb = min(512, s)

    def body(q_ref, dy_ref, proj_hbm, qw_ref, kw_ref, bias_ref, dq_ref, dk_hbm, dv_hbm, dbias_ref, dqw_ref, dkw_ref,
             kraw_scr, kn_scr, vp_scr, dkn_scr, dvp_scr, sems):
        i = pl.program_id(0)

        @pl.when(i == 0)
        def _():
            _ch_load(proj_hbm, hp, s, kw_ref, kraw_scr, kn_scr, vp_scr, sems)
            dkn_scr[...] = jnp.zeros_like(dkn_scr)
            dvp_scr[...] = jnp.zeros_like(dvp_scr)
            dbias_ref[...] = jnp.zeros_like(dbias_ref)
            dqw_ref[...] = jnp.zeros_like(dqw_ref)

        band = pl.ds(pl.multiple_of(i * CH_BLOCK, CH_BLOCK), CH_BAND)
        kb, vb = kn_scr[band, :], vp_scr[band, :]
        valid = _band_valid(i)
        qn, qhat, qinv = _pair_norm(q_ref[...].astype(F32), qw_ref[...])
        qh = _split_heads(qn.astype(BF16))
        dyh = _split_heads(dy_ref[...])
        dqn = []
        dkc = jnp.zeros((CH_BAND, LANES), F32)
        dvc = jnp.zeros((CH_BAND, LANES), F32)
        for h in range(2):
            p, rl = _ch_probs(qh[h], kb, bias_ref[h], valid)
            p = p * rl
            dp = _dot_nt(dyh[h], vb)
            ds = p * (dp - jnp.sum(dp * p, axis=1, keepdims=True))
            dbias_ref[h] += ds
            dsb = (ds * SCALE).astype(BF16)
            dqn.append(_dot(dsb, kb))
            dkc = dkc + _dot_tn(dsb, qh[h])
            dvc = dvc + _dot_tn(p.astype(BF16), dyh[h])
        dkn_scr[band, :] += dkc
        dvp_scr[band, :] += dvc
        dqn = jnp.where(_head0_lanes(shape), dqn[0], dqn[1])
        dq_ref[...] = _pair_norm_bwd(dqn, qhat, qinv, qw_ref[...]).astype(dq_ref.dtype)
        dqw_ref[...] += (dqn * qhat).reshape(CH_BLOCK // 8, 8, LANES).sum(axis=0)

        @pl.when(i == nb - 1)
        def _():
            def step(t, acc):
                r = pl.multiple_of(t * rb, rb)
                rows = pl.ds(CH_PAD + r, rb)
                _, khat, kinv = _pair_norm(kraw_scr[pl.ds(r, rb), :].astype(F32), kw_ref[...])
                dkn = dkn_scr[rows, :]
                dkn_scr[rows, :] = _pair_norm_bwd(dkn, khat, kinv, kw_ref[...])
                return acc + (dkn * khat).reshape(rb // 8, 8, LANES).sum(axis=0)

            dkw_ref[...] = lax.fori_loop(0, s // rb, step, jnp.zeros((8, LANES), F32))
            _copy_in([pltpu.make_async_copy(dkn_scr.at[pl.ds(CH_PAD, s), :], dk_hbm, sems.at[0]),
                      pltpu.make_async_copy(dvp_scr.at[pl.ds(CH_PAD, s), :], dv_hbm, sems.at[1])])

    blk = lambda c0: pl.BlockSpec(shape, lambda i: (i, c0))
    vec = pl.BlockSpec((1, LANES), lambda i: (0, 0))
    part = pl.BlockSpec((8, LANES), lambda i: (0, 0))
    hbm = pl.BlockSpec(memory_space=pl.ANY)
    return pl.pallas_call(
        body, grid=(nb,),
        in_specs=[blk(COL_QB // LANES + hp), blk(D_SB // LANES + hp), hbm, vec, vec,
                  pl.BlockSpec((2, CH_BLOCK, CH_BAND), lambda i: (hp, 0, 0))],
        out_specs=(blk(0), hbm, hbm, pl.BlockSpec((2, CH_BLOCK, CH_BAND), lambda i: (0, 0, 0)), part, part),
        out_shape=(jax.ShapeDtypeStruct((s, LANES), BF16), jax.ShapeDtypeStruct((s, LANES), F32),
                   jax.ShapeDtypeStruct((s, LANES), F32), jax.ShapeDtypeStruct((2, CH_BLOCK, CH_BAND), F32),
                   jax.ShapeDtypeStruct((8, LANES), F32), jax.ShapeDtypeStruct((8, LANES), F32)),
        scratch_shapes=[pltpu.VMEM((s, LANES), BF16), pltpu.VMEM((s + CH_PAD, LANES), BF16),
                        pltpu.VMEM((s + CH_PAD, LANES), BF16), pltpu.VMEM((s + CH_PAD, LANES), F32),
                        pltpu.VMEM((s + CH_PAD, LANES), F32), pltpu.SemaphoreType.DMA((2,))],
        compiler_params=_cparams(("arbitrary",)), name=name)(proj, dy_all, proj, qw2, kw2, bias)


def _shift_down(x, halo, k):
    out = pltpu.roll(x, k, 0)
    row = lax.broadcasted_iota(jnp.int32, x.shape, 0)
    for t in range(k):
        out = jnp.where(row == t, halo[8 - k + t:8 - k + t + 1, :], out)
    return out


def _shift_up(x, halo, k):
    n = x.shape[0]
    out = pltpu.roll(x, n - k, 0)
    row = lax.broadcasted_iota(jnp.int32, x.shape, 0)
    for t in range(k):
        out = jnp.where(row == n - k + t, halo[t:t + 1, :], out)
    return out


def _conv_specs(s, tm):
    cw = D_CV
    tile = lambda c0: pl.BlockSpec((tm, cw), lambda i: (i, c0))
    above = lambda c0: pl.BlockSpec((8, cw), lambda i: (jnp.maximum(i * (tm // 8) - 1, 0), c0))
    below = lambda c0: pl.BlockSpec((8, cw), lambda i: (jnp.minimum((i + 1) * (tm // 8), s // 8 - 1), c0))
    return tile, above, below


def _conv_fwd(proj, w8, *, name):
    s = proj.shape[0]
    tm = min(512, s)
    tile, above, _ = _conv_specs(s, tm)

    def body(gb_ref, gc_ref, xc_ref, gca_ref, xca_ref, w_ref, y_ref):
        hc = gc_ref[...].astype(F32) * xc_ref[...].astype(F32)
        top = jnp.where(pl.program_id(0) > 0, gca_ref[...].astype(F32) * xca_ref[...].astype(F32), 0.0)
        u = w_ref[0:1, :] * _shift_down(hc, top, 2) + w_ref[1:2, :] * _shift_down(hc, top, 1) + w_ref[2:3, :] * hc
        y_ref[...] = (gb_ref[...].astype(F32) * u).astype(y_ref.dtype)

    cb, cc, cx = COL_GB // D_CV, COL_GC // D_CV, COL_XC // D_CV
    return pl.pallas_call(
        body, grid=(s // tm,),
        in_specs=[tile(cb), tile(cc), tile(cx), above(cc), above(cx), pl.BlockSpec((8, D_CV), lambda i: (0, 0))],
        out_specs=pl.BlockSpec((tm, D_CV), lambda i: (i, 0)),
        out_shape=jax.ShapeDtypeStruct((s, D_CV), BF16),
        compiler_params=_cparams(("parallel",)), name=name)(proj, proj, proj, proj, proj, w8)


def _conv_bwd(proj, dy_all, w8, *, name):
    s = proj.shape[0]
    tm = min(512, s)
    nt = s // tm
    tile, above, below = _conv_specs(s, tm)

    def body(gb_ref, gc_ref, xc_ref, dy_ref, gca_ref, xca_ref, gbb_ref, dyb_ref, w_ref,
             dgb_ref, dgc_ref, dxc_ref, dw_ref):
        i = pl.program_id(0)
        gb, gc, xc = gb_ref[...].astype(F32), gc_ref[...].astype(F32), xc_ref[...].astype(F32)
        dy = dy_ref[...].astype(F32)
        hc = gc * xc
        top = jnp.where(i > 0, gca_ref[...].astype(F32) * xca_ref[...].astype(F32), 0.0)
        hc1, hc2 = _shift_down(hc, top, 1), _shift_down(hc, top, 2)
        u = w_ref[0:1, :] * hc2 + w_ref[1:2, :] * hc1 + w_ref[2:3, :] * hc
        du = dy * gb
        bottom = jnp.where(i < nt - 1, dyb_ref[...].astype(F32) * gbb_ref[...].astype(F32), 0.0)
        dhc = w_ref[2:3, :] * du + w_ref[1:2, :] * _shift_up(du, bottom, 1) + w_ref[0:1, :] * _shift_up(du, bottom, 2)
        dgb_ref[...] = (dy * u).astype(dgb_ref.dtype)
        dgc_ref[...] = (dhc * xc).astype(dgc_ref.dtype)
        dxc_ref[...] = (dhc * gc).astype(dxc_ref.dtype)

        @pl.when(i == 0)
        def _():
            dw_ref[...] = jnp.zeros_like(dw_ref)
        dw_ref[0:1, :] += jnp.sum(du * hc2, axis=0, keepdims=True)
        dw_ref[1:2, :] += jnp.sum(du * hc1, axis=0, keepdims=True)
        dw_ref[2:3, :] += jnp.sum(du * hc, axis=0, keepdims=True)

    cb, cc, cx = COL_GB // D_CV, COL_GC // D_CV, COL_XC // D_CV
    cy = (D_SB + D_CH) // D_CV
    out_tile = pl.BlockSpec((tm, D_CV), lambda i: (i, 0))
    act = jax.ShapeDtypeStruct((s, D_CV), BF16)
    return pl.pallas_call(
        body, grid=(nt,),
        in_specs=[tile(cb), tile(cc), tile(cx), tile(cy), above(cc), above(cx), below(cb), below(cy),
                  pl.BlockSpec((8, D_CV), lambda i: (0, 0))],
        out_specs=(out_tile, out_tile, out_tile, pl.BlockSpec((8, D_CV), lambda i: (0, 0))),
        out_shape=(act, act, act, jax.ShapeDtypeStruct((8, D_CV), F32)),
        compiler_params=_cparams(("arbitrary",)), name=name)(proj, proj, proj, dy_all, proj, proj, proj, dy_all, w8)


def _ffn_up(h, wg, wu, *, name):
    s, d = h.shape
    f = wg.shape[1]
    tm, tn = min(512, s), _tile(f, 1408)

    def body(h_ref, wg_ref, wu_ref, g_ref, u_ref, a_ref):
        hv = h_ref[...]
        g = _dot(hv, wg_ref[...])
        u = _dot(hv, wu_ref[...])
        g_ref[...] = g.astype(BF16)
        u_ref[...] = u.astype(BF16)
        a_ref[...] = (g * jax.nn.sigmoid(g) * u).astype(BF16)

    wspec = pl.BlockSpec((d, tn), lambda j, i: (0, j))
    ospec = pl.BlockSpec((tm, tn), lambda j, i: (i, j))
    act = jax.ShapeDtypeStruct((s, f), BF16)
    return pl.pallas_call(
        body, grid=(f // tn, s // tm),
        in_specs=[pl.BlockSpec((tm, d), lambda j, i: (i, 0)), wspec, wspec],
        out_specs=(ospec, ospec, ospec), out_shape=(act, act, act),
        compiler_params=_cparams(("parallel", "parallel")), name=name)(h, wg, wu)


def _ffn_dact(dx, wd, g, u, *, name):
    s, d = dx.shape
    f = wd.shape[0]
    tm, tf = min(512, s), _tile(f, 1408)

    def body(dx_ref, wd_ref, g_ref, u_ref, dg_ref, du_ref):
        da = _dot_nt(dx_ref[...].astype(BF16), wd_ref[...])
        gv, uv = g_ref[...].astype(F32), u_ref[...].astype(F32)
        sg = jax.nn.sigmoid(gv)
        dg_ref[...] = (da * uv * sg * (1.0 + gv * (1.0 - sg))).astype(BF16)
        du_ref[...] = (da * gv * sg).astype(BF16)

    tspec = pl.BlockSpec((tm, tf), lambda j, i: (i, j))
    act = jax.ShapeDtypeStruct((s, f), BF16)
    return pl.pallas_call(
        body, grid=(f // tf, s // tm),
        in_specs=[pl.BlockSpec((tm, d), lambda j, i: (i, 0)), pl.BlockSpec((tf, d), lambda j, i: (j, 0)), tspec, tspec],
        out_specs=(tspec, tspec), out_shape=(act, act),
        compiler_params=_cparams(("parallel", "parallel")), name=name)(dx, wd, g, u)


def _loss_head(y, target, *, name):
    s, d = y.shape
    tm = min(512, s)

    def body(y_ref, t_ref, dy_ref, sq_ref):
        err = y_ref[...] - t_ref[...]
        dy_ref[...] = err * (1.0 / d)

        @pl.when(pl.program_id(0) == 0)
        def _():
            sq_ref[...] = jnp.zeros_like(sq_ref)
        sq_ref[...] += (err * err).reshape(tm // 8, 8, d).sum(axis=0)

    tile = pl.BlockSpec((tm, d), lambda i: (i, 0))
    return pl.pallas_call(
        body, grid=(s // tm,), in_specs=[tile, tile],
        out_specs=(tile, pl.BlockSpec((8, d), lambda i: (0, 0))),
        out_shape=(jax.ShapeDtypeStruct((s, d), F32), jax.ShapeDtypeStruct((8, d), F32)),
        compiler_params=_cparams(("arbitrary",)), name=name)(y, target)


def _adamw(parts, w, m, v, *, name):
    r, c = w.shape
    tr = r if r <= 512 else _tile_rows(r)

    def body(p_ref, w_ref, m_ref, v_ref, g_ref, d_ref, nm_ref, nv_ref):
        g = p_ref[0]
        for dev in range(1, N_DEV):
            g = g + p_ref[dev]
        nm = ADAM_B1 * m_ref[...] + (1.0 - ADAM_B1) * g
        nv = ADAM_B2 * v_ref[...] + (1.0 - ADAM_B2) * (g * g)
        m_hat = nm / (1.0 - ADAM_B1 ** ADAM_STEP)
        v_hat = nv / (1.0 - ADAM_B2 ** ADAM_STEP)
        g_ref[...] = g
        d_ref[...] = -ADAM_LR * (m_hat / (jnp.sqrt(v_hat) + ADAM_EPS) + ADAM_WD * w_ref[...])
        nm_ref[...] = nm
        nv_ref[...] = nv

    tile = pl.BlockSpec((tr, c), lambda i: (i, 0))
    out = jax.ShapeDtypeStruct((r, c), F32)
    return pl.pallas_call(
        body, grid=(r // tr,),
        in_specs=[pl.BlockSpec((N_DEV, tr, c), lambda i: (0, i, 0)), tile, tile, tile],
        out_specs=(tile, tile, tile, tile), out_shape=(out, out, out, out),
        compiler_params=_cparams(("parallel",)), name=name)(parts, w, m, v)


def _tile_rows(r):
    for t in (512, 256, 128, 64, 32, 16, 8):
        if r % t == 0:
            return t
    return r


def _peer(k):
    x, y, c = (lax.axis_index(a) for a in MESH_AXES)
    px = 1 - x if k & 4 else x
    py = 1 - y if k & 2 else y
    pc = 1 - c if k & 1 else c
    return (px, py, pc), 4 * px + 2 * py + pc


def _exchange(src_of, dst_of, local_sem, send_sems, recv_sems):
    x, y, c = (lax.axis_index(a) for a in MESH_AXES)
    me = 4 * x + 2 * y + c
    mine = pltpu.make_async_copy(src_of(me), dst_of(me), local_sem)
    mine.start()
    sends = []
    for k in range(1, N_DEV):
        dev, idx = _peer(k)
        cp = pltpu.make_async_remote_copy(src_ref=src_of(idx), dst_ref=dst_of(me), send_sem=send_sems.at[k - 1],
                                          recv_sem=recv_sems.at[k - 1], device_id=dev,
                                          device_id_type=pl.DeviceIdType.MESH)
        cp.start()
        sends.append(cp)
    for k in range(1, N_DEV):
        dev, idx = _peer(k)
        pltpu.make_async_remote_copy(src_ref=src_of(idx), dst_ref=dst_of(idx), send_sem=send_sems.at[k - 1],
                                     recv_sem=recv_sems.at[k - 1], device_id=dev,
                                     device_id_type=pl.DeviceIdType.MESH).wait_recv()
    for cp in sends:
        cp.wait_send()
    mine.wait()


def _comm_call(body, out_shape, space, name, *args):
    spec = pl.BlockSpec(memory_space=space)
    return pl.pallas_call(
        body, in_specs=[spec] * len(args), out_specs=spec, out_shape=out_shape,
        scratch_shapes=[pltpu.SemaphoreType.DMA, pltpu.SemaphoreType.DMA((N_DEV - 1,)),
                        pltpu.SemaphoreType.DMA((N_DEV - 1,))],
        compiler_params=pltpu.CompilerParams(vmem_limit_bytes=VMEM_LIMIT_MIB * 2 ** 20), name=name)(*args)


def _all_gather(shard, *, name, space=pl.ANY):
    def body(x_ref, out_ref, local_sem, send_sems, recv_sems):
        _exchange(lambda d: x_ref, lambda d: out_ref.at[d], local_sem, send_sems, recv_sems)

    return _comm_call(body, jax.ShapeDtypeStruct((N_DEV,) + shard.shape, shard.dtype), space, name, shard)


def _all_to_all(slabs, *, name):
    def body(x_ref, out_ref, local_sem, send_sems, recv_sems):
        _exchange(lambda d: x_ref.at[d], lambda d: out_ref.at[d], local_sem, send_sems, recv_sems)

    return _comm_call(body, jax.ShapeDtypeStruct(slabs.shape, slabs.dtype), pl.ANY, name, slabs)


def _cols_from_dev(g, depth):
    rows = g.shape[1] // depth
    return g.reshape(N_DEV, depth, rows, g.shape[2]).transpose(1, 2, 0, 3).reshape(depth, rows, -1)


def _rows_from_dev(g, depth):
    r = g.shape[1] // depth
    return g.reshape(N_DEV, depth, r, g.shape[2]).transpose(1, 0, 2, 3).reshape(depth, N_DEV * r, g.shape[2])


def _cols_to_dev(g):
    depth, rows, n = g.shape
    return g.reshape(depth, rows, N_DEV, n // N_DEV).transpose(2, 0, 1, 3).reshape(N_DEV, depth * rows, n // N_DEV)


def _rows_to_dev(g):
    depth, rows, n = g.shape
    return g.reshape(depth, N_DEV, rows // N_DEV, n).transpose(1, 0, 2, 3).reshape(N_DEV, -1, n)


def _pack(arrays):
    flat = jnp.concatenate([a.reshape(-1) for a in arrays])
    pad = (-flat.shape[0]) % (8 * LANES)
    return jnp.pad(flat, (0, pad)).reshape(-1, LANES)


def _unpack(buf, shapes):
    flat = buf.reshape(-1)
    out, pos = [], 0
    for shp in shapes:
        n = int(np.prod(shp))
        out.append(flat[pos:pos + n].reshape(shp))
        pos += n
    return out


def _fold_heads(part8):
    t = part8.sum(axis=(0, 1))
    return t[:HEAD_DIM] + t[HEAD_DIM:]


def _forward_backward(x, target, wts):
    depth = wts["w_in"].shape[0]
    row = lambda a: a.reshape(1, -1)
    saved = []
    for l in range(depth):
        t = f"l{l}_"
        h = _norm_fwd([x], row(wts["attn_norm_w"][l]), group=D_MODEL, name=t + "attn_norm")
        proj = _mm([(h, wts["w_in"][l])], nt=False, name=t + "proj")
        qw2 = row(jnp.tile(wts["q_norm_w"][l], 2))
        kw2 = row(jnp.tile(wts["k_norm_w"][l], 2))
        bias = _bias_expand(wts["rel_bias"][l], name=t + "bias")
        w8 = jnp.pad(wts["conv_w"][l], ((0, 5), (0, 0)))
        sb = [_sb_fwd(proj, hp, name=t + f"sb{hp}") for hp in range(D_SB // LANES)]
        ys = [o[0] for o in sb]
        sb_carries = [o[1:] for o in sb]
        ys += [_ch_fwd(proj, qw2, kw2, bias, hp, name=t + f"ch{hp}") for hp in range(D_CH // LANES)]
        ys.append(_conv_fwd(proj, w8, name=t + "conv"))
        yn = _norm_fwd(ys, row(wts["out_norm_w"][l]), group=HEAD_DIM, name=t + "out_norm")
        x_mid = _mm([(yn, wts["w_out"][l])], nt=False, res=x, out_dtype=F32, name=t + "out_proj")
        h2 = _norm_fwd([x_mid], row(wts["ffn_norm_w"][l]), group=D_MODEL, name=t + "ffn_norm")
        g, u, a = _ffn_up(h2, wts["w_gate"][l], wts["w_up"][l], name=t + "ffn_up")
        x_out = _mm([(a, wts["w_down"][l])], nt=False, res=x_mid, out_dtype=F32, name=t + "ffn_down")
        saved.append(dict(x=x, h=h, proj=proj, sb_carries=sb_carries, qw2=qw2, kw2=kw2, bias=bias, w8=w8, ys=ys, yn=yn, x_mid=x_mid,
                          h2=h2, g=g, u=u, a=a))
        x = x_out

    dx, sq = _loss_head(x, target, name="loss_head")
    grads = {k: [None] * depth for k in
             ("attn_norm_w", "w_in", "q_norm_w", "k_norm_w", "rel_bias", "conv_w", "out_norm_w", "w_out",
              "ffn_norm_w", "w_gate", "w_up", "w_down")}
    for l in reversed(range(depth)):
        t = f"l{l}_b_"
        sv = saved[l]
        grads["w_down"][l] = _mm_tn(sv["a"], dx, name=t + "w_down")
        dg, du = _ffn_dact(dx, wts["w_down"][l], sv["g"], sv["u"], name=t + "ffn_dact")
        grads["w_gate"][l] = _mm_tn(sv["h2"], dg, name=t + "w_gate")
        grads["w_up"][l] = _mm_tn(sv["h2"], du, name=t + "w_up")
        dh2 = _mm([(dg, wts["w_gate"][l]), (du, wts["w_up"][l])], nt=True, name=t + "ffn_dh")
        dx_mid, dw8 = _norm_bwd([sv["x_mid"]], row(wts["ffn_norm_w"][l]), dh2, group=D_MODEL, res=dx,
                                name=t + "ffn_norm")
        grads["ffn_norm_w"][l] = dw8.sum(axis=0)
        grads["w_out"][l] = _mm_tn(sv["yn"], dx_mid, name=t + "w_out")
        dyn = _mm([(dx_mid, wts["w_out"][l])], nt=True, name=t + "out_dy")
        dy, dw8 = _norm_bwd(sv["ys"], row(wts["out_norm_w"][l]), dyn, group=HEAD_DIM, out_dtype=BF16,
                            name=t + "out_norm")
        grads["out_norm_w"][l] = dw8.sum(axis=0)
        proj = sv["proj"]
        n_sb, n_ch = D_SB // LANES, D_CH // LANES
        sb = [_sb_bwd(proj, sv["sb_carries"][hp], dy, hp, name=t + f"sb{hp}") for hp in range(n_sb)]
        ch = [_ch_bwd(proj, dy, sv["qw2"], sv["kw2"], sv["bias"], hp, name=t + f"ch{hp}") for hp in range(n_ch)]
        dgb, dgc, dxc, dcw = _conv_bwd(proj, dy, sv["w8"], name=t + "conv")
        grads["conv_w"][l] = dcw[:3]
        grads["q_norm_w"][l] = _fold_heads(jnp.stack([c[4] for c in ch]))
        grads["k_norm_w"][l] = _fold_heads(jnp.stack([c[5] for c in ch]))
        grads["rel_bias"][l] = _bias_fold(jnp.concatenate([c[3] for c in ch], axis=0), name=t + "bias")
        pieces = ([o[0] for o in sb] + [o[1] for o in sb] + [o[2] for o in sb]
                  + [o[0] for o in ch] + [o[1] for o in ch] + [o[2] for o in ch] + [dgb, dgc, dxc])
        dproj = jnp.concatenate([p.astype(BF16) for p in pieces], axis=1)
        grads["w_in"][l] = _mm_tn(sv["h"], dproj, name=t + "w_in")
        dh = _mm([(dproj, wts["w_in"][l])], nt=True, name=t + "proj_dh")
        dx, dw8 = _norm_bwd([sv["x"]], row(wts["attn_norm_w"][l]), dh, group=D_MODEL, res=dx_mid,
                            name=t + "attn_norm")
        grads["attn_norm_w"][l] = dw8.sum(axis=0)
    return sq, dx, {k: jnp.stack(v) for k, v in grads.items()}


SMALL = ("attn_norm_w", "q_norm_w", "k_norm_w", "rel_bias", "conv_w", "out_norm_w", "ffn_norm_w")
ORDER = ("attn_norm_w", "w_in", "q_norm_w", "k_norm_w", "rel_bias", "conv_w", "out_norm_w", "w_out",
         "ffn_norm_w", "w_gate", "w_up", "w_down")


def kernel(x, attn_norm_w, w_in, q_norm_w, k_norm_w, rel_bias, conv_w, out_norm_w, w_out, ffn_norm_w, w_gate, w_up, w_down, loss_target, m_attn_norm_w, m_w_in, m_q_norm_w, m_k_norm_w, m_rel_bias, m_conv_w, m_out_norm_w, m_w_out, m_ffn_norm_w, m_w_gate, m_w_up, m_w_down, v_attn_norm_w, v_w_in, v_q_norm_w, v_k_norm_w, v_rel_bias, v_conv_w, v_out_norm_w, v_w_out, v_ffn_norm_w, v_w_gate, v_w_up, v_w_down):
    w = dict(attn_norm_w=attn_norm_w, w_in=w_in, q_norm_w=q_norm_w, k_norm_w=k_norm_w, rel_bias=rel_bias,
             conv_w=conv_w, out_norm_w=out_norm_w, w_out=w_out, ffn_norm_w=ffn_norm_w, w_gate=w_gate, w_up=w_up,
             w_down=w_down)
    mom = dict(attn_norm_w=m_attn_norm_w, w_in=m_w_in, q_norm_w=m_q_norm_w, k_norm_w=m_k_norm_w, rel_bias=m_rel_bias,
               conv_w=m_conv_w, out_norm_w=m_out_norm_w, w_out=m_w_out, ffn_norm_w=m_ffn_norm_w, w_gate=m_w_gate,
               w_up=m_w_up, w_down=m_w_down)
    var = dict(attn_norm_w=v_attn_norm_w, w_in=v_w_in, q_norm_w=v_q_norm_w, k_norm_w=v_k_norm_w, rel_bias=v_rel_bias,
               conv_w=v_conv_w, out_norm_w=v_out_norm_w, w_out=v_w_out, ffn_norm_w=v_ffn_norm_w, w_gate=v_w_gate,
               w_up=v_w_up, w_down=v_w_down)
    depth = w_in.shape[0]
    seq = x.shape[1]
    me = 4 * lax.axis_index("x") + 2 * lax.axis_index("y") + lax.axis_index("c")
    flat2 = lambda a: a.reshape(-1, a.shape[-1])

    full = {k: w[k] for k in SMALL if k != "conv_w"}
    for k in ("w_in", "w_gate", "w_up"):
        full[k] = _cols_from_dev(_all_gather(flat2(w[k]).astype(BF16), name="gather_" + k), depth)
    for k in ("w_out", "w_down"):
        full[k] = _rows_from_dev(_all_gather(flat2(w[k]).astype(BF16), name="gather_" + k), depth)
    conv_shard = conv_w.shape[2]
    conv_all = _all_gather(_pack([conv_w]), name="gather_conv_w", space=pltpu.VMEM)
    conv_all = conv_all.reshape(N_DEV, -1)[:, :depth * 3 * conv_shard].reshape(N_DEV, depth, 3, conv_shard)
    full["conv_w"] = conv_all.transpose(1, 2, 0, 3).reshape(depth, 3, N_DEV * conv_shard)

    sq, dx, grads = _forward_backward(x.reshape(seq, D_MODEL), loss_target.reshape(seq, D_MODEL), full)
    loss = lax.psum(0.5 / D_MODEL * jnp.sum(sq), MESH_AXES)

    outs = {}
    for k in ("w_in", "w_gate", "w_up", "w_out", "w_down"):
        slabs = _cols_to_dev(grads[k]) if k in ("w_in", "w_gate", "w_up") else _rows_to_dev(grads[k])
        parts = _all_to_all(slabs, name="scatter_" + k)
        res = _adamw(parts, flat2(w[k]), flat2(mom[k]), flat2(var[k]), name="adamw_" + k)
        outs[k] = [r.reshape(w[k].shape) for r in res]

    shapes = [grads[k].shape for k in SMALL]
    col0 = me * conv_shard

    def widen(a):
        return lax.dynamic_update_slice(jnp.zeros((depth, 3, N_DEV * conv_shard), F32), a, (0, 0, col0))

    parts = _all_gather(_pack([grads[k] for k in SMALL]), name="gather_small_grads", space=pltpu.VMEM)
    pw = _pack([widen(w[k]) if k == "conv_w" else w[k] for k in SMALL])
    pm = _pack([widen(mom[k]) if k == "conv_w" else mom[k] for k in SMALL])
    pv = _pack([widen(var[k]) if k == "conv_w" else var[k] for k in SMALL])
    res = [_unpack(r, shapes) for r in _adamw(parts, pw, pm, pv, name="adamw_small")]
    for idx, k in enumerate(SMALL):
        vals = [r[idx] for r in res]
        if k == "conv_w":
            vals = [lax.dynamic_slice(a, (0, 0, col0), (depth, 3, conv_shard)) for a in vals]
        outs[k] = vals

    result = [loss, dx.reshape(x.shape)]
    for part in range(4):
        result += [outs[k][part] for k in ORDER]
    return tuple(result)
```

```python
import numpy as np
import jax
import jax.numpy as jnp
from jax import lax
from jax.experimental import pallas as pl
from jax.experimental.pallas import tpu as pltpu

F32 = jnp.float32
BF16 = jnp.bfloat16

N_DEV = 8
D_MODEL = 1024
HEAD_DIM = 64
LANES = 128
SB_BLOCK = 128
SB_GROUP = 4
CH_BLOCK = 128
CH_PAD = 512
CH_BAND = 640
REL_CLIP = 128
N_REL = 2 * REL_CLIP + 1
REL_PAD = 384
D_SB, D_CH, D_CV = 256, 512, 256
COL_QA, COL_KA, COL_VA = 0, 256, 512
COL_QB, COL_KB, COL_VB = 768, 1280, 1792
COL_GB, COL_GC, COL_XC = 2304, 2560, 2816
EPS = 1e-6
NEG = -1e30
SCALE = HEAD_DIM ** -0.5

ADAM_LR, ADAM_B1, ADAM_B2, ADAM_EPS, ADAM_WD, ADAM_STEP = 0.001, 0.9, 0.999, 1e-08, 0.01, 10

VMEM_LIMIT_MIB = 56
MM_WEIGHT_TILE_BYTES = 8 * 2 ** 20
MESH_AXES = ("x", "y", "c")


def _cparams(sem=None):
    return pltpu.CompilerParams(dimension_semantics=sem, vmem_limit_bytes=VMEM_LIMIT_MIB * 2 ** 20)


def _dot(a, b):
    return lax.dot_general(a, b, (((1,), (0,)), ((), ())), preferred_element_type=F32)


def _dot_nt(a, b):
    return lax.dot_general(a, b, (((1,), (1,)), ((), ())), preferred_element_type=F32)


def _dot_tn(a, b):
    return lax.dot_general(a, b, (((0,), (0,)), ((), ())), preferred_element_type=F32)


def _tile(n, cap):
    if n <= cap:
        return n
    best = None
    for t in range(128, cap + 1, 128):
        if n % t == 0:
            best = t
    assert best is not None, (n, cap)
    return best


def _cat(vals):
    return vals[0] if len(vals) == 1 else jnp.concatenate(vals, axis=1)


def _mm(pairs, *, nt, name, res=None, out_dtype=BF16):
    m = pairs[0][0].shape[0]
    n = pairs[0][1].shape[0] if nt else pairs[0][1].shape[1]
    tm = min(512, m)
    tn = n
    while sum(a.shape[1] for a, _ in pairs) * tn * 2 > MM_WEIGHT_TILE_BYTES and tn % 256 == 0:
        tn //= 2
    npairs = len(pairs)

    def body(*refs):
        acc = None
        for p in range(npairs):
            a = refs[2 * p][...].astype(BF16)
            b = refs[2 * p + 1][...]
            d = _dot_nt(a, b) if nt else _dot(a, b)
            acc = d if acc is None else acc + d
        if res is not None:
            acc = acc + refs[2 * npairs][...]
        refs[-1][...] = acc.astype(refs[-1].dtype)

    in_specs, args = [], []
    for a, b in pairs:
        k = a.shape[1]
        in_specs.append(pl.BlockSpec((tm, k), lambda j, i: (i, 0)))
        if nt:
            in_specs.append(pl.BlockSpec((tn, k), lambda j, i: (j, 0)))
        else:
            in_specs.append(pl.BlockSpec((k, tn), lambda j, i: (0, j)))
        args += [a, b]
    if res is not None:
        in_specs.append(pl.BlockSpec((tm, tn), lambda j, i: (i, j)))
        args.append(res)
    return pl.pallas_call(
        body, grid=(n // tn, m // tm), in_specs=in_specs,
        out_specs=pl.BlockSpec((tm, tn), lambda j, i: (i, j)),
        out_shape=jax.ShapeDtypeStruct((m, n), out_dtype),
        compiler_params=_cparams(("parallel", "parallel")), name=name)(*args)


def _mm_tn(a, g, *, name):
    s, k = a.shape
    n = g.shape[1]
    tk, tn, ts = _tile(k, 1408), _tile(n, 1536), min(512, s)

    def body(a_ref, g_ref, o_ref):
        @pl.when(pl.program_id(2) == 0)
        def _():
            o_ref[...] = jnp.zeros_like(o_ref)
        o_ref[...] += _dot_tn(a_ref[...].astype(BF16), g_ref[...].astype(BF16))

    return pl.pallas_call(
        body, grid=(k // tk, n // tn, s // ts),
        in_specs=[pl.BlockSpec((ts, tk), lambda i, j, t: (t, i)),
                  pl.BlockSpec((ts, tn), lambda i, j, t: (t, j))],
        out_specs=pl.BlockSpec((tk, tn), lambda i, j, t: (i, j)),
        out_shape=jax.ShapeDtypeStruct((k, n), F32),
        compiler_params=_cparams(("parallel", "parallel", "arbitrary")), name=name)(a, g)


def _block_diag(width, group):
    r = lax.broadcasted_iota(jnp.int32, (width, width), 0) // group
    c = lax.broadcasted_iota(jnp.int32, (width, width), 1) // group
    return (r == c).astype(BF16)


def _group_mean(v, bd, group):
    hi = v.astype(BF16)
    lo = (v - hi.astype(F32)).astype(BF16)
    return (_dot(hi, bd) + _dot(lo, bd)) * (1.0 / group)


def _norm_fwd(xs, w, *, group, name):
    s = xs[0].shape[0]
    width = sum(x.shape[1] for x in xs)
    tm = min(512, s)
    nx = len(xs)
    grouped = group != width

    def body(*refs):
        x = _cat([r[...].astype(F32) for r in refs[:nx]])
        if grouped:
            ms = _group_mean(x * x, refs[nx + 1][...], group)
        else:
            ms = jnp.mean(x * x, axis=1, keepdims=True)
        refs[-1][...] = (x * lax.rsqrt(ms + EPS) * refs[nx][...]).astype(BF16)

    in_specs = [pl.BlockSpec((tm, x.shape[1]), lambda i: (i, 0)) for x in xs]
    in_specs.append(pl.BlockSpec((1, width), lambda i: (0, 0)))
    args = list(xs) + [w]
    if grouped:
        in_specs.append(pl.BlockSpec((width, width), lambda i: (0, 0)))
        args.append(_block_diag(width, group))
    return pl.pallas_call(
        body, grid=(s // tm,), in_specs=in_specs,
        out_specs=pl.BlockSpec((tm, width), lambda i: (i, 0)),
        out_shape=jax.ShapeDtypeStruct((s, width), BF16),
        compiler_params=_cparams(("parallel",)), name=name)(*args)


def _norm_bwd(xs, w, dy, *, group, name, res=None, out_dtype=F32):
    s = xs[0].shape[0]
    width = sum(x.shape[1] for x in xs)
    tm = min(512, s)
    nx = len(xs)
    grouped = group != width
    has_res = res is not None

    def body(*refs):
        x = _cat([r[...].astype(F32) for r in refs[:nx]])
        w_ref, dy_ref = refs[nx], refs[nx + 1]
        pos = nx + 2
        if has_res:
            res_ref = refs[pos]
            pos += 1
        if grouped:
            bd = refs[pos][...]
            mean = lambda v: _group_mean(v, bd, group)
        else:
            mean = lambda v: jnp.mean(v, axis=1, keepdims=True)
        dx_ref, dw_ref = refs[-2], refs[-1]
        inv = lax.rsqrt(mean(x * x) + EPS)
        xh = x * inv
        d = dy_ref[...].astype(F32)
        g = d * w_ref[...]
        dx = inv * (g - xh * mean(g * xh))
        if has_res:
            dx = dx + res_ref[...]
        dx_ref[...] = dx.astype(dx_ref.dtype)

        @pl.when(pl.program_id(0) == 0)
        def _():
            dw_ref[...] = jnp.zeros_like(dw_ref)
        dw_ref[...] += (d * xh).reshape(tm // 8, 8, width).sum(axis=0)

    in_specs = [pl.BlockSpec((tm, x.shape[1]), lambda i: (i, 0)) for x in xs]
    in_specs += [pl.BlockSpec((1, width), lambda i: (0, 0)), pl.BlockSpec((tm, width), lambda i: (i, 0))]
    args = list(xs) + [w, dy]
    if has_res:
        in_specs.append(pl.BlockSpec((tm, width), lambda i: (i, 0)))
        args.append(res)
    if grouped:
        in_specs.append(pl.BlockSpec((width, width), lambda i: (0, 0)))
        args.append(_block_diag(width, group))
    return pl.pallas_call(
        body, grid=(s // tm,), in_specs=in_specs,
        out_specs=(pl.BlockSpec((tm, width), lambda i: (i, 0)), pl.BlockSpec((8, width), lambda i: (0, 0))),
        out_shape=(jax.ShapeDtypeStruct((s, width), out_dtype), jax.ShapeDtypeStruct((8, width), F32)),
        compiler_params=_cparams(("arbitrary",)), name=name)(*args)


def _head0_lanes(shape):
    return lax.broadcasted_iota(jnp.int32, shape, len(shape) - 1) < HEAD_DIM


def _split_heads(v):
    m0 = _head0_lanes(v.shape)
    zero = jnp.zeros_like(v)
    return jnp.where(m0, v, zero), jnp.where(m0, zero, v)


def _pair_sum(v):
    m0 = _head0_lanes(v.shape)
    s0 = jnp.sum(jnp.where(m0, v, 0.0), axis=1, keepdims=True)
    s1 = jnp.sum(jnp.where(m0, 0.0, v), axis=1, keepdims=True)
    return jnp.where(m0, s0, s1)


def _pair_norm(x, w2):
    inv = lax.rsqrt(_pair_sum(x * x) * (1.0 / HEAD_DIM) + EPS)
    xh = x * inv
    return xh * w2, xh, inv


def _pair_norm_bwd(dy, xh, inv, w2):
    g = dy * w2
    return inv * (g - xh * (_pair_sum(g * xh) * (1.0 / HEAD_DIM)))


def _copy_in(copies):
    for c in copies:
        c.start()
    for c in copies:
        c.wait()


def _neg_softplus(z):
    return -(jnp.maximum(z, 0.0) + jnp.log(1.0 + jnp.exp(-jnp.abs(z))))


def _sb_fwd(proj, hp, *, name):
    s = proj.shape[0]
    nq = s // SB_BLOCK
    assert nq <= LANES and nq % SB_GROUP == 0
    shape = (SB_BLOCK, SB_BLOCK)

    def body(q_ref, proj_hbm, y_ref, c0_ref, c1_ref, k_scr, v_scr, sems):
        i = pl.program_id(0)

        @pl.when(i == 0)
        def _():
            _copy_in([pltpu.make_async_copy(proj_hbm.at[:, pl.ds(COL_KA + hp * LANES, LANES)], k_scr, sems.at[0]),
                      pltpu.make_async_copy(proj_hbm.at[:, pl.ds(COL_VA + hp * LANES, LANES)], v_scr, sems.at[1])])

        row = lax.broadcasted_iota(jnp.int32, shape, 0)
        col = lax.broadcasted_iota(jnp.int32, shape, 1)
        m0 = col < HEAD_DIM
        later_keys = (row > col).astype(BF16)
        qh = _split_heads(q_ref[...])

        def group(g, carry, masked):
            c, acc, seen = list(carry[:2]), carry[2], list(carry[3:5])
            js = [g * SB_GROUP + b for b in reversed(range(SB_GROUP))]
            rows = [pl.ds(pl.multiple_of(j * SB_BLOCK, SB_BLOCK), SB_BLOCK) for j in js]
            chains = [(b, h) for b in range(SB_GROUP) for h in range(2)]
            earlier = [(col + j * SB_BLOCK < row + i * SB_BLOCK) if masked else None for j in js]
            z = {(b, h): _dot_nt(qh[h], k_scr[rows[b], :]) * SCALE for b, h in chains}
            lr = {ch: _neg_softplus(z[ch]) for ch in chains}
            lrm = {(b, h): jnp.where(earlier[b], lr[b, h], 0.0) if masked else lr[b, h] for b, h in chains}
            after = {ch: _dot(lrm[ch].astype(BF16), later_keys) for ch in chains}
            w = {}
            for b, h in chains:
                seen[h] = jnp.where(col == js[b], c[h], seen[h])
                wv = jnp.exp(z[b, h] + lr[b, h] + (after[b, h] + c[h]))
                w[b, h] = jnp.where(earlier[b], wv, 0.0) if masked else wv
                c[h] = c[h] + jnp.sum(lrm[b, h], axis=1, keepdims=True)
            pv = {(b, h): _dot(w[b, h].astype(BF16), v_scr[rows[b], :]) for b, h in chains}
            for b in range(SB_GROUP):
                acc = acc + jnp.where(m0, pv[b, 0], pv[b, 1])
            return c[0], c[1], acc, seen[0], seen[1]

        zc = jnp.zeros((SB_BLOCK, 1), F32)
        zt = jnp.zeros(shape, F32)
        gi = i // SB_GROUP
        carry = group(gi, (zc, zc, zt, zt, zt), True)
        carry = lax.fori_loop(0, gi, lambda t, cr: group(gi - 1 - t, cr, False), carry)
        y_ref[...] = carry[2].astype(y_ref.dtype)
        c0_ref[...] = carry[3]
        c1_ref[...] = carry[4]

    return pl.pallas_call(
        body, grid=(nq,),
        in_specs=[pl.BlockSpec(shape, lambda i: (i, COL_QA // LANES + hp)), pl.BlockSpec(memory_space=pl.ANY)],
        out_specs=(pl.BlockSpec(shape, lambda i: (i, 0)),) * 3,
        out_shape=(jax.ShapeDtypeStruct((s, LANES), BF16), jax.ShapeDtypeStruct((s, LANES), F32),
                   jax.ShapeDtypeStruct((s, LANES), F32)),
        scratch_shapes=[pltpu.VMEM((s, LANES), BF16), pltpu.VMEM((s, LANES), BF16), pltpu.SemaphoreType.DMA((2,))],
        compiler_params=_cparams(("arbitrary",)), name=name)(proj, proj)


def _sb_bwd(proj, carries, dy_all, hp, *, name):
    s = proj.shape[0]
    nq = s // SB_BLOCK
    shape = (SB_BLOCK, SB_BLOCK)

    def body(q_ref, c0_ref, c1_ref, dy_ref, proj_hbm, dq_ref, dk_hbm, dv_hbm, k_scr, v_scr, dk_scr, dv_scr, sems):
        i = pl.program_id(0)

        @pl.when(i == 0)
        def _():
            _copy_in([pltpu.make_async_copy(proj_hbm.at[:, pl.ds(COL_KA + hp * LANES, LANES)], k_scr, sems.at[0]),
                      pltpu.make_async_copy(proj_hbm.at[:, pl.ds(COL_VA + hp * LANES, LANES)], v_scr, sems.at[1])])
            dk_scr[...] = jnp.zeros_like(dk_scr)
            dv_scr[...] = jnp.zeros_like(dv_scr)

        row = lax.broadcasted_iota(jnp.int32, shape, 0)
        col = lax.broadcasted_iota(jnp.int32, shape, 1)
        m0 = col < HEAD_DIM
        later_keys = (row > col).astype(BF16)
        earlier_keys = (row < col).astype(BF16)
        qh = _split_heads(q_ref[...])
        dyh = _split_heads(dy_ref[...])
        seen = (c0_ref[...], c1_ref[...])

        def group(g, carry, masked):
            r, dq = list(carry[0:2]), list(carry[2:4])
            js = [g * SB_GROUP + b for b in range(SB_GROUP)]
            rows = [pl.ds(pl.multiple_of(j * SB_BLOCK, SB_BLOCK), SB_BLOCK) for j in js]
            chains = [(b, h) for b in range(SB_GROUP) for h in range(2)]
            earlier = [(col + j * SB_BLOCK < row + i * SB_BLOCK) if masked else None for j in js]
            z = {(b, h): _dot_nt(qh[h], k_scr[rows[b], :]) * SCALE for b, h in chains}
            dw = {(b, h): _dot_nt(dyh[h], v_scr[rows[b], :]) for b, h in chains}
            lr = {ch: _neg_softplus(z[ch]) for ch in chains}
            lrm = {(b, h): jnp.where(earlier[b], lr[b, h], 0.0) if masked else lr[b, h] for b, h in chains}
            after = {ch: _dot(lrm[ch].astype(BF16), later_keys) for ch in chains}
            w, de = {}, {}
            for b, h in chains:
                c = jnp.sum(jnp.where(col == js[b], seen[h], 0.0), axis=1, keepdims=True)
                wv = jnp.exp(z[b, h] + lr[b, h] + (after[b, h] + c))
                w[b, h] = jnp.where(earlier[b], wv, 0.0) if masked else wv
                de[b, h] = dw[b, h] * w[b, h]
            before = {ch: _dot(de[ch].astype(BF16), earlier_keys) for ch in chains}
            dzb = {}
            for b, h in chains:
                dz = de[b, h] - jnp.exp(z[b, h] + lr[b, h]) * (de[b, h] + (r[h] + before[b, h]))
                if masked:
                    dz = jnp.where(earlier[b], dz, 0.0)
                dzb[b, h] = (dz * SCALE).astype(BF16)
                r[h] = r[h] + jnp.sum(de[b, h], axis=1, keepdims=True)
            for b, h in chains:
                dq[h] = dq[h] + _dot(dzb[b, h], k_scr[rows[b], :])
            dkc = {ch: _dot_tn(dzb[ch], qh[ch[1]]) for ch in chains}
            dvc = {ch: _dot_tn(w[ch].astype(BF16), dyh[ch[1]]) for ch in chains}
            for b in range(SB_GROUP):
                dk_scr[rows[b], :] += dkc[b, 0] + dkc[b, 1]
                dv_scr[rows[b], :] += dvc[b, 0] + dvc[b, 1]
            return r[0], r[1], dq[0], dq[1]

        zc = jnp.zeros((SB_BLOCK, 1), F32)
        zq = jnp.zeros(shape, F32)
        gi = i // SB_GROUP
        carry = lax.fori_loop(0, gi, lambda t, cr: group(t, cr, False), (zc, zc, zq, zq))
        carry = group(gi, carry, True)
        dq_ref[...] = jnp.where(m0, carry[2], carry[3]).astype(dq_ref.dtype)

        @pl.when(i == nq - 1)
        def _():
            _copy_in([pltpu.make_async_copy(dk_scr, dk_hbm, sems.at[0]),
                      pltpu.make_async_copy(dv_scr, dv_hbm, sems.at[1])])

    blk = lambda c0: pl.BlockSpec(shape, lambda i: (i, c0))
    return pl.pallas_call(
        body, grid=(nq,),
        in_specs=[blk(COL_QA // LANES + hp), blk(0), blk(0), blk(hp), pl.BlockSpec(memory_space=pl.ANY)],
        out_specs=(blk(0), pl.BlockSpec(memory_space=pl.ANY), pl.BlockSpec(memory_space=pl.ANY)),
        out_shape=(jax.ShapeDtypeStruct((s, LANES), BF16), jax.ShapeDtypeStruct((s, LANES), F32),
                   jax.ShapeDtypeStruct((s, LANES), F32)),
        scratch_shapes=[pltpu.VMEM((s, LANES), BF16), pltpu.VMEM((s, LANES), BF16),
                        pltpu.VMEM((s, LANES), F32), pltpu.VMEM((s, LANES), F32), pltpu.SemaphoreType.DMA((2,))],
        compiler_params=_cparams(("arbitrary",)), name=name)(proj, carries[0], carries[1], dy_all, proj)


def _rel_index():
    p = np.arange(CH_BLOCK)[:, None]
    m = np.arange(CH_BAND)[None, :]
    return (np.clip(CH_PAD + p - m, -REL_CLIP, REL_CLIP) + REL_CLIP).astype(np.int32)


def _bias_expand(rel_bias, *, name):
    nh = rel_bias.shape[0]
    table = jnp.pad(rel_bias, ((0, 0), (0, REL_PAD - N_REL)))

    def body(tab_ref, rel_ref, out_ref):
        rid = lax.broadcasted_iota(jnp.int32, (REL_PAD, CH_BAND), 0)

        def step(p, _):
            onehot = (rid == rel_ref[pl.ds(p, 1), :]).astype(F32)
            out_ref[p] = lax.dot_general(tab_ref[...], onehot, (((1,), (0,)), ((), ())),
                                         precision=lax.Precision.HIGHEST, preferred_element_type=F32)
            return 0

        lax.fori_loop(0, CH_BLOCK, step, 0)

    out = pl.pallas_call(
        body, out_shape=jax.ShapeDtypeStruct((CH_BLOCK, nh, CH_BAND), F32),
        compiler_params=_cparams(), name=name)(table, jnp.asarray(_rel_index()))
    return out.transpose(1, 0, 2)


def _bias_fold(dbias, *, name):
    nh = dbias.shape[0]

    def body(db_ref, rel_ref, out_ref):
        rid = lax.broadcasted_iota(jnp.int32, (REL_PAD, CH_BAND), 0)

        def step(p, acc):
            onehot = (rid == rel_ref[pl.ds(p, 1), :]).astype(F32)
            return acc + lax.dot_general(db_ref[p], onehot, (((1,), (1,)), ((), ())),
                                         precision=lax.Precision.HIGHEST, preferred_element_type=F32)

        out_ref[...] = lax.fori_loop(0, CH_BLOCK, step, jnp.zeros((nh, REL_PAD), F32))

    out = pl.pallas_call(
        body, out_shape=jax.ShapeDtypeStruct((nh, REL_PAD), F32),
        compiler_params=_cparams(), name=name)(dbias.transpose(1, 0, 2), jnp.asarray(_rel_index()))
    return out[:, :N_REL]


def _band_valid(i):
    row = lax.broadcasted_iota(jnp.int32, (CH_BLOCK, CH_BAND), 0)
    col = lax.broadcasted_iota(jnp.int32, (CH_BLOCK, CH_BAND), 1)
    first = row < CH_BLOCK // 2
    window = (first & (col < CH_BAND - 64)) | (jnp.logical_not(first) & (col >= 64))
    return window & (col + i * CH_BLOCK >= CH_PAD)


def _ch_load(proj_hbm, hp, s, kw_ref, kraw_scr, kn_scr, vp_scr, sems):
    _copy_in([pltpu.make_async_copy(proj_hbm.at[:, pl.ds(COL_KB + hp * LANES, LANES)], kraw_scr, sems.at[0]),
              pltpu.make_async_copy(proj_hbm.at[:, pl.ds(COL_VB + hp * LANES, LANES)],
                                    vp_scr.at[pl.ds(CH_PAD, s), :], sems.at[1])])
    kn_scr[0:CH_PAD, :] = jnp.zeros((CH_PAD, LANES), BF16)
    vp_scr[0:CH_PAD, :] = jnp.zeros((CH_PAD, LANES), BF16)
    rb = min(512, s)

    def step(t, _):
        r = pl.multiple_of(t * rb, rb)
        kn = _pair_norm(kraw_scr[pl.ds(r, rb), :].astype(F32), kw_ref[...])[0]
        kn_scr[pl.ds(CH_PAD + r, rb), :] = kn.astype(BF16)
        return 0

    lax.fori_loop(0, s // rb, step, 0)


def _ch_probs(qh, kb, bias, valid):
    sc = jnp.where(valid, _dot_nt(qh, kb) * SCALE + bias, NEG)
    p = jnp.exp(sc - jnp.max(sc, axis=1, keepdims=True))
    return p, 1.0 / jnp.sum(p, axis=1, keepdims=True)


def _ch_fwd(proj, qw2, kw2, bias, hp, *, name):
    s = proj.shape[0]
    nb = s // CH_BLOCK
    shape = (CH_BLOCK, LANES)

    def body(q_ref, proj_hbm, qw_ref, kw_ref, bias_ref, y_ref, kraw_scr, kn_scr, vp_scr, sems):
        i = pl.program_id(0)

        @pl.when(i == 0)
        def _():
            _ch_load(proj_hbm, hp, s, kw_ref, kraw_scr, kn_scr, vp_scr, sems)

        band = pl.ds(pl.multiple_of(i * CH_BLOCK, CH_BLOCK), CH_BAND)
        kb, vb = kn_scr[band, :], vp_scr[band, :]
        valid = _band_valid(i)
        qn = _pair_norm(q_ref[...].astype(F32), qw_ref[...])[0]
        qh = _split_heads(qn.astype(BF16))
        outs = []
        for h in range(2):
            p, rl = _ch_probs(qh[h], kb, bias_ref[h], valid)
            outs.append(_dot(p.astype(BF16), vb) * rl)
        y_ref[...] = jnp.where(_head0_lanes(shape), outs[0], outs[1]).astype(y_ref.dtype)

    return pl.pallas_call(
        body, grid=(nb,),
        in_specs=[pl.BlockSpec(shape, lambda i: (i, COL_QB // LANES + hp)), pl.BlockSpec(memory_space=pl.ANY),
                  pl.BlockSpec((1, LANES), lambda i: (0, 0)), pl.BlockSpec((1, LANES), lambda i: (0, 0)),
                  pl.BlockSpec((2, CH_BLOCK, CH_BAND), lambda i: (hp, 0, 0))],
        out_specs=pl.BlockSpec(shape, lambda i: (i, 0)),
        out_shape=jax.ShapeDtypeStruct((s, LANES), BF16),
        scratch_shapes=[pltpu.VMEM((s, LANES), BF16), pltpu.VMEM((s + CH_PAD, LANES), BF16),
                        pltpu.VMEM((s + CH_PAD, LANES), BF16), pltpu.SemaphoreType.DMA((2,))],
        compiler_params=_cparams(("arbitrary",)), name=name)(proj, proj, qw2, kw2, bias)


def _ch_bwd(proj, dy_all, qw2, kw2, bias, hp, *, name):
    s = proj.shape[0]
    nb = s // CH_BLOCK
    shape = (CH_BLOCK, LANES)
    rb = min(512, s)

    def body(q_ref, dy_ref, proj_hbm, qw_ref, kw_ref, bias_ref, dq_ref, dk_hbm, dv_hbm, dbias_ref, dqw_ref, dkw_ref,
             kraw_scr, kn_scr, vp_scr, dkn_scr, dvp_scr, sems):
        i = pl.program_id(0)

        @pl.when(i == 0)
        def _():
            _ch_load(proj_hbm, hp, s, kw_ref, kraw_scr, kn_scr, vp_scr, sems)
            dkn_scr[...] = jnp.zeros_like(dkn_scr)
            dvp_scr[...] = jnp.zeros_like(dvp_scr)
            dbias_ref[...] = jnp.zeros_like(dbias_ref)
            dqw_ref[...] = jnp.zeros_like(dqw_ref)

        band = pl.ds(pl.multiple_of(i * CH_BLOCK, CH_BLOCK), CH_BAND)
        kb, vb = kn_scr[band, :], vp_scr[band, :]
        valid = _band_valid(i)
        qn, qhat, qinv = _pair_norm(q_ref[...].astype(F32), qw_ref[...])
        qh = _split_heads(qn.astype(BF16))
        dyh = _split_heads(dy_ref[...])
        dqn = []
        dkc = jnp.zeros((CH_BAND, LANES), F32)
        dvc = jnp.zeros((CH_BAND, LANES), F32)
        for h in range(2):
            p, rl = _ch_probs(qh[h], kb, bias_ref[h], valid)
            p = p * rl
            dp = _dot_nt(dyh[h], vb)
            ds = p * (dp - jnp.sum(dp * p, axis=1, keepdims=True))
            dbias_ref[h] += ds
            dsb = (ds * SCALE).astype(BF16)
            dqn.append(_dot(dsb, kb))
            dkc = dkc + _dot_tn(dsb, qh[h])
            dvc = dvc + _dot_tn(p.astype(BF16), dyh[h])
        dkn_scr[band, :] += dkc
        dvp_scr[band, :] += dvc
        dqn = jnp.where(_head0_lanes(shape), dqn[0], dqn[1])
        dq_ref[...] = _pair_norm_bwd(dqn, qhat, qinv, qw_ref[...]).astype(dq_ref.dtype)
        dqw_ref[...] += (dqn * qhat).reshape(CH_BLOCK // 8, 8, LANES).sum(axis=0)

        @pl.when(i == nb - 1)
        def _():
            def step(t, acc):
                r = pl.multiple_of(t * rb, rb)
                rows = pl.ds(CH_PAD + r, rb)
                _, khat, kinv = _pair_norm(kraw_scr[pl.ds(r, rb), :].astype(F32), kw_ref[...])
                dkn = dkn_scr[rows, :]
                dkn_scr[rows, :] = _pair_norm_bwd(dkn, khat, kinv, kw_ref[...])
                return acc + (dkn * khat).reshape(rb // 8, 8, LANES).sum(axis=0)

            dkw_ref[...] = lax.fori_loop(0, s // rb, step, jnp.zeros((8, LANES), F32))
            _copy_in([pltpu.make_async_copy(dkn_scr.at[pl.ds(CH_PAD, s), :], dk_hbm, sems.at[0]),
                      pltpu.make_async_copy(dvp_scr.at[pl.ds(CH_PAD, s), :], dv_hbm, sems.at[1])])

    blk = lambda c0: pl.BlockSpec(shape, lambda i: (i, c0))
    vec = pl.BlockSpec((1, LANES), lambda i: (0, 0))
    part = pl.BlockSpec((8, LANES), lambda i: (0, 0))
    hbm = pl.BlockSpec(memory_space=pl.ANY)
    return pl.pallas_call(
        body, grid=(nb,),
        in_specs=[blk(COL_QB // LANES + hp), blk(D_SB // LANES + hp), hbm, vec, vec,
                  pl.BlockSpec((2, CH_BLOCK, CH_BAND), lambda i: (hp, 0, 0))],
        out_specs=(blk(0), hbm, hbm, pl.BlockSpec((2, CH_BLOCK, CH_BAND), lambda i: (0, 0, 0)), part, part),
        out_shape=(jax.ShapeDtypeStruct((s, LANES), BF16), jax.ShapeDtypeStruct((s, LANES), F32),
                   jax.ShapeDtypeStruct((s, LANES), F32), jax.ShapeDtypeStruct((2, CH_BLOCK, CH_BAND), F32),
                   jax.ShapeDtypeStruct((8, LANES), F32), jax.ShapeDtypeStruct((8, LANES), F32)),
        scratch_shapes=[pltpu.VMEM((s, LANES), BF16), pltpu.VMEM((s + CH_PAD, LANES), BF16),
                        pltpu.VMEM((s + CH_PAD, LANES), BF16), pltpu.VMEM((s + CH_PAD, LANES), F32),
                        pltpu.VMEM((s + CH_PAD, LANES), F32), pltpu.SemaphoreType.DMA((2,))],
        compiler_params=_cparams(("arbitrary",)), name=name)(proj, dy_all, proj, qw2, kw2, bias)


def _shift_down(x, halo, k):
    out = pltpu.roll(x, k, 0)
    row = lax.broadcasted_iota(jnp.int32, x.shape, 0)
    for t in range(k):
        out = jnp.where(row == t, halo[8 - k + t:8 - k + t + 1, :], out)
    return out


def _shift_up(x, halo, k):
    n = x.shape[0]
    out = pltpu.roll(x, n - k, 0)
    row = lax.broadcasted_iota(jnp.int32, x.shape, 0)
    for t in range(k):
        out = jnp.where(row == n - k + t, halo[t:t + 1, :], out)
    return out


def _conv_specs(s, tm):
    cw = D_CV
    tile = lambda c0: pl.BlockSpec((tm, cw), lambda i: (i, c0))
    above = lambda c0: pl.BlockSpec((8, cw), lambda i: (jnp.maximum(i * (tm // 8) - 1, 0), c0))
    below = lambda c0: pl.BlockSpec((8, cw), lambda i: (jnp.minimum((i + 1) * (tm // 8), s // 8 - 1), c0))
    return tile, above, below


def _conv_fwd(proj, w8, *, name):
    s = proj.shape[0]
    tm = min(512, s)
    tile, above, _ = _conv_specs(s, tm)

    def body(gb_ref, gc_ref, xc_ref, gca_ref, xca_ref, w_ref, y_ref):
        hc = gc_ref[...].astype(F32) * xc_ref[...].astype(F32)
        top = jnp.where(pl.program_id(0) > 0, gca_ref[...].astype(F32) * xca_ref[...].astype(F32), 0.0)
        u = w_ref[0:1, :] * _shift_down(hc, top, 2) + w_ref[1:2, :] * _shift_down(hc, top, 1) + w_ref[2:3, :] * hc
        y_ref[...] = (gb_ref[...].astype(F32) * u).astype(y_ref.dtype)

    cb, cc, cx = COL_GB // D_CV, COL_GC // D_CV, COL_XC // D_CV
    return pl.pallas_call(
        body, grid=(s // tm,),
        in_specs=[tile(cb), tile(cc), tile(cx), above(cc), above(cx), pl.BlockSpec((8, D_CV), lambda i: (0, 0))],
        out_specs=pl.BlockSpec((tm, D_CV), lambda i: (i, 0)),
        out_shape=jax.ShapeDtypeStruct((s, D_CV), BF16),
        compiler_params=_cparams(("parallel",)), name=name)(proj, proj, proj, proj, proj, w8)


def _conv_bwd(proj, dy_all, w8, *, name):
    s = proj.shape[0]
    tm = min(512, s)
    nt = s // tm
    tile, above, below = _conv_specs(s, tm)

    def body(gb_ref, gc_ref, xc_ref, dy_ref, gca_ref, xca_ref, gbb_ref, dyb_ref, w_ref,
             dgb_ref, dgc_ref, dxc_ref, dw_ref):
        i = pl.program_id(0)
        gb, gc, xc = gb_ref[...].astype(F32), gc_ref[...].astype(F32), xc_ref[...].astype(F32)
        dy = dy_ref[...].astype(F32)
        hc = gc * xc
        top = jnp.where(i > 0, gca_ref[...].astype(F32) * xca_ref[...].astype(F32), 0.0)
        hc1, hc2 = _shift_down(hc, top, 1), _shift_down(hc, top, 2)
        u = w_ref[0:1, :] * hc2 + w_ref[1:2, :] * hc1 + w_ref[2:3, :] * hc
        du = dy * gb
        bottom = jnp.where(i < nt - 1, dyb_ref[...].astype(F32) * gbb_ref[...].astype(F32), 0.0)
        dhc = w_ref[2:3, :] * du + w_ref[1:2, :] * _shift_up(du, bottom, 1) + w_ref[0:1, :] * _shift_up(du, bottom, 2)
        dgb_ref[...] = (dy * u).astype(dgb_ref.dtype)
        dgc_ref[...] = (dhc * xc).astype(dgc_ref.dtype)
        dxc_ref[...] = (dhc * gc).astype(dxc_ref.dtype)

        @pl.when(i == 0)
        def _():
            dw_ref[...] = jnp.zeros_like(dw_ref)
        dw_ref[0:1, :] += jnp.sum(du * hc2, axis=0, keepdims=True)
        dw_ref[1:2, :] += jnp.sum(du * hc1, axis=0, keepdims=True)
        dw_ref[2:3, :] += jnp.sum(du * hc, axis=0, keepdims=True)

    cb, cc, cx = COL_GB // D_CV, COL_GC // D_CV, COL_XC // D_CV
    cy = (D_SB + D_CH) // D_CV
    out_tile = pl.BlockSpec((tm, D_CV), lambda i: (i, 0))
    act = jax.ShapeDtypeStruct((s, D_CV), BF16)
    return pl.pallas_call(
        body, grid=(nt,),
        in_specs=[tile(cb), tile(cc), tile(cx), tile(cy), above(cc), above(cx), below(cb), below(cy),
                  pl.BlockSpec((8, D_CV), lambda i: (0, 0))],
        out_specs=(out_tile, out_tile, out_tile, pl.BlockSpec((8, D_CV), lambda i: (0, 0))),
        out_shape=(act, act, act, jax.ShapeDtypeStruct((8, D_CV), F32)),
        compiler_params=_cparams(("arbitrary",)), name=name)(proj, proj, proj, dy_all, proj, proj, proj, dy_all, w8)


def _ffn_up(h, wg, wu, *, name):
    s, d = h.shape
    f = wg.shape[1]
    tm, tn = min(512, s), _tile(f, 1408)

    def body(h_ref, wg_ref, wu_ref, g_ref, u_ref, a_ref):
        hv = h_ref[...]
        g = _dot(hv, wg_ref[...])
        u = _dot(hv, wu_ref[...])
        g_ref[...] = g.astype(BF16)
        u_ref[...] = u.astype(BF16)
        a_ref[...] = (g * jax.nn.sigmoid(g) * u).astype(BF16)

    wspec = pl.BlockSpec((d, tn), lambda j, i: (0, j))
    ospec = pl.BlockSpec((tm, tn), lambda j, i: (i, j))
    act = jax.ShapeDtypeStruct((s, f), BF16)
    return pl.pallas_call(
        body, grid=(f // tn, s // tm),
        in_specs=[pl.BlockSpec((tm, d), lambda j, i: (i, 0)), wspec, wspec],
        out_specs=(ospec, ospec, ospec), out_shape=(act, act, act),
        compiler_params=_cparams(("parallel", "parallel")), name=name)(h, wg, wu)


def _ffn_dact(dx, wd, g, u, *, name):
    s, d = dx.shape
    f = wd.shape[0]
    tm, tf = min(512, s), _tile(f, 1408)

    def body(dx_ref, wd_ref, g_ref, u_ref, dg_ref, du_ref):
        da = _dot_nt(dx_ref[...].astype(BF16), wd_ref[...])
        gv, uv = g_ref[...].astype(F32), u_ref[...].astype(F32)
        sg = jax.nn.sigmoid(gv)
        dg_ref[...] = (da * uv * sg * (1.0 + gv * (1.0 - sg))).astype(BF16)
        du_ref[...] = (da * gv * sg).astype(BF16)

    tspec = pl.BlockSpec((tm, tf), lambda j, i: (i, j))
    act = jax.ShapeDtypeStruct((s, f), BF16)
    return pl.pallas_call(
        body, grid=(f // tf, s // tm),
        in_specs=[pl.BlockSpec((tm, d), lambda j, i: (i, 0)), pl.BlockSpec((tf, d), lambda j, i: (j, 0)), tspec, tspec],
        out_specs=(tspec, tspec), out_shape=(act, act),
        compiler_params=_cparams(("parallel", "parallel")), name=name)(dx, wd, g, u)


def _loss_head(y, target, *, name):
    s, d = y.shape
    tm = min(512, s)

    def body(y_ref, t_ref, dy_ref, sq_ref):
        err = y_ref[...] - t_ref[...]
        dy_ref[...] = err * (1.0 / d)

        @pl.when(pl.program_id(0) == 0)
        def _():
            sq_ref[...] = jnp.zeros_like(sq_ref)
        sq_ref[...] += (err * err).reshape(tm // 8, 8, d).sum(axis=0)

    tile = pl.BlockSpec((tm, d), lambda i: (i, 0))
    return pl.pallas_call(
        body, grid=(s // tm,), in_specs=[tile, tile],
        out_specs=(tile, pl.BlockSpec((8, d), lambda i: (0, 0))),
        out_shape=(jax.ShapeDtypeStruct((s, d), F32), jax.ShapeDtypeStruct((8, d), F32)),
        compiler_params=_cparams(("arbitrary",)), name=name)(y, target)


def _adamw(parts, w, m, v, *, name):
    r, c = w.shape
    tr = r if r <= 512 else _tile_rows(r)

    def body(p_ref, w_ref, m_ref, v_ref, g_ref, d_ref, nm_ref, nv_ref):
        g = p_ref[0]
        for dev in range(1, N_DEV):
            g = g + p_ref[dev]
        nm = ADAM_B1 * m_ref[...] + (1.0 - ADAM_B1) * g
        nv = ADAM_B2 * v_ref[...] + (1.0 - ADAM_B2) * (g * g)
        m_hat = nm / (1.0 - ADAM_B1 ** ADAM_STEP)
        v_hat = nv / (1.0 - ADAM_B2 ** ADAM_STEP)
        g_ref[...] = g
        d_ref[...] = -ADAM_LR * (m_hat / (jnp.sqrt(v_hat) + ADAM_EPS) + ADAM_WD * w_ref[...])
        nm_ref[...] = nm
        nv_ref[...] = nv

    tile = pl.BlockSpec((tr, c), lambda i: (i, 0))
    out = jax.ShapeDtypeStruct((r, c), F32)
    return pl.pallas_call(
        body, grid=(r // tr,),
        in_specs=[pl.BlockSpec((N_DEV, tr, c), lambda i: (0, i, 0)), tile, tile, tile],
        out_specs=(tile, tile, tile, tile), out_shape=(out, out, out, out),
        compiler_params=_cparams(("parallel",)), name=name)(parts, w, m, v)


def _tile_rows(r):
    for t in (512, 256, 128, 64, 32, 16, 8):
        if r % t == 0:
            return t
    return r


def _peer(k):
    x, y, c = (lax.axis_index(a) for a in MESH_AXES)
    px = 1 - x if k & 4 else x
    py = 1 - y if k & 2 else y
    pc = 1 - c if k & 1 else c
    return (px, py, pc), 4 * px + 2 * py + pc


def _exchange(src_of, dst_of, local_sem, send_sems, recv_sems):
    x, y, c = (lax.axis_index(a) for a in MESH_AXES)
    me = 4 * x + 2 * y + c
    mine = pltpu.make_async_copy(src_of(me), dst_of(me), local_sem)
    mine.start()
    sends = []
    for k in range(1, N_DEV):
        dev, idx = _peer(k)
        cp = pltpu.make_async_remote_copy(src_ref=src_of(idx), dst_ref=dst_of(me), send_sem=send_sems.at[k - 1],
                                          recv_sem=recv_sems.at[k - 1], device_id=dev,
                                          device_id_type=pl.DeviceIdType.MESH)
        cp.start()
        sends.append(cp)
    for k in range(1, N_DEV):
        dev, idx = _peer(k)
        pltpu.make_async_remote_copy(src_ref=src_of(idx), dst_ref=dst_of(idx), send_sem=send_sems.at[k - 1],
                                     recv_sem=recv_sems.at[k - 1], device_id=dev,
                                     device_id_type=pl.DeviceIdType.MESH).wait_recv()
    for cp in sends:
        cp.wait_send()
    mine.wait()


def _comm_call(body, out_shape, space, name, *args):
    spec = pl.BlockSpec(memory_space=space)
    return pl.pallas_call(
        body, in_specs=[spec] * len(args), out_specs=spec, out_shape=out_shape,
        scratch_shapes=[pltpu.SemaphoreType.DMA, pltpu.SemaphoreType.DMA((N_DEV - 1,)),
                        pltpu.SemaphoreType.DMA((N_DEV - 1,))],
        compiler_params=pltpu.CompilerParams(vmem_limit_bytes=VMEM_LIMIT_MIB * 2 ** 20), name=name)(*args)


def _all_gather(shard, *, name, space=pl.ANY):
    def body(x_ref, out_ref, local_sem, send_sems, recv_sems):
        _exchange(lambda d: x_ref, lambda d: out_ref.at[d], local_sem, send_sems, recv_sems)

    return _comm_call(body, jax.ShapeDtypeStruct((N_DEV,) + shard.shape, shard.dtype), space, name, shard)


def _all_to_all(slabs, *, name):
    def body(x_ref, out_ref, local_sem, send_sems, recv_sems):
        _exchange(lambda d: x_ref.at[d], lambda d: out_ref.at[d], local_sem, send_sems, recv_sems)

    return _comm_call(body, jax.ShapeDtypeStruct(slabs.shape, slabs.dtype), pl.ANY, name, slabs)


def _cols_from_dev(g, depth):
    rows = g.shape[1] // depth
    return g.reshape(N_DEV, depth, rows, g.shape[2]).transpose(1, 2, 0, 3).reshape(depth, rows, -1)


def _rows_from_dev(g, depth):
    r = g.shape[1] // depth
    return g.reshape(N_DEV, depth, r, g.shape[2]).transpose(1, 0, 2, 3).reshape(depth, N_DEV * r, g.shape[2])


def _cols_to_dev(g):
    depth, rows, n = g.shape
    return g.reshape(depth, rows, N_DEV, n // N_DEV).transpose(2, 0, 1, 3).reshape(N_DEV, depth * rows, n // N_DEV)


def _rows_to_dev(g):
    depth, rows, n = g.shape
    return g.reshape(depth, N_DEV, rows // N_DEV, n).transpose(1, 0, 2, 3).reshape(N_DEV, -1, n)


def _pack(arrays):
    flat = jnp.concatenate([a.reshape(-1) for a in arrays])
    pad = (-flat.shape[0]) % (8 * LANES)
    return jnp.pad(flat, (0, pad)).reshape(-1, LANES)


def _unpack(buf, shapes):
    flat = buf.reshape(-1)
    out, pos = [], 0
    for shp in shapes:
        n = int(np.prod(shp))
        out.append(flat[pos:pos + n].reshape(shp))
        pos += n
    return out


def _fold_heads(part8):
    t = part8.sum(axis=(0, 1))
    return t[:HEAD_DIM] + t[HEAD_DIM:]


def _forward_backward(x, target, wts):
    depth = wts["w_in"].shape[0]
    row = lambda a: a.reshape(1, -1)
    saved = []
    for l in range(depth):
        t = f"l{l}_"
        h = _norm_fwd([x], row(wts["attn_norm_w"][l]), group=D_MODEL, name=t + "attn_norm")
        proj = _mm([(h, wts["w_in"][l])], nt=False, name=t + "proj")
        qw2 = row(jnp.tile(wts["q_norm_w"][l], 2))
        kw2 = row(jnp.tile(wts["k_norm_w"][l], 2))
        bias = _bias_expand(wts["rel_bias"][l], name=t + "bias")
        w8 = jnp.pad(wts["conv_w"][l], ((0, 5), (0, 0)))
        sb = [_sb_fwd(proj, hp, name=t + f"sb{hp}") for hp in range(D_SB // LANES)]
        ys = [o[0] for o in sb]
        sb_carries = [o[1:] for o in sb]
        ys += [_ch_fwd(proj, qw2, kw2, bias, hp, name=t + f"ch{hp}") for hp in range(D_CH // LANES)]
        ys.append(_conv_fwd(proj, w8, name=t + "conv"))
        yn = _norm_fwd(ys, row(wts["out_norm_w"][l]), group=HEAD_DIM, name=t + "out_norm")
        x_mid = _mm([(yn, wts["w_out"][l])], nt=False, res=x, out_dtype=F32, name=t + "out_proj")
        h2 = _norm_fwd([x_mid], row(wts["ffn_norm_w"][l]), group=D_MODEL, name=t + "ffn_norm")
        g, u, a = _ffn_up(h2, wts["w_gate"][l], wts["w_up"][l], name=t + "ffn_up")
        x_out = _mm([(a, wts["w_down"][l])], nt=False, res=x_mid, out_dtype=F32, name=t + "ffn_down")
        saved.append(dict(x=x, h=h, proj=proj, sb_carries=sb_carries, qw2=qw2, kw2=kw2, bias=bias, w8=w8, ys=ys, yn=yn, x_mid=x_mid,
                          h2=h2, g=g, u=u, a=a))
        x = x_out

    dx, sq = _loss_head(x, target, name="loss_head")
    grads = {k: [None] * depth for k in
             ("attn_norm_w", "w_in", "q_norm_w", "k_norm_w", "rel_bias", "conv_w", "out_norm_w", "w_out",
              "ffn_norm_w", "w_gate", "w_up", "w_down")}
    for l in reversed(range(depth)):
        t = f"l{l}_b_"
        sv = saved[l]
        grads["w_down"][l] = _mm_tn(sv["a"], dx, name=t + "w_down")
        dg, du = _ffn_dact(dx, wts["w_down"][l], sv["g"], sv["u"], name=t + "ffn_dact")
        grads["w_gate"][l] = _mm_tn(sv["h2"], dg, name=t + "w_gate")
        grads["w_up"][l] = _mm_tn(sv["h2"], du, name=t + "w_up")
        dh2 = _mm([(dg, wts["w_gate"][l]), (du, wts["w_up"][l])], nt=True, name=t + "ffn_dh")
        dx_mid, dw8 = _norm_bwd([sv["x_mid"]], row(wts["ffn_norm_w"][l]), dh2, group=D_MODEL, res=dx,
                                name=t + "ffn_norm")
        grads["ffn_norm_w"][l] = dw8.sum(axis=0)
        grads["w_out"][l] = _mm_tn(sv["yn"], dx_mid, name=t + "w_out")
        dyn = _mm([(dx_mid, wts["w_out"][l])], nt=True, name=t + "out_dy")
        dy, dw8 = _norm_bwd(sv["ys"], row(wts["out_norm_w"][l]), dyn, group=HEAD_DIM, out_dtype=BF16,
                            name=t + "out_norm")
        grads["out_norm_w"][l] = dw8.sum(axis=0)
        proj = sv["proj"]
        n_sb, n_ch = D_SB // LANES, D_CH // LANES
        sb = [_sb_bwd(proj, sv["sb_carries"][hp], dy, hp, name=t + f"sb{hp}") for hp in range(n_sb)]
        ch = [_ch_bwd(proj, dy, sv["qw2"], sv["kw2"], sv["bias"], hp, name=t + f"ch{hp}") for hp in range(n_ch)]
        dgb, dgc, dxc, dcw = _conv_bwd(proj, dy, sv["w8"], name=t + "conv")
        grads["conv_w"][l] = dcw[:3]
        grads["q_norm_w"][l] = _fold_heads(jnp.stack([c[4] for c in ch]))
        grads["k_norm_w"][l] = _fold_heads(jnp.stack([c[5] for c in ch]))
        grads["rel_bias"][l] = _bias_fold(jnp.concatenate([c[3] for c in ch], axis=0), name=t + "bias")
        pieces = ([o[0] for o in sb] + [o[1] for o in sb] + [o[2] for o in sb]
                  + [o[0] for o in ch] + [o[1] for o in ch] + [o[2] for o in ch] + [dgb, dgc, dxc])
        dproj = jnp.concatenate([p.astype(BF16) for p in pieces], axis=1)
        grads["w_in"][l] = _mm_tn(sv["h"], dproj, name=t + "w_in")
        dh = _mm([(dproj, wts["w_in"][l])], nt=True, name=t + "proj_dh")
        dx, dw8 = _norm_bwd([sv["x"]], row(wts["attn_norm_w"][l]), dh, group=D_MODEL, res=dx_mid,
                            name=t + "attn_norm")
        grads["attn_norm_w"][l] = dw8.sum(axis=0)
    return sq, dx, {k: jnp.stack(v) for k, v in grads.items()}


SMALL = ("attn_norm_w", "q_norm_w", "k_norm_w", "rel_bias", "conv_w", "out_norm_w", "ffn_norm_w")
ORDER = ("attn_norm_w", "w_in", "q_norm_w", "k_norm_w", "rel_bias", "conv_w", "out_norm_w", "w_out",
         "ffn_norm_w", "w_gate", "w_up", "w_down")


def kernel(x, attn_norm_w, w_in, q_norm_w, k_norm_w, rel_bias, conv_w, out_norm_w, w_out, ffn_norm_w, w_gate, w_up, w_down, loss_target, m_attn_norm_w, m_w_in, m_q_norm_w, m_k_norm_w, m_rel_bias, m_conv_w, m_out_norm_w, m_w_out, m_ffn_norm_w, m_w_gate, m_w_up, m_w_down, v_attn_norm_w, v_w_in, v_q_norm_w, v_k_norm_w, v_rel_bias, v_conv_w, v_out_norm_w, v_w_out, v_ffn_norm_w, v_w_gate, v_w_up, v_w_down):
    w = dict(attn_norm_w=attn_norm_w, w_in=w_in, q_norm_w=q_norm_w, k_norm_w=k_norm_w, rel_bias=rel_bias,
             conv_w=conv_w, out_norm_w=out_norm_w, w_out=w_out, ffn_norm_w=ffn_norm_w, w_gate=w_gate, w_up=w_up,
             w_down=w_down)
    mom = dict(attn_norm_w=m_attn_norm_w, w_in=m_w_in, q_norm_w=m_q_norm_w, k_norm_w=m_k_norm_w, rel_bias=m_rel_bias,
               conv_w=m_conv_w, out_norm_w=m_out_norm_w, w_out=m_w_out, ffn_norm_w=m_ffn_norm_w, w_gate=m_w_gate,
               w_up=m_w_up, w_down=m_w_down)
    var = dict(attn_norm_w=v_attn_norm_w, w_in=v_w_in, q_norm_w=v_q_norm_w, k_norm_w=v_k_norm_w, rel_bias=v_rel_bias,
               conv_w=v_conv_w, out_norm_w=v_out_norm_w, w_out=v_w_out, ffn_norm_w=v_ffn_norm_w, w_gate=v_w_gate,
               w_up=v_w_up, w_down=v_w_down)
    depth = w_in.shape[0]
    seq = x.shape[1]
    me = 4 * lax.axis_index("x") + 2 * lax.axis_index("y") + lax.axis_index("c")
    flat2 = lambda a: a.reshape(-1, a.shape[-1])

    full = {k: w[k] for k in SMALL if k != "conv_w"}
    for k in ("w_in", "w_gate", "w_up"):
        full[k] = _cols_from_dev(_all_gather(flat2(w[k]).astype(BF16), name="gather_" + k), depth)
    for k in ("w_out", "w_down"):
        full[k] = _rows_from_dev(_all_gather(flat2(w[k]).astype(BF16), name="gather_" + k), depth)
    conv_shard = conv_w.shape[2]
    conv_all = _all_gather(_pack([conv_w]), name="gather_conv_w", space=pltpu.VMEM)
    conv_all = conv_all.reshape(N_DEV, -1)[:, :depth * 3 * conv_shard].reshape(N_DEV, depth, 3, conv_shard)
    full["conv_w"] = conv_all.transpose(1, 2, 0, 3).reshape(depth, 3, N_DEV * conv_shard)

    sq, dx, grads = _forward_backward(x.reshape(seq, D_MODEL), loss_target.reshape(seq, D_MODEL), full)
    loss = lax.psum(0.5 / D_MODEL * jnp.sum(sq), MESH_AXES)

    outs = {}
    for k in ("w_in", "w_gate", "w_up", "w_out", "w_down"):
        slabs = _cols_to_dev(grads[k]) if k in ("w_in", "w_gate", "w_up") else _rows_to_dev(grads[k])
        parts = _all_to_all(slabs, name="scatter_" + k)
        res = _adamw(parts, flat2(w[k]), flat2(mom[k]), flat2(var[k]), name="adamw_" + k)
        outs[k] = [r.reshape(w[k].shape) for r in res]

    shapes = [grads[k].shape for k in SMALL]
    col0 = me * conv_shard

    def widen(a):
        return lax.dynamic_update_slice(jnp.zeros((depth, 3, N_DEV * conv_shard), F32), a, (0, 0, col0))

    parts = _all_gather(_pack([grads[k] for k in SMALL]), name="gather_small_grads", space=pltpu.VMEM)
    pw = _pack([widen(w[k]) if k == "conv_w" else w[k] for k in SMALL])
    pm = _pack([widen(mom[k]) if k == "conv_w" else mom[k] for k in SMALL])
    pv = _pack([widen(var[k]) if k == "conv_w" else var[k] for k in SMALL])
    res = [_unpack(r, shapes) for r in _adamw(parts, pw, pm, pv, name="adamw_small")]
    for idx, k in enumerate(SMALL):
        vals = [r[idx] for r in res]
        if k == "conv_w":
            vals = [lax.dynamic_slice(a, (0, 0, col0), (depth, 3, conv_shard)) for a in vals]
        outs[k] = vals

    result = [loss, dx.reshape(x.shape)]
    for part in range(4):
        result += [outs[k][part] for k in ORDER]
    return tuple(result)
```

```python
import numpy as np
import jax
import jax.numpy as jnp
from jax import lax
from jax.experimental import pallas as pl
from jax.experimental.pallas import tpu as pltpu

F32 = jnp.float32
BF16 = jnp.bfloat16

N_DEV = 8
D_MODEL = 1024
HEAD_DIM = 64
LANES = 128
SB_BLOCK = 128
SB_GROUP = 8
CH_BLOCK = 128
CH_PAD = 512
CH_BAND = 640
REL_CLIP = 128
N_REL = 2 * REL_CLIP + 1
REL_PAD = 384
D_SB, D_CH, D_CV = 256, 512, 256
COL_QA, COL_KA, COL_VA = 0, 256, 512
COL_QB, COL_KB, COL_VB = 768, 1280, 1792
COL_GB, COL_GC, COL_XC = 2304, 2560, 2816
EPS = 1e-6
NEG = -1e30
SCALE = HEAD_DIM ** -0.5

ADAM_LR, ADAM_B1, ADAM_B2, ADAM_EPS, ADAM_WD, ADAM_STEP = 0.001, 0.9, 0.999, 1e-08, 0.01, 10

VMEM_LIMIT_MIB = 56
MM_WEIGHT_TILE_BYTES = 8 * 2 ** 20
MESH_AXES = ("x", "y", "c")


def _cparams(sem=None):
    return pltpu.CompilerParams(dimension_semantics=sem, vmem_limit_bytes=VMEM_LIMIT_MIB * 2 ** 20)


def _dot(a, b):
    return lax.dot_general(a, b, (((1,), (0,)), ((), ())), preferred_element_type=F32)


def _dot_nt(a, b):
    return lax.dot_general(a, b, (((1,), (1,)), ((), ())), preferred_element_type=F32)


def _dot_tn(a, b):
    return lax.dot_general(a, b, (((0,), (0,)), ((), ())), preferred_element_type=F32)


def _tile(n, cap):
    if n <= cap:
        return n
    best = None
    for t in range(128, cap + 1, 128):
        if n % t == 0:
            best = t
    assert best is not None, (n, cap)
    return best


def _cat(vals):
    return vals[0] if len(vals) == 1 else jnp.concatenate(vals, axis=1)


def _mm(pairs, *, nt, name, res=None, out_dtype=BF16):
    m = pairs[0][0].shape[0]
    n = pairs[0][1].shape[0] if nt else pairs[0][1].shape[1]
    tm = min(512, m)
    tn = n
    while sum(a.shape[1] for a, _ in pairs) * tn * 2 > MM_WEIGHT_TILE_BYTES and tn % 256 == 0:
        tn //= 2
    npairs = len(pairs)

    def body(*refs):
        acc = None
        for p in range(npairs):
            a = refs[2 * p][...].astype(BF16)
            b = refs[2 * p + 1][...]
            d = _dot_nt(a, b) if nt else _dot(a, b)
            acc = d if acc is None else acc + d
        if res is not None:
            acc = acc + refs[2 * npairs][...]
        refs[-1][...] = acc.astype(refs[-1].dtype)

    in_specs, args = [], []
    for a, b in pairs:
        k = a.shape[1]
        in_specs.append(pl.BlockSpec((tm, k), lambda j, i: (i, 0)))
        if nt:
            in_specs.append(pl.BlockSpec((tn, k), lambda j, i: (j, 0)))
        else:
            in_specs.append(pl.BlockSpec((k, tn), lambda j, i: (0, j)))
        args += [a, b]
    if res is not None:
        in_specs.append(pl.BlockSpec((tm, tn), lambda j, i: (i, j)))
        args.append(res)
    return pl.pallas_call(
        body, grid=(n // tn, m // tm), in_specs=in_specs,
        out_specs=pl.BlockSpec((tm, tn), lambda j, i: (i, j)),
        out_shape=jax.ShapeDtypeStruct((m, n), out_dtype),
        compiler_params=_cparams(("parallel", "parallel")), name=name)(*args)


def _mm_tn(a, g, *, name):
    s, k = a.shape
    n = g.shape[1]
    tk, tn, ts = _tile(k, 1408), _tile(n, 1536), min(512, s)

    def body(a_ref, g_ref, o_ref):
        @pl.when(pl.program_id(2) == 0)
        def _():
            o_ref[...] = jnp.zeros_like(o_ref)
        o_ref[...] += _dot_tn(a_ref[...].astype(BF16), g_ref[...].astype(BF16))

    return pl.pallas_call(
        body, grid=(k // tk, n // tn, s // ts),
        in_specs=[pl.BlockSpec((ts, tk), lambda i, j, t: (t, i)),
                  pl.BlockSpec((ts, tn), lambda i, j, t: (t, j))],
        out_specs=pl.BlockSpec((tk, tn), lambda i, j, t: (i, j)),
        out_shape=jax.ShapeDtypeStruct((k, n), F32),
        compiler_params=_cparams(("parallel", "parallel", "arbitrary")), name=name)(a, g)


def _block_diag(width, group):
    r = lax.broadcasted_iota(jnp.int32, (width, width), 0) // group
    c = lax.broadcasted_iota(jnp.int32, (width, width), 1) // group
    return (r == c).astype(BF16)


def _group_mean(v, bd, group):
    hi = v.astype(BF16)
    lo = (v - hi.astype(F32)).astype(BF16)
    return (_dot(hi, bd) + _dot(lo, bd)) * (1.0 / group)


def _norm_fwd(xs, w, *, group, name):
    s = xs[0].shape[0]
    width = sum(x.shape[1] for x in xs)
    tm = min(512, s)
    nx = len(xs)
    grouped = group != width

    def body(*refs):
        x = _cat([r[...].astype(F32) for r in refs[:nx]])
        if grouped:
            ms = _group_mean(x * x, refs[nx + 1][...], group)
        else:
            ms = jnp.mean(x * x, axis=1, keepdims=True)
        refs[-1][...] = (x * lax.rsqrt(ms + EPS) * refs[nx][...]).astype(BF16)

    in_specs = [pl.BlockSpec((tm, x.shape[1]), lambda i: (i, 0)) for x in xs]
    in_specs.append(pl.BlockSpec((1, width), lambda i: (0, 0)))
    args = list(xs) + [w]
    if grouped:
        in_specs.append(pl.BlockSpec((width, width), lambda i: (0, 0)))
        args.append(_block_diag(width, group))
    return pl.pallas_call(
        body, grid=(s // tm,), in_specs=in_specs,
        out_specs=pl.BlockSpec((tm, width), lambda i: (i, 0)),
        out_shape=jax.ShapeDtypeStruct((s, width), BF16),
        compiler_params=_cparams(("parallel",)), name=name)(*args)


def _norm_bwd(xs, w, dy, *, group, name, res=None, out_dtype=F32):
    s = xs[0].shape[0]
    width = sum(x.shape[1] for x in xs)
    tm = min(512, s)
    nx = len(xs)
    grouped = group != width
    has_res = res is not None

    def body(*refs):
        x = _cat([r[...].astype(F32) for r in refs[:nx]])
        w_ref, dy_ref = refs[nx], refs[nx + 1]
        pos = nx + 2
        if has_res:
            res_ref = refs[pos]
            pos += 1
        if grouped:
            bd = refs[pos][...]
            mean = lambda v: _group_mean(v, bd, group)
        else:
            mean = lambda v: jnp.mean(v, axis=1, keepdims=True)
        dx_ref, dw_ref = refs[-2], refs[-1]
        inv = lax.rsqrt(mean(x * x) + EPS)
        xh = x * inv
        d = dy_ref[...].astype(F32)
        g = d * w_ref[...]
        dx = inv * (g - xh * mean(g * xh))
        if has_res:
            dx = dx + res_ref[...]
        dx_ref[...] = dx.astype(dx_ref.dtype)

        @pl.when(pl.program_id(0) == 0)
        def _():
            dw_ref[...] = jnp.zeros_like(dw_ref)
        dw_ref[...] += (d * xh).reshape(tm // 8, 8, width).sum(axis=0)

    in_specs = [pl.BlockSpec((tm, x.shape[1]), lambda i: (i, 0)) for x in xs]
    in_specs += [pl.BlockSpec((1, width), lambda i: (0, 0)), pl.BlockSpec((tm, width), lambda i: (i, 0))]
    args = list(xs) + [w, dy]
    if has_res:
        in_specs.append(pl.BlockSpec((tm, width), lambda i: (i, 0)))
        args.append(res)
    if grouped:
        in_specs.append(pl.BlockSpec((width, width), lambda i: (0, 0)))
        args.append(_block_diag(width, group))
    return pl.pallas_call(
        body, grid=(s // tm,), in_specs=in_specs,
        out_specs=(pl.BlockSpec((tm, width), lambda i: (i, 0)), pl.BlockSpec((8, width), lambda i: (0, 0))),
        out_shape=(jax.ShapeDtypeStruct((s, width), out_dtype), jax.ShapeDtypeStruct((8, width), F32)),
        compiler_params=_cparams(("arbitrary",)), name=name)(*args)


def _head0_lanes(shape):
    return lax.broadcasted_iota(jnp.int32, shape, len(shape) - 1) < HEAD_DIM


def _split_heads(v):
    m0 = _head0_lanes(v.shape)
    zero = jnp.zeros_like(v)
    return jnp.where(m0, v, zero), jnp.where(m0, zero, v)


def _pair_sum(v):
    m0 = _head0_lanes(v.shape)
    s0 = jnp.sum(jnp.where(m0, v, 0.0), axis=1, keepdims=True)
    s1 = jnp.sum(jnp.where(m0, 0.0, v), axis=1, keepdims=True)
    return jnp.where(m0, s0, s1)


def _pair_norm(x, w2):
    inv = lax.rsqrt(_pair_sum(x * x) * (1.0 / HEAD_DIM) + EPS)
    xh = x * inv
    return xh * w2, xh, inv


def _pair_norm_bwd(dy, xh, inv, w2):
    g = dy * w2
    return inv * (g - xh * (_pair_sum(g * xh) * (1.0 / HEAD_DIM)))


def _copy_in(copies):
    for c in copies:
        c.start()
    for c in copies:
        c.wait()


def _stack(vals):
    return jnp.concatenate(list(vals), axis=0)


def _unstack(results, chains):
    blocks = []
    for res in results:
        blocks += [res[n * SB_BLOCK:(n + 1) * SB_BLOCK] for n in range(res.shape[0] // SB_BLOCK)]
    return dict(zip(chains, blocks))


def _neg_softplus(z):
    return -(jnp.maximum(z, 0.0) + jnp.log(1.0 + jnp.exp(-jnp.abs(z))))


def _sb_fwd(proj, hp, *, name):
    s = proj.shape[0]
    nq = s // SB_BLOCK
    assert nq <= LANES and nq % SB_GROUP == 0
    shape = (SB_BLOCK, SB_BLOCK)

    def body(q_ref, proj_hbm, y_ref, c0_ref, c1_ref, k_scr, v_scr, sems):
        i = pl.program_id(0)

        @pl.when(i == 0)
        def _():
            _copy_in([pltpu.make_async_copy(proj_hbm.at[:, pl.ds(COL_KA + hp * LANES, LANES)], k_scr, sems.at[0]),
                      pltpu.make_async_copy(proj_hbm.at[:, pl.ds(COL_VA + hp * LANES, LANES)], v_scr, sems.at[1])])

        row = lax.broadcasted_iota(jnp.int32, shape, 0)
        col = lax.broadcasted_iota(jnp.int32, shape, 1)
        m0 = col < HEAD_DIM
        later_keys = (row > col).astype(BF16)
        q_both = _stack(_split_heads(q_ref[...] * SCALE))

        def group(g, carry, masked):
            c, acc, seen = list(carry[:2]), carry[2], list(carry[3:5])
            js = [g * SB_GROUP + b for b in reversed(range(SB_GROUP))]
            rows = [pl.ds(pl.multiple_of(j * SB_BLOCK, SB_BLOCK), SB_BLOCK) for j in js]
            chains = [(b, h) for b in range(SB_GROUP) for h in range(2)]
            earlier = [(col + j * SB_BLOCK < row + i * SB_BLOCK) if masked else None for j in js]
            z = _unstack([_dot_nt(q_both, k_scr[rows[b], :]) for b in range(SB_GROUP)], chains)
            lr = {ch: _neg_softplus(z[ch]) for ch in chains}
            lrm = {(b, h): jnp.where(earlier[b], lr[b, h], 0.0) if masked else lr[b, h] for b, h in chains}
            after = _unstack([_dot(_stack([lrm[ch].astype(BF16) for ch in chains]), later_keys)], chains)
            w = {}
            for b, h in chains:
                seen[h] = jnp.where(col == js[b], c[h], seen[h])
                wv = jnp.exp(z[b, h] + lr[b, h] + (after[b, h] + c[h]))
                w[b, h] = jnp.where(earlier[b], wv, 0.0) if masked else wv
                c[h] = c[h] + jnp.sum(lrm[b, h], axis=1, keepdims=True)
            pv = _unstack([_dot(_stack([w[b, 0].astype(BF16), w[b, 1].astype(BF16)]), v_scr[rows[b], :])
                           for b in range(SB_GROUP)], chains)
            for b in range(SB_GROUP):
                acc = acc + jnp.where(m0, pv[b, 0], pv[b, 1])
            return c[0], c[1], acc, seen[0], seen[1]

        zc = jnp.zeros((SB_BLOCK, 1), F32)
        zt = jnp.zeros(shape, F32)
        gi = i // SB_GROUP
        carry = group(gi, (zc, zc, zt, zt, zt), True)
        carry = lax.fori_loop(0, gi, lambda t, cr: group(gi - 1 - t, cr, False), carry)
        y_ref[...] = carry[2].astype(y_ref.dtype)
        c0_ref[...] = carry[3]
        c1_ref[...] = carry[4]

    return pl.pallas_call(
        body, grid=(nq,),
        in_specs=[pl.BlockSpec(shape, lambda i: (i, COL_QA // LANES + hp)), pl.BlockSpec(memory_space=pl.ANY)],
        out_specs=(pl.BlockSpec(shape, lambda i: (i, 0)),) * 3,
        out_shape=(jax.ShapeDtypeStruct((s, LANES), BF16), jax.ShapeDtypeStruct((s, LANES), F32),
                   jax.ShapeDtypeStruct((s, LANES), F32)),
        scratch_shapes=[pltpu.VMEM((s, LANES), BF16), pltpu.VMEM((s, LANES), BF16), pltpu.SemaphoreType.DMA((2,))],
        compiler_params=_cparams(("arbitrary",)), name=name)(proj, proj)


def _sb_bwd(proj, carries, dy_all, hp, *, name):
    s = proj.shape[0]
    nq = s // SB_BLOCK
    shape = (SB_BLOCK, SB_BLOCK)

    def body(q_ref, c0_ref, c1_ref, dy_ref, proj_hbm, dq_ref, dk_hbm, dv_hbm, k_scr, v_scr, dk_scr, dv_scr, sems):
        i = pl.program_id(0)

        @pl.when(i == 0)
        def _():
            _copy_in([pltpu.make_async_copy(proj_hbm.at[:, pl.ds(COL_KA + hp * LANES, LANES)], k_scr, sems.at[0]),
                      pltpu.make_async_copy(proj_hbm.at[:, pl.ds(COL_VA + hp * LANES, LANES)], v_scr, sems.at[1])])
            dk_scr[...] = jnp.zeros_like(dk_scr)
            dv_scr[...] = jnp.zeros_like(dv_scr)

        row = lax.broadcasted_iota(jnp.int32, shape, 0)
        col = lax.broadcasted_iota(jnp.int32, shape, 1)
        m0 = col < HEAD_DIM
        later_keys = (row > col).astype(BF16)
        earlier_keys = (row < col).astype(BF16)
        qh = _split_heads(q_ref[...] * SCALE)
        dyh = _split_heads(dy_ref[...])
        q_both, dy_both = _stack(qh), _stack(dyh)
        seen = (c0_ref[...], c1_ref[...])

        def group(g, carry, masked):
            r, dq = list(carry[0:2]), list(carry[2:4])
            js = [g * SB_GROUP + b for b in range(SB_GROUP)]
            rows = [pl.ds(pl.multiple_of(j * SB_BLOCK, SB_BLOCK), SB_BLOCK) for j in js]
            blocks = range(SB_GROUP)
            chains = [(b, h) for b in blocks for h in range(2)]
            earlier = [(col + j * SB_BLOCK < row + i * SB_BLOCK) if masked else None for j in js]
            z = _unstack([_dot_nt(q_both, k_scr[rows[b], :]) for b in blocks], chains)
            dw = _unstack([_dot_nt(dy_both, v_scr[rows[b], :]) for b in blocks], chains)
            lr = {ch: _neg_softplus(z[ch]) for ch in chains}
            lrm = {(b, h): jnp.where(earlier[b], lr[b, h], 0.0) if masked else lr[b, h] for b, h in chains}
            after = _unstack([_dot(_stack([lrm[ch].astype(BF16) for ch in chains]), later_keys)], chains)
            w, de = {}, {}
            for b, h in chains:
                c = jnp.sum(jnp.where(col == js[b], seen[h], 0.0), axis=1, keepdims=True)
                wv = jnp.exp(z[b, h] + lr[b, h] + (after[b, h] + c))
                w[b, h] = jnp.where(earlier[b], wv, 0.0) if masked else wv
                de[b, h] = dw[b, h] * w[b, h]
            before = _unstack([_dot(_stack([de[ch].astype(BF16) for ch in chains]), earlier_keys)], chains)
            dzb = {}
            for b, h in chains:
                dz = de[b, h] - jnp.exp(z[b, h] + lr[b, h]) * (de[b, h] + (r[h] + before[b, h]))
                if masked:
                    dz = jnp.where(earlier[b], dz, 0.0)
                dzb[b, h] = dz.astype(BF16)
                r[h] = r[h] + jnp.sum(de[b, h], axis=1, keepdims=True)
            keys = pl.ds(pl.multiple_of(g * (SB_GROUP * SB_BLOCK), SB_BLOCK), SB_GROUP * SB_BLOCK)
            dz_wide = [jnp.concatenate([dzb[b, h] for b in blocks], axis=1) for h in range(2)]
            w_wide = [jnp.concatenate([w[b, h].astype(BF16) for b in blocks], axis=1) for h in range(2)]
            dqc = _dot(_stack(dz_wide), k_scr[keys, :])
            dq = [dq[0] + dqc[:SB_BLOCK], dq[1] + dqc[SB_BLOCK:]]
            dk_scr[keys, :] += _dot_tn(_stack(dz_wide), q_both)
            dv_scr[keys, :] += _dot_tn(_stack(w_wide), dy_both)
            return r[0], r[1], dq[0], dq[1]

        zc = jnp.zeros((SB_BLOCK, 1), F32)
        zq = jnp.zeros(shape, F32)
        gi = i // SB_GROUP
        carry = lax.fori_loop(0, gi, lambda t, cr: group(t, cr, False), (zc, zc, zq, zq))
        carry = group(gi, carry, True)
        dq_ref[...] = (jnp.where(m0, carry[2], carry[3]) * SCALE).astype(dq_ref.dtype)

        @pl.when(i == nq - 1)
        def _():
            _copy_in([pltpu.make_async_copy(dk_scr, dk_hbm, sems.at[0]),
                      pltpu.make_async_copy(dv_scr, dv_hbm, sems.at[1])])

    blk = lambda c0: pl.BlockSpec(shape, lambda i: (i, c0))
    return pl.pallas_call(
        body, grid=(nq,),
        in_specs=[blk(COL_QA // LANES + hp), blk(0), blk(0), blk(hp), pl.BlockSpec(memory_space=pl.ANY)],
        out_specs=(blk(0), pl.BlockSpec(memory_space=pl.ANY), pl.BlockSpec(memory_space=pl.ANY)),
        out_shape=(jax.ShapeDtypeStruct((s, LANES), BF16), jax.ShapeDtypeStruct((s, LANES), F32),
                   jax.ShapeDtypeStruct((s, LANES), F32)),
        scratch_shapes=[pltpu.VMEM((s, LANES), BF16), pltpu.VMEM((s, LANES), BF16),
                        pltpu.VMEM((s, LANES), F32), pltpu.VMEM((s, LANES), F32), pltpu.SemaphoreType.DMA((2,))],
        compiler_params=_cparams(("arbitrary",)), name=name)(proj, carries[0], carries[1], dy_all, proj)


def _rel_index():
    p = np.arange(CH_BLOCK)[:, None]
    m = np.arange(CH_BAND)[None, :]
    return (np.clip(CH_PAD + p - m, -REL_CLIP, REL_CLIP) + REL_CLIP).astype(np.int32)


def _bias_expand(rel_bias, *, name):
    nh = rel_bias.shape[0]
    table = jnp.pad(rel_bias, ((0, 0), (0, REL_PAD - N_REL)))

    def body(tab_ref, rel_ref, out_ref):
        rid = lax.broadcasted_iota(jnp.int32, (REL_PAD, CH_BAND), 0)

        def step(p, _):
            onehot = (rid == rel_ref[pl.ds(p, 1), :]).astype(F32)
            out_ref[p] = lax.dot_general(tab_ref[...], onehot, (((1,), (0,)), ((), ())),
                                         precision=lax.Precision.HIGHEST, preferred_element_type=F32)
            return 0

        lax.fori_loop(0, CH_BLOCK, step, 0)

    out = pl.pallas_call(
        body, out_shape=jax.ShapeDtypeStruct((CH_BLOCK, nh, CH_BAND), F32),
        compiler_params=_cparams(), name=name)(table, jnp.asarray(_rel_index()))
    return out.transpose(1, 0, 2)


def _bias_fold(dbias, *, name):
    nh = dbias.shape[0]

    def body(db_ref, rel_ref, out_ref):
        rid = lax.broadcasted_iota(jnp.int32, (REL_PAD, CH_BAND), 0)

        def step(p, acc):
            onehot = (rid == rel_ref[pl.ds(p, 1), :]).astype(F32)
            return acc + lax.dot_general(db_ref[p], onehot, (((1,), (1,)), ((), ())),
                                         precision=lax.Precision.HIGHEST, preferred_element_type=F32)

        out_ref[...] = lax.fori_loop(0, CH_BLOCK, step, jnp.zeros((nh, REL_PAD), F32))

    out = pl.pallas_call(
        body, out_shape=jax.ShapeDtypeStruct((nh, REL_PAD), F32),
        compiler_params=_cparams(), name=name)(dbias.transpose(1, 0, 2), jnp.asarray(_rel_index()))
    return out[:, :N_REL]


def _band_valid(i):
    row = lax.broadcasted_iota(jnp.int32, (CH_BLOCK, CH_BAND), 0)
    col = lax.broadcasted_iota(jnp.int32, (CH_BLOCK, CH_BAND), 1)
    first = row < CH_BLOCK // 2
    window = (first & (col < CH_BAND - 64)) | (jnp.logical_not(first) & (col >= 64))
    return window & (col + i * CH_BLOCK >= CH_PAD)


def _ch_load(proj_hbm, hp, s, kw_ref, kraw_scr, kn_scr, vp_scr, sems):
    _copy_in([pltpu.make_async_copy(proj_hbm.at[:, pl.ds(COL_KB + hp * LANES, LANES)], kraw_scr, sems.at[0]),
              pltpu.make_async_copy(proj_hbm.at[:, pl.ds(COL_VB + hp * LANES, LANES)],
                                    vp_scr.at[pl.ds(CH_PAD, s), :], sems.at[1])])
    kn_scr[0:CH_PAD, :] = jnp.zeros((CH_PAD, LANES), BF16)
    vp_scr[0:CH_PAD, :] = jnp.zeros((CH_PAD, LANES), BF16)
    rb = min(512, s)

    def step(t, _):
        r = pl.multiple_of(t * rb, rb)
        kn = _pair_norm(kraw_scr[pl.ds(r, rb), :].astype(F32), kw_ref[...])[0]
        kn_scr[pl.ds(CH_PAD + r, rb), :] = kn.astype(BF16)
        return 0

    lax.fori_loop(0, s // rb, step, 0)


def _ch_probs(qk, bias, valid):
    sc = jnp.where(valid, qk * SCALE + bias, NEG)
    p = jnp.exp(sc - jnp.max(sc, axis=1, keepdims=True))
    return p, 1.0 / jnp.sum(p, axis=1, keepdims=True)


def _ch_fwd(proj, qw2, kw2, bias, hp, *, name):
    s = proj.shape[0]
    nb = s // CH_BLOCK
    shape = (CH_BLOCK, LANES)

    def body(q_ref, proj_hbm, qw_ref, kw_ref, bias_ref, y_ref, kraw_scr, kn_scr, vp_scr, sems):
        i = pl.program_id(0)

        @pl.when(i == 0)
        def _():
            _ch_load(proj_hbm, hp, s, kw_ref, kraw_scr, kn_scr, vp_scr, sems)

        band = pl.ds(pl.multiple_of(i * CH_BLOCK, CH_BLOCK), CH_BAND)
        kb, vb = kn_scr[band, :], vp_scr[band, :]
        valid = _band_valid(i)
        qn = _pair_norm(q_ref[...].astype(F32), qw_ref[...])[0]
        scores = _dot_nt(_stack(_split_heads(qn.astype(BF16))), kb)
        probs = [_ch_probs(scores[h * CH_BLOCK:(h + 1) * CH_BLOCK], bias_ref[h], valid) for h in range(2)]
        outs = _dot(_stack([p.astype(BF16) for p, _ in probs]), vb)
        outs = [outs[h * CH_BLOCK:(h + 1) * CH_BLOCK] * probs[h][1] for h in range(2)]
        y_ref[...] = jnp.where(_head0_lanes(shape), outs[0], outs[1]).astype(y_ref.dtype)

    return pl.pallas_call(
        body, grid=(nb,),
        in_specs=[pl.BlockSpec(shape, lambda i: (i, COL_QB // LANES + hp)), pl.BlockSpec(memory_space=pl.ANY),
                  pl.BlockSpec((1, LANES), lambda i: (0, 0)), pl.BlockSpec((1, LANES), lambda i: (0, 0)),
                  pl.BlockSpec((2, CH_BLOCK, CH_BAND), lambda i: (hp, 0, 0))],
        out_specs=pl.BlockSpec(shape, lambda i: (i, 0)),
        out_shape=jax.ShapeDtypeStruct((s, LANES), BF16),
        scratch_shapes=[pltpu.VMEM((s, LANES), BF16), pltpu.VMEM((s + CH_PAD, LANES), BF16),
                        pltpu.VMEM((s + CH_PAD, LANES), BF16), pltpu.SemaphoreType.DMA((2,))],
        compiler_params=_cparams(("arbitrary",)), name=name)(proj, proj, qw2, kw2, bias)


def _ch_bwd(proj, dy_all, qw2, kw2, bias, hp, *, name):
    s = proj.shape[0]
    nb = s // CH_BLOCK
    shape = (CH_BLOCK, LANES)
    rb = min(512, s)

    def body(q_ref, dy_ref, proj_hbm, qw_ref, kw_ref, bias_ref, dq_ref, dk_hbm, dv_hbm, dbias_ref, dqw_ref, dkw_ref,
             kraw_scr, kn_scr, vp_scr, dkn_scr, dvp_scr, sems):
        i = pl.program_id(0)

        @pl.when(i == 0)
        def _():
            _ch_load(proj_hbm, hp, s, kw_ref, kraw_scr, kn_scr, vp_scr, sems)
            dkn_scr[...] = jnp.zeros_like(dkn_scr)
            dvp_scr[...] = jnp.zeros_like(dvp_scr)
            dbias_ref[...] = jnp.zeros_like(dbias_ref)
            dqw_ref[...] = jnp.zeros_like(dqw_ref)

        band = pl.ds(pl.multiple_of(i * CH_BLOCK, CH_BLOCK), CH_BAND)
        kb, vb = kn_scr[band, :], vp_scr[band, :]
        valid = _band_valid(i)
        qn, qhat, qinv = _pair_norm(q_ref[...].astype(F32), qw_ref[...])
        q_both = _stack(_split_heads(qn.astype(BF16)))
        dy_both = _stack(_split_heads(dy_ref[...]))
        scores = _dot_nt(q_both, kb)
        dprobs = _dot_nt(dy_both, vb)
        pb, dsb = [], []
        for h in range(2):
            part = slice(h * CH_BLOCK, (h + 1) * CH_BLOCK)
            p, rl = _ch_probs(scores[part], bias_ref[h], valid)
            p = p * rl
            dp = dprobs[part]
            ds = p * (dp - jnp.sum(dp * p, axis=1, keepdims=True))
            dbias_ref[h] += ds
            pb.append(p.astype(BF16))
            dsb.append((ds * SCALE).astype(BF16))
        dsb = _stack(dsb)
        dqn = _dot(dsb, kb)
        dkn_scr[band, :] += _dot_tn(dsb, q_both)
        dvp_scr[band, :] += _dot_tn(_stack(pb), dy_both)
        dqn = jnp.where(_head0_lanes(shape), dqn[:CH_BLOCK], dqn[CH_BLOCK:])
        dq_ref[...] = _pair_norm_bwd(dqn, qhat, qinv, qw_ref[...]).astype(dq_ref.dtype)
        dqw_ref[...] += (dqn * qhat).reshape(CH_BLOCK // 8, 8, LANES).sum(axis=0)

        @pl.when(i == nb - 1)
        def _():
            def step(t, acc):
                r = pl.multiple_of(t * rb, rb)
                rows = pl.ds(CH_PAD + r, rb)
                _, khat, kinv = _pair_norm(kraw_scr[pl.ds(r, rb), :].astype(F32), kw_ref[...])
                dkn = dkn_scr[rows, :]
                dkn_scr[rows, :] = _pair_norm_bwd(dkn, khat, kinv, kw_ref[...])
                return acc + (dkn * khat).reshape(rb // 8, 8, LANES).sum(axis=0)

            dkw_ref[...] = lax.fori_loop(0, s // rb, step, jnp.zeros((8, LANES), F32))
            _copy_in([pltpu.make_async_copy(dkn_scr.at[pl.ds(CH_PAD, s), :], dk_hbm, sems.at[0]),
                      pltpu.make_async_copy(dvp_scr.at[pl.ds(CH_PAD, s), :], dv_hbm, sems.at[1])])

    blk = lambda c0: pl.BlockSpec(shape, lambda i: (i, c0))
    vec = pl.BlockSpec((1, LANES), lambda i: (0, 0))
    part = pl.BlockSpec((8, LANES), lambda i: (0, 0))
    hbm = pl.BlockSpec(memory_space=pl.ANY)
    return pl.pallas_call(
        body, grid=(nb,),
        in_specs=[blk(COL_QB // LANES + hp), blk(D_SB // LANES + hp), hbm, vec, vec,
                  pl.BlockSpec((2, CH_BLOCK, CH_BAND), lambda i: (hp, 0, 0))],
        out_specs=(blk(0), hbm, hbm, pl.BlockSpec((2, CH_BLOCK, CH_BAND), lambda i: (0, 0, 0)), part, part),
        out_shape=(jax.ShapeDtypeStruct((s, LANES), BF16), jax.ShapeDtypeStruct((s, LANES), F32),
                   jax.ShapeDtypeStruct((s, LANES), F32), jax.ShapeDtypeStruct((2, CH_BLOCK, CH_BAND), F32),
                   jax.ShapeDtypeStruct((8, LANES), F32), jax.ShapeDtypeStruct((8, LANES), F32)),
        scratch_shapes=[pltpu.VMEM((s, LANES), BF16), pltpu.VMEM((s + CH_PAD, LANES), BF16),
                        pltpu.VMEM((s + CH_PAD, LANES), BF16), pltpu.VMEM((s + CH_PAD, LANES), F32),
                        pltpu.VMEM((s + CH_PAD, LANES), F32), pltpu.SemaphoreType.DMA((2,))],
        compiler_params=_cparams(("arbitrary",)), name=name)(proj, dy_all, proj, qw2, kw2, bias)


def _shift_down(x, halo, k):
    out = pltpu.roll(x, k, 0)
    row = lax.broadcasted_iota(jnp.int32, x.shape, 0)
    for t in range(k):
        out = jnp.where(row == t, halo[8 - k + t:8 - k + t + 1, :], out)
    return out


def _shift_up(x, halo, k):
    n = x.shape[0]
    out = pltpu.roll(x, n - k, 0)
    row = lax.broadcasted_iota(jnp.int32, x.shape, 0)
    for t in range(k):
        out = jnp.where(row == n - k + t, halo[t:t + 1, :], out)
    return out


def _conv_specs(s, tm):
    cw = D_CV
    tile = lambda c0: pl.BlockSpec((tm, cw), lambda i: (i, c0))
    above = lambda c0: pl.BlockSpec((8, cw), lambda i: (jnp.maximum(i * (tm // 8) - 1, 0), c0))
    below = lambda c0: pl.BlockSpec((8, cw), lambda i: (jnp.minimum((i + 1) * (tm // 8), s // 8 - 1), c0))
    return tile, above, below


def _conv_fwd(proj, w8, *, name):
    s = proj.shape[0]
    tm = min(512, s)
    tile, above, _ = _conv_specs(s, tm)

    def body(gb_ref, gc_ref, xc_ref, gca_ref, xca_ref, w_ref, y_ref):
        hc = gc_ref[...].astype(F32) * xc_ref[...].astype(F32)
        top = jnp.where(pl.program_id(0) > 0, gca_ref[...].astype(F32) * xca_ref[...].astype(F32), 0.0)
        u = w_ref[0:1, :] * _shift_down(hc, top, 2) + w_ref[1:2, :] * _shift_down(hc, top, 1) + w_ref[2:3, :] * hc
        y_ref[...] = (gb_ref[...].astype(F32) * u).astype(y_ref.dtype)

    cb, cc, cx = COL_GB // D_CV, COL_GC // D_CV, COL_XC // D_CV
    return pl.pallas_call(
        body, grid=(s // tm,),
        in_specs=[tile(cb), tile(cc), tile(cx), above(cc), above(cx), pl.BlockSpec((8, D_CV), lambda i: (0, 0))],
        out_specs=pl.BlockSpec((tm, D_CV), lambda i: (i, 0)),
        out_shape=jax.ShapeDtypeStruct((s, D_CV), BF16),
        compiler_params=_cparams(("parallel",)), name=name)(proj, proj, proj, proj, proj, w8)


def _conv_bwd(proj, dy_all, w8, *, name):
    s = proj.shape[0]
    tm = min(512, s)
    nt = s // tm
    tile, above, below = _conv_specs(s, tm)

    def body(gb_ref, gc_ref, xc_ref, dy_ref, gca_ref, xca_ref, gbb_ref, dyb_ref, w_ref,
             dgb_ref, dgc_ref, dxc_ref, dw_ref):
        i = pl.program_id(0)
        gb, gc, xc = gb_ref[...].astype(F32), gc_ref[...].astype(F32), xc_ref[...].astype(F32)
        dy = dy_ref[...].astype(F32)
        hc = gc * xc
        top = jnp.where(i > 0, gca_ref[...].astype(F32) * xca_ref[...].astype(F32), 0.0)
        hc1, hc2 = _shift_down(hc, top, 1), _shift_down(hc, top, 2)
        u = w_ref[0:1, :] * hc2 + w_ref[1:2, :] * hc1 + w_ref[2:3, :] * hc
        du = dy * gb
        bottom = jnp.where(i < nt - 1, dyb_ref[...].astype(F32) * gbb_ref[...].astype(F32), 0.0)
        dhc = w_ref[2:3, :] * du + w_ref[1:2, :] * _shift_up(du, bottom, 1) + w_ref[0:1, :] * _shift_up(du, bottom, 2)
        dgb_ref[...] = (dy * u).astype(dgb_ref.dtype)
        dgc_ref[...] = (dhc * xc).astype(dgc_ref.dtype)
        dxc_ref[...] = (dhc * gc).astype(dxc_ref.dtype)

        @pl.when(i == 0)
        def _():
            dw_ref[...] = jnp.zeros_like(dw_ref)
        dw_ref[0:1, :] += jnp.sum(du * hc2, axis=0, keepdims=True)
        dw_ref[1:2, :] += jnp.sum(du * hc1, axis=0, keepdims=True)
        dw_ref[2:3, :] += jnp.sum(du * hc, axis=0, keepdims=True)

    cb, cc, cx = COL_GB // D_CV, COL_GC // D_CV, COL_XC // D_CV
    cy = (D_SB + D_CH) // D_CV
    out_tile = pl.BlockSpec((tm, D_CV), lambda i: (i, 0))
    act = jax.ShapeDtypeStruct((s, D_CV), BF16)
    return pl.pallas_call(
        body, grid=(nt,),
        in_specs=[tile(cb), tile(cc), tile(cx), tile(cy), above(cc), above(cx), below(cb), below(cy),
                  pl.BlockSpec((8, D_CV), lambda i: (0, 0))],
        out_specs=(out_tile, out_tile, out_tile, pl.BlockSpec((8, D_CV), lambda i: (0, 0))),
        out_shape=(act, act, act, jax.ShapeDtypeStruct((8, D_CV), F32)),
        compiler_params=_cparams(("arbitrary",)), name=name)(proj, proj, proj, dy_all, proj, proj, proj, dy_all, w8)


def _ffn_up(h, wg, wu, *, name):
    s, d = h.shape
    f = wg.shape[1]
    tm, tn = min(512, s), _tile(f, 1408)

    def body(h_ref, wg_ref, wu_ref, g_ref, u_ref, a_ref):
        hv = h_ref[...]
        g = _dot(hv, wg_ref[...])
        u = _dot(hv, wu_ref[...])
        g_ref[...] = g.astype(BF16)
        u_ref[...] = u.astype(BF16)
        a_ref[...] = (g * jax.nn.sigmoid(g) * u).astype(BF16)

    wspec = pl.BlockSpec((d, tn), lambda j, i: (0, j))
    ospec = pl.BlockSpec((tm, tn), lambda j, i: (i, j))
    act = jax.ShapeDtypeStruct((s, f), BF16)
    return pl.pallas_call(
        body, grid=(f // tn, s // tm),
        in_specs=[pl.BlockSpec((tm, d), lambda j, i: (i, 0)), wspec, wspec],
        out_specs=(ospec, ospec, ospec), out_shape=(act, act, act),
        compiler_params=_cparams(("parallel", "parallel")), name=name)(h, wg, wu)


def _ffn_dact(dx, wd, g, u, *, name):
    s, d = dx.shape
    f = wd.shape[0]
    tm, tf = min(512, s), _tile(f, 1408)

    def body(dx_ref, wd_ref, g_ref, u_ref, dg_ref, du_ref):
        da = _dot_nt(dx_ref[...].astype(BF16), wd_ref[...])
        gv, uv = g_ref[...].astype(F32), u_ref[...].astype(F32)
        sg = jax.nn.sigmoid(gv)
        dg_ref[...] = (da * uv * sg * (1.0 + gv * (1.0 - sg))).astype(BF16)
        du_ref[...] = (da * gv * sg).astype(BF16)

    tspec = pl.BlockSpec((tm, tf), lambda j, i: (i, j))
    act = jax.ShapeDtypeStruct((s, f), BF16)
    return pl.pallas_call(
        body, grid=(f // tf, s // tm),
        in_specs=[pl.BlockSpec((tm, d), lambda j, i: (i, 0)), pl.BlockSpec((tf, d), lambda j, i: (j, 0)), tspec, tspec],
        out_specs=(tspec, tspec), out_shape=(act, act),
        compiler_params=_cparams(("parallel", "parallel")), name=name)(dx, wd, g, u)


def _loss_head(y, target, *, name):
    s, d = y.shape
    tm = min(512, s)

    def body(y_ref, t_ref, dy_ref, sq_ref):
        err = y_ref[...] - t_ref[...]
        dy_ref[...] = err * (1.0 / d)

        @pl.when(pl.program_id(0) == 0)
        def _():
            sq_ref[...] = jnp.zeros_like(sq_ref)
        sq_ref[...] += (err * err).reshape(tm // 8, 8, d).sum(axis=0)

    tile = pl.BlockSpec((tm, d), lambda i: (i, 0))
    return pl.pallas_call(
        body, grid=(s // tm,), in_specs=[tile, tile],
        out_specs=(tile, pl.BlockSpec((8, d), lambda i: (0, 0))),
        out_shape=(jax.ShapeDtypeStruct((s, d), F32), jax.ShapeDtypeStruct((8, d), F32)),
        compiler_params=_cparams(("arbitrary",)), name=name)(y, target)


def _adamw(parts, w, m, v, *, name):
    r, c = w.shape
    tr = r if r <= 512 else _tile_rows(r)

    def body(p_ref, w_ref, m_ref, v_ref, g_ref, d_ref, nm_ref, nv_ref):
        g = p_ref[0].astype(F32)
        for dev in range(1, N_DEV):
            g = g + p_ref[dev].astype(F32)
        nm = ADAM_B1 * m_ref[...] + (1.0 - ADAM_B1) * g
        nv = ADAM_B2 * v_ref[...] + (1.0 - ADAM_B2) * (g * g)
        m_hat = nm / (1.0 - ADAM_B1 ** ADAM_STEP)
        v_hat = nv / (1.0 - ADAM_B2 ** ADAM_STEP)
        g_ref[...] = g
        d_ref[...] = -ADAM_LR * (m_hat / (jnp.sqrt(v_hat) + ADAM_EPS) + ADAM_WD * w_ref[...])
        nm_ref[...] = nm
        nv_ref[...] = nv

    tile = pl.BlockSpec((tr, c), lambda i: (i, 0))
    out = jax.ShapeDtypeStruct((r, c), F32)
    return pl.pallas_call(
        body, grid=(r // tr,),
        in_specs=[pl.BlockSpec((N_DEV, tr, c), lambda i: (0, i, 0)), tile, tile, tile],
        out_specs=(tile, tile, tile, tile), out_shape=(out, out, out, out),
        compiler_params=_cparams(("parallel",)), name=name)(parts, w, m, v)


def _tile_rows(r):
    for t in (512, 256, 128, 64, 32, 16, 8):
        if r % t == 0:
            return t
    return r


def _peer(k):
    x, y, c = (lax.axis_index(a) for a in MESH_AXES)
    px = 1 - x if k & 4 else x
    py = 1 - y if k & 2 else y
    pc = 1 - c if k & 1 else c
    return (px, py, pc), 4 * px + 2 * py + pc


def _exchange(src_of, dst_of, local_sem, send_sems, recv_sems):
    x, y, c = (lax.axis_index(a) for a in MESH_AXES)
    me = 4 * x + 2 * y + c
    mine = pltpu.make_async_copy(src_of(me), dst_of(me), local_sem)
    mine.start()
    sends = []
    for k in range(1, N_DEV):
        dev, idx = _peer(k)
        cp = pltpu.make_async_remote_copy(src_ref=src_of(idx), dst_ref=dst_of(me), send_sem=send_sems.at[k - 1],
                                          recv_sem=recv_sems.at[k - 1], device_id=dev,
                                          device_id_type=pl.DeviceIdType.MESH)
        cp.start()
        sends.append(cp)
    for k in range(1, N_DEV):
        dev, idx = _peer(k)
        pltpu.make_async_remote_copy(src_ref=src_of(idx), dst_ref=dst_of(idx), send_sem=send_sems.at[k - 1],
                                     recv_sem=recv_sems.at[k - 1], device_id=dev,
                                     device_id_type=pl.DeviceIdType.MESH).wait_recv()
    for cp in sends:
        cp.wait_send()
    mine.wait()


def _comm_call(body, out_shape, space, name, *args):
    spec = pl.BlockSpec(memory_space=space)
    return pl.pallas_call(
        body, in_specs=[spec] * len(args), out_specs=spec, out_shape=out_shape,
        scratch_shapes=[pltpu.SemaphoreType.DMA, pltpu.SemaphoreType.DMA((N_DEV - 1,)),
                        pltpu.SemaphoreType.DMA((N_DEV - 1,))],
        compiler_params=pltpu.CompilerParams(vmem_limit_bytes=VMEM_LIMIT_MIB * 2 ** 20), name=name)(*args)


def _all_gather(shard, *, name, space=pl.ANY):
    def body(x_ref, out_ref, local_sem, send_sems, recv_sems):
        _exchange(lambda d: x_ref, lambda d: out_ref.at[d], local_sem, send_sems, recv_sems)

    return _comm_call(body, jax.ShapeDtypeStruct((N_DEV,) + shard.shape, shard.dtype), space, name, shard)


def _all_to_all(slabs, *, name):
    def body(x_ref, out_ref, local_sem, send_sems, recv_sems):
        _exchange(lambda d: x_ref.at[d], lambda d: out_ref.at[d], local_sem, send_sems, recv_sems)

    return _comm_call(body, jax.ShapeDtypeStruct(slabs.shape, slabs.dtype), pl.ANY, name, slabs)


def _cols_from_dev(g, depth):
    rows = g.shape[1] // depth
    return g.reshape(N_DEV, depth, rows, g.shape[2]).transpose(1, 2, 0, 3).reshape(depth, rows, -1)


def _rows_from_dev(g, depth):
    r = g.shape[1] // depth
    return g.reshape(N_DEV, depth, r, g.shape[2]).transpose(1, 0, 2, 3).reshape(depth, N_DEV * r, g.shape[2])


def _cols_to_dev(g):
    depth, rows, n = g.shape
    return g.reshape(depth, rows, N_DEV, n // N_DEV).transpose(2, 0, 1, 3).reshape(N_DEV, depth * rows, n // N_DEV)


def _rows_to_dev(g):
    depth, rows, n = g.shape
    return g.reshape(depth, N_DEV, rows // N_DEV, n).transpose(1, 0, 2, 3).reshape(N_DEV, -1, n)


def _pack(arrays):
    flat = jnp.concatenate([a.reshape(-1) for a in arrays])
    pad = (-flat.shape[0]) % (8 * LANES)
    return jnp.pad(flat, (0, pad)).reshape(-1, LANES)


def _unpack(buf, shapes):
    flat = buf.reshape(-1)
    out, pos = [], 0
    for shp in shapes:
        n = int(np.prod(shp))
        out.append(flat[pos:pos + n].reshape(shp))
        pos += n
    return out


def _fold_heads(part8):
    t = part8.sum(axis=(0, 1))
    return t[:HEAD_DIM] + t[HEAD_DIM:]


def _forward_backward(x, target, wts):
    depth = wts["w_in"].shape[0]
    row = lambda a: a.reshape(1, -1)
    saved = []
    for l in range(depth):
        t = f"l{l}_"
        h = _norm_fwd([x], row(wts["attn_norm_w"][l]), group=D_MODEL, name=t + "attn_norm")
        proj = _mm([(h, wts["w_in"][l])], nt=False, name=t + "proj")
        qw2 = row(jnp.tile(wts["q_norm_w"][l], 2))
        kw2 = row(jnp.tile(wts["k_norm_w"][l], 2))
        bias = _bias_expand(wts["rel_bias"][l], name=t + "bias")
        w8 = jnp.pad(wts["conv_w"][l], ((0, 5), (0, 0)))
        sb = [_sb_fwd(proj, hp, name=t + f"sb{hp}") for hp in range(D_SB // LANES)]
        ys = [o[0] for o in sb]
        sb_carries = [o[1:] for o in sb]
        ys += [_ch_fwd(proj, qw2, kw2, bias, hp, name=t + f"ch{hp}") for hp in range(D_CH // LANES)]
        ys.append(_conv_fwd(proj, w8, name=t + "conv"))
        yn = _norm_fwd(ys, row(wts["out_norm_w"][l]), group=HEAD_DIM, name=t + "out_norm")
        x_mid = _mm([(yn, wts["w_out"][l])], nt=False, res=x, out_dtype=F32, name=t + "out_proj")
        h2 = _norm_fwd([x_mid], row(wts["ffn_norm_w"][l]), group=D_MODEL, name=t + "ffn_norm")
        g, u, a = _ffn_up(h2, wts["w_gate"][l], wts["w_up"][l], name=t + "ffn_up")
        x_out = _mm([(a, wts["w_down"][l])], nt=False, res=x_mid, out_dtype=F32, name=t + "ffn_down")
        saved.append(dict(x=x, h=h, proj=proj, sb_carries=sb_carries, qw2=qw2, kw2=kw2, bias=bias, w8=w8, ys=ys, yn=yn, x_mid=x_mid,
                          h2=h2, g=g, u=u, a=a))
        x = x_out

    dx, sq = _loss_head(x, target, name="loss_head")
    grads = {k: [None] * depth for k in
             ("attn_norm_w", "w_in", "q_norm_w", "k_norm_w", "rel_bias", "conv_w", "out_norm_w", "w_out",
              "ffn_norm_w", "w_gate", "w_up", "w_down")}
    for l in reversed(range(depth)):
        t = f"l{l}_b_"
        sv = saved[l]
        grads["w_down"][l] = _mm_tn(sv["a"], dx, name=t + "w_down")
        dg, du = _ffn_dact(dx, wts["w_down"][l], sv["g"], sv["u"], name=t + "ffn_dact")
        grads["w_gate"][l] = _mm_tn(sv["h2"], dg, name=t + "w_gate")
        grads["w_up"][l] = _mm_tn(sv["h2"], du, name=t + "w_up")
        dh2 = _mm([(dg, wts["w_gate"][l]), (du, wts["w_up"][l])], nt=True, name=t + "ffn_dh")
        dx_mid, dw8 = _norm_bwd([sv["x_mid"]], row(wts["ffn_norm_w"][l]), dh2, group=D_MODEL, res=dx,
                                name=t + "ffn_norm")
        grads["ffn_norm_w"][l] = dw8.sum(axis=0)
        grads["w_out"][l] = _mm_tn(sv["yn"], dx_mid, name=t + "w_out")
        dyn = _mm([(dx_mid, wts["w_out"][l])], nt=True, name=t + "out_dy")
        dy, dw8 = _norm_bwd(sv["ys"], row(wts["out_norm_w"][l]), dyn, group=HEAD_DIM, out_dtype=BF16,
                            name=t + "out_norm")
        grads["out_norm_w"][l] = dw8.sum(axis=0)
        proj = sv["proj"]
        n_sb, n_ch = D_SB // LANES, D_CH // LANES
        sb = [_sb_bwd(proj, sv["sb_carries"][hp], dy, hp, name=t + f"sb{hp}") for hp in range(n_sb)]
        ch = [_ch_bwd(proj, dy, sv["qw2"], sv["kw2"], sv["bias"], hp, name=t + f"ch{hp}") for hp in range(n_ch)]
        dgb, dgc, dxc, dcw = _conv_bwd(proj, dy, sv["w8"], name=t + "conv")
        grads["conv_w"][l] = dcw[:3]
        grads["q_norm_w"][l] = _fold_heads(jnp.stack([c[4] for c in ch]))
        grads["k_norm_w"][l] = _fold_heads(jnp.stack([c[5] for c in ch]))
        grads["rel_bias"][l] = _bias_fold(jnp.concatenate([c[3] for c in ch], axis=0), name=t + "bias")
        pieces = ([o[0] for o in sb] + [o[1] for o in sb] + [o[2] for o in sb]
                  + [o[0] for o in ch] + [o[1] for o in ch] + [o[2] for o in ch] + [dgb, dgc, dxc])
        dproj = jnp.concatenate([p.astype(BF16) for p in pieces], axis=1)
        grads["w_in"][l] = _mm_tn(sv["h"], dproj, name=t + "w_in")
        dh = _mm([(dproj, wts["w_in"][l])], nt=True, name=t + "proj_dh")
        dx, dw8 = _norm_bwd([sv["x"]], row(wts["attn_norm_w"][l]), dh, group=D_MODEL, res=dx_mid,
                            name=t + "attn_norm")
        grads["attn_norm_w"][l] = dw8.sum(axis=0)
    return sq, dx, {k: jnp.stack(v) for k, v in grads.items()}


SMALL = ("attn_norm_w", "q_norm_w", "k_norm_w", "rel_bias", "conv_w", "out_norm_w", "ffn_norm_w")
ORDER = ("attn_norm_w", "w_in", "q_norm_w", "k_norm_w", "rel_bias", "conv_w", "out_norm_w", "w_out",
         "ffn_norm_w", "w_gate", "w_up", "w_down")


def kernel(x, attn_norm_w, w_in, q_norm_w, k_norm_w, rel_bias, conv_w, out_norm_w, w_out, ffn_norm_w, w_gate, w_up, w_down, loss_target, m_attn_norm_w, m_w_in, m_q_norm_w, m_k_norm_w, m_rel_bias, m_conv_w, m_out_norm_w, m_w_out, m_ffn_norm_w, m_w_gate, m_w_up, m_w_down, v_attn_norm_w, v_w_in, v_q_norm_w, v_k_norm_w, v_rel_bias, v_conv_w, v_out_norm_w, v_w_out, v_ffn_norm_w, v_w_gate, v_w_up, v_w_down):
    w = dict(attn_norm_w=attn_norm_w, w_in=w_in, q_norm_w=q_norm_w, k_norm_w=k_norm_w, rel_bias=rel_bias,
             conv_w=conv_w, out_norm_w=out_norm_w, w_out=w_out, ffn_norm_w=ffn_norm_w, w_gate=w_gate, w_up=w_up,
             w_down=w_down)
    mom = dict(attn_norm_w=m_attn_norm_w, w_in=m_w_in, q_norm_w=m_q_norm_w, k_norm_w=m_k_norm_w, rel_bias=m_rel_bias,
               conv_w=m_conv_w, out_norm_w=m_out_norm_w, w_out=m_w_out, ffn_norm_w=m_ffn_norm_w, w_gate=m_w_gate,
               w_up=m_w_up, w_down=m_w_down)
    var = dict(attn_norm_w=v_attn_norm_w, w_in=v_w_in, q_norm_w=v_q_norm_w, k_norm_w=v_k_norm_w, rel_bias=v_rel_bias,
               conv_w=v_conv_w, out_norm_w=v_out_norm_w, w_out=v_w_out, ffn_norm_w=v_ffn_norm_w, w_gate=v_w_gate,
               w_up=v_w_up, w_down=v_w_down)
    depth = w_in.shape[0]
    seq = x.shape[1]
    me = 4 * lax.axis_index("x") + 2 * lax.axis_index("y") + lax.axis_index("c")
    flat2 = lambda a: a.reshape(-1, a.shape[-1])

    full = {k: w[k] for k in SMALL if k != "conv_w"}
    for k in ("w_in", "w_gate", "w_up"):
        full[k] = _cols_from_dev(_all_gather(flat2(w[k]).astype(BF16), name="gather_" + k), depth)
    for k in ("w_out", "w_down"):
        full[k] = _rows_from_dev(_all_gather(flat2(w[k]).astype(BF16), name="gather_" + k), depth)
    conv_shard = conv_w.shape[2]
    conv_all = _all_gather(_pack([conv_w]), name="gather_conv_w", space=pltpu.VMEM)
    conv_all = conv_all.reshape(N_DEV, -1)[:, :depth * 3 * conv_shard].reshape(N_DEV, depth, 3, conv_shard)
    full["conv_w"] = conv_all.transpose(1, 2, 0, 3).reshape(depth, 3, N_DEV * conv_shard)

    sq, dx, grads = _forward_backward(x.reshape(seq, D_MODEL), loss_target.reshape(seq, D_MODEL), full)
    loss = lax.psum(0.5 / D_MODEL * jnp.sum(sq), MESH_AXES)

    outs = {}
    for k in ("w_in", "w_gate", "w_up", "w_out", "w_down"):
        slabs = _cols_to_dev(grads[k]) if k in ("w_in", "w_gate", "w_up") else _rows_to_dev(grads[k])
        parts = _all_to_all(slabs.astype(BF16), name="scatter_" + k)
        res = _adamw(parts, flat2(w[k]), flat2(mom[k]), flat2(var[k]), name="adamw_" + k)
        outs[k] = [r.reshape(w[k].shape) for r in res]

    shapes = [grads[k].shape for k in SMALL]
    col0 = me * conv_shard

    def widen(a):
        return lax.dynamic_update_slice(jnp.zeros((depth, 3, N_DEV * conv_shard), F32), a, (0, 0, col0))

    parts = _all_gather(_pack([grads[k] for k in SMALL]), name="gather_small_grads", space=pltpu.VMEM)
    pw = _pack([widen(w[k]) if k == "conv_w" else w[k] for k in SMALL])
    pm = _pack([widen(mom[k]) if k == "conv_w" else mom[k] for k in SMALL])
    pv = _pack([widen(var[k]) if k == "conv_w" else var[k] for k in SMALL])
    res = [_unpack(r, shapes) for r in _adamw(parts, pw, pm, pv, name="adamw_small")]
    for idx, k in enumerate(SMALL):
        vals = [r[idx] for r in res]
        if k == "conv_w":
            vals = [lax.dynamic_slice(a, (0, 0, col0), (depth, 3, conv_shard)) for a in vals]
        outs[k] = vals

    result = [loss, dx.reshape(x.shape)]
    for part in range(4):
        result += [outs[k][part] for k in ORDER]
    return tuple(result)
```

```python
import numpy as np
import jax
import jax.numpy as jnp
from jax import lax
from jax.experimental import pallas as pl
from jax.experimental.pallas import tpu as pltpu

F32 = jnp.float32
BF16 = jnp.bfloat16

N_DEV = 8
D_MODEL = 1024
HEAD_DIM = 64
LANES = 128
SB_BLOCK = 128
SB_GROUP = 8
SB_HALF = SB_GROUP // 2
SB_LANE_LOWER, SB_LANE_UPPER = 127, 126
CH_BLOCK = 128
CH_PAD = 512
CH_BAND = 640
REL_CLIP = 128
N_REL = 2 * REL_CLIP + 1
REL_PAD = 384
D_SB, D_CH, D_CV = 256, 512, 256
COL_QA, COL_KA, COL_VA = 0, 256, 512
COL_QB, COL_KB, COL_VB = 768, 1280, 1792
COL_GB, COL_GC, COL_XC = 2304, 2560, 2816
EPS = 1e-6
NEG = -1e30
SCALE = HEAD_DIM ** -0.5

ADAM_LR, ADAM_B1, ADAM_B2, ADAM_EPS, ADAM_WD, ADAM_STEP = 0.001, 0.9, 0.999, 1e-08, 0.01, 10

VMEM_LIMIT_MIB = 56
MM_WEIGHT_TILE_BYTES = 8 * 2 ** 20
MESH_AXES = ("x", "y", "c")


def _cparams(sem=None):
    return pltpu.CompilerParams(dimension_semantics=sem, vmem_limit_bytes=VMEM_LIMIT_MIB * 2 ** 20)


def _dot(a, b):
    return lax.dot_general(a, b, (((1,), (0,)), ((), ())), preferred_element_type=F32)


def _dot_nt(a, b):
    return lax.dot_general(a, b, (((1,), (1,)), ((), ())), preferred_element_type=F32)


def _dot_tn(a, b):
    return lax.dot_general(a, b, (((0,), (0,)), ((), ())), preferred_element_type=F32)


def _tile(n, cap):
    if n <= cap:
        return n
    best = None
    for t in range(128, cap + 1, 128):
        if n % t == 0:
            best = t
    assert best is not None, (n, cap)
    return best


def _cat(vals):
    return vals[0] if len(vals) == 1 else jnp.concatenate(vals, axis=1)


def _mm(pairs, *, nt, name, res=None, out_dtype=BF16):
    m = pairs[0][0].shape[0]
    n = pairs[0][1].shape[0] if nt else pairs[0][1].shape[1]
    tm = min(512, m)
    tn = n
    while sum(a.shape[1] for a, _ in pairs) * tn * 2 > MM_WEIGHT_TILE_BYTES and tn % 256 == 0:
        tn //= 2
    npairs = len(pairs)

    def body(*refs):
        acc = None
        for p in range(npairs):
            a = refs[2 * p][...].astype(BF16)
            b = refs[2 * p + 1][...]
            d = _dot_nt(a, b) if nt else _dot(a, b)
            acc = d if acc is None else acc + d
        if res is not None:
            acc = acc + refs[2 * npairs][...]
        refs[-1][...] = acc.astype(refs[-1].dtype)

    in_specs, args = [], []
    for a, b in pairs:
        k = a.shape[1]
        in_specs.append(pl.BlockSpec((tm, k), lambda j, i: (i, 0)))
        if nt:
            in_specs.append(pl.BlockSpec((tn, k), lambda j, i: (j, 0)))
        else:
            in_specs.append(pl.BlockSpec((k, tn), lambda j, i: (0, j)))
        args += [a, b]
    if res is not None:
        in_specs.append(pl.BlockSpec((tm, tn), lambda j, i: (i, j)))
        args.append(res)
    return pl.pallas_call(
        body, grid=(n // tn, m // tm), in_specs=in_specs,
        out_specs=pl.BlockSpec((tm, tn), lambda j, i: (i, j)),
        out_shape=jax.ShapeDtypeStruct((m, n), out_dtype),
        compiler_params=_cparams(("parallel", "parallel")), name=name)(*args)


def _mm_tn(a, g, *, name):
    s, k = a.shape
    n = g.shape[1]
    tk, tn, ts = _tile(k, 1408), _tile(n, 1536), min(512, s)

    def body(a_ref, g_ref, o_ref):
        @pl.when(pl.program_id(2) == 0)
        def _():
            o_ref[...] = jnp.zeros_like(o_ref)
        o_ref[...] += _dot_tn(a_ref[...].astype(BF16), g_ref[...].astype(BF16))

    return pl.pallas_call(
        body, grid=(k // tk, n // tn, s // ts),
        in_specs=[pl.BlockSpec((ts, tk), lambda i, j, t: (t, i)),
                  pl.BlockSpec((ts, tn), lambda i, j, t: (t, j))],
        out_specs=pl.BlockSpec((tk, tn), lambda i, j, t: (i, j)),
        out_shape=jax.ShapeDtypeStruct((k, n), F32),
        compiler_params=_cparams(("parallel", "parallel", "arbitrary")), name=name)(a, g)


def _group_indicator(width, group):
    assert width // group <= LANES
    r = lax.broadcasted_iota(jnp.int32, (width, LANES), 0) // group
    c = lax.broadcasted_iota(jnp.int32, (width, LANES), 1)
    return (r == c).astype(BF16)


def _split_bf16(v):
    hi = v.astype(BF16)
    return hi, (v - hi.astype(F32)).astype(BF16)


def _group_mean(v, ind, group):
    hi, lo = _split_bf16(v)
    hi, lo = _split_bf16((_dot(hi, ind) + _dot(lo, ind)) * (1.0 / group))
    return _dot_nt(hi, ind) + _dot_nt(lo, ind)


def _norm_fwd(xs, w, *, group, name):
    s = xs[0].shape[0]
    width = sum(x.shape[1] for x in xs)
    tm = min(512, s)
    nx = len(xs)
    grouped = group != width

    def body(*refs):
        x = _cat([r[...].astype(F32) for r in refs[:nx]])
        if grouped:
            ms = _group_mean(x * x, refs[nx + 1][...], group)
        else:
            ms = jnp.mean(x * x, axis=1, keepdims=True)
        refs[-1][...] = (x * lax.rsqrt(ms + EPS) * refs[nx][...]).astype(BF16)

    in_specs = [pl.BlockSpec((tm, x.shape[1]), lambda i: (i, 0)) for x in xs]
    in_specs.append(pl.BlockSpec((1, width), lambda i: (0, 0)))
    args = list(xs) + [w]
    if grouped:
        in_specs.append(pl.BlockSpec((width, LANES), lambda i: (0, 0)))
        args.append(_group_indicator(width, group))
    return pl.pallas_call(
        body, grid=(s // tm,), in_specs=in_specs,
        out_specs=pl.BlockSpec((tm, width), lambda i: (i, 0)),
        out_shape=jax.ShapeDtypeStruct((s, width), BF16),
        compiler_params=_cparams(("parallel",)), name=name)(*args)


def _norm_bwd(xs, w, dy, *, group, name, res=None, out_dtype=F32):
    s = xs[0].shape[0]
    width = sum(x.shape[1] for x in xs)
    tm = min(512, s)
    nx = len(xs)
    grouped = group != width
    has_res = res is not None

    def body(*refs):
        x = _cat([r[...].astype(F32) for r in refs[:nx]])
        w_ref, dy_ref = refs[nx], refs[nx + 1]
        pos = nx + 2
        if has_res:
            res_ref = refs[pos]
            pos += 1
        if grouped:
            ind = refs[pos][...]
            mean = lambda v: _group_mean(v, ind, group)
        else:
            mean = lambda v: jnp.mean(v, axis=1, keepdims=True)
        dx_ref, dw_ref = refs[-2], refs[-1]
        inv = lax.rsqrt(mean(x * x) + EPS)
        xh = x * inv
        d = dy_ref[...].astype(F32)
        g = d * w_ref[...]
        dx = inv * (g - xh * mean(g * xh))
        if has_res:
            dx = dx + res_ref[...]
        dx_ref[...] = dx.astype(dx_ref.dtype)

        @pl.when(pl.program_id(0) == 0)
        def _():
            dw_ref[...] = jnp.zeros_like(dw_ref)
        dw_ref[...] += (d * xh).reshape(tm // 8, 8, width).sum(axis=0)

    in_specs = [pl.BlockSpec((tm, x.shape[1]), lambda i: (i, 0)) for x in xs]
    in_specs += [pl.BlockSpec((1, width), lambda i: (0, 0)), pl.BlockSpec((tm, width), lambda i: (i, 0))]
    args = list(xs) + [w, dy]
    if has_res:
        in_specs.append(pl.BlockSpec((tm, width), lambda i: (i, 0)))
        args.append(res)
    if grouped:
        in_specs.append(pl.BlockSpec((width, LANES), lambda i: (0, 0)))
        args.append(_group_indicator(width, group))
    return pl.pallas_call(
        body, grid=(s // tm,), in_specs=in_specs,
        out_specs=(pl.BlockSpec((tm, width), lambda i: (i, 0)), pl.BlockSpec((8, width), lambda i: (0, 0))),
        out_shape=(jax.ShapeDtypeStruct((s, width), out_dtype), jax.ShapeDtypeStruct((8, width), F32)),
        compiler_params=_cparams(("arbitrary",)), name=name)(*args)


def _head0_lanes(shape):
    return lax.broadcasted_iota(jnp.int32, shape, len(shape) - 1) < HEAD_DIM


def _split_heads(v):
    m0 = _head0_lanes(v.shape)
    zero = jnp.zeros_like(v)
    return jnp.where(m0, v, zero), jnp.where(m0, zero, v)


def _pair_sum(v):
    m0 = _head0_lanes(v.shape)
    s0 = jnp.sum(jnp.where(m0, v, 0.0), axis=1, keepdims=True)
    s1 = jnp.sum(jnp.where(m0, 0.0, v), axis=1, keepdims=True)
    return jnp.where(m0, s0, s1)


def _pair_norm(x, w2):
    inv = lax.rsqrt(_pair_sum(x * x) * (1.0 / HEAD_DIM) + EPS)
    xh = x * inv
    return xh * w2, xh, inv


def _pair_norm_bwd(dy, xh, inv, w2):
    g = dy * w2
    return inv * (g - xh * (_pair_sum(g * xh) * (1.0 / HEAD_DIM)))


def _copy_in(copies):
    for c in copies:
        c.start()
    for c in copies:
        c.wait()


def _stack(vals):
    return jnp.concatenate(list(vals), axis=0)


def _unstack(results, chains):
    blocks = []
    for res in results:
        blocks += [res[n * SB_BLOCK:(n + 1) * SB_BLOCK] for n in range(res.shape[0] // SB_BLOCK)]
    return dict(zip(chains, blocks))


def _log_rest(nz):
    return jnp.minimum(nz, 0.0) - jnp.log(1.0 + jnp.exp(-jnp.abs(nz)))


def _sb_fwd(proj, hp, *, name):
    s = proj.shape[0]
    nq = s // SB_BLOCK
    assert nq % SB_GROUP == 0 and nq // SB_GROUP <= SB_LANE_UPPER
    shape = (SB_BLOCK, SB_BLOCK)

    def body(q_ref, proj_hbm, y_ref, c0_ref, c1_ref, k_scr, v_scr, sems):
        i = pl.program_id(0)

        @pl.when(i == 0)
        def _():
            _copy_in([pltpu.make_async_copy(proj_hbm.at[:, pl.ds(COL_KA + hp * LANES, LANES)], k_scr, sems.at[0]),
                      pltpu.make_async_copy(proj_hbm.at[:, pl.ds(COL_VA + hp * LANES, LANES)], v_scr, sems.at[1])])

        row = lax.broadcasted_iota(jnp.int32, shape, 0)
        col = lax.broadcasted_iota(jnp.int32, shape, 1)
        m0 = col < HEAD_DIM
        later_keys = (row > col).astype(BF16)
        q_neg = _stack(_split_heads(q_ref[...] * (-SCALE)))

        def blocks_of(first, n, carry, masked, lane):
            c, acc, seen = list(carry[:2]), carry[2], list(carry[3:5])
            chains = [(b, h) for b in range(n) for h in range(2)]
            js = [first + b for b in reversed(range(n))]
            rows = [pl.ds(pl.multiple_of(j * SB_BLOCK, SB_BLOCK), SB_BLOCK) for j in js]
            earlier = [(col + j * SB_BLOCK < row + i * SB_BLOCK) if masked else None for j in js]
            nz = _unstack([_dot_nt(q_neg, k_scr[r, :]) for r in rows], chains)
            lr = {ch: _log_rest(nz[ch]) for ch in chains}
            lrm = {(b, h): jnp.where(earlier[b], lr[b, h], 0.0) if masked else lr[b, h] for b, h in chains}
            after = _unstack([_dot(_stack([lrm[ch].astype(BF16) for ch in chains]), later_keys)], chains)
            seen = [jnp.where(col == lane, c[h], seen[h]) for h in range(2)]
            w = {}
            for b, h in chains:
                wv = jnp.exp((lr[b, h] - nz[b, h]) + (after[b, h] + c[h]))
                w[b, h] = jnp.where(earlier[b], wv, 0.0) if masked else wv
                c[h] = c[h] + jnp.sum(lrm[b, h], axis=1, keepdims=True)
            pv = _unstack([_dot(_stack([w[b, 0].astype(BF16), w[b, 1].astype(BF16)]), v_scr[rows[b], :])
                           for b in range(n)], chains)
            for b in range(n):
                acc = acc + jnp.where(m0, pv[b, 0], pv[b, 1])
            return c[0], c[1], acc, seen[0], seen[1]

        zc = jnp.zeros((SB_BLOCK, 1), F32)
        zt = jnp.zeros(shape, F32)
        gi = i // SB_GROUP
        carry = lax.cond(i % SB_GROUP >= SB_HALF,
                         lambda cr: blocks_of(gi * SB_GROUP + SB_HALF, SB_HALF, cr, True, SB_LANE_UPPER),
                         lambda cr: cr, (zc, zc, zt, zt, zt))
        carry = blocks_of(gi * SB_GROUP, SB_HALF, carry, True, SB_LANE_LOWER)
        carry = lax.fori_loop(
            0, gi, lambda t, cr: blocks_of((gi - 1 - t) * SB_GROUP, SB_GROUP, cr, False, gi - 1 - t), carry)
        y_ref[...] = carry[2].astype(y_ref.dtype)
        c0_ref[...] = carry[3]
        c1_ref[...] = carry[4]

    return pl.pallas_call(
        body, grid=(nq,),
        in_specs=[pl.BlockSpec(shape, lambda i: (i, COL_QA // LANES + hp)), pl.BlockSpec(memory_space=pl.ANY)],
        out_specs=(pl.BlockSpec(shape, lambda i: (i, 0)),) * 3,
        out_shape=(jax.ShapeDtypeStruct((s, LANES), BF16), jax.ShapeDtypeStruct((s, LANES), F32),
                   jax.ShapeDtypeStruct((s, LANES), F32)),
        scratch_shapes=[pltpu.VMEM((s, LANES), BF16), pltpu.VMEM((s, LANES), BF16), pltpu.SemaphoreType.DMA((2,))],
        compiler_params=_cparams(("arbitrary",)), name=name)(proj, proj)


def _sb_bwd(proj, carries, dy_all, hp, *, name):
    s = proj.shape[0]
    nq = s // SB_BLOCK
    shape = (SB_BLOCK, SB_BLOCK)

    def body(q_ref, c0_ref, c1_ref, dy_ref, proj_hbm, dq_ref, dk_hbm, dv_hbm, k_scr, v_scr, dk_scr, dv_scr, sems):
        i = pl.program_id(0)

        @pl.when(i == 0)
        def _():
            _copy_in([pltpu.make_async_copy(proj_hbm.at[:, pl.ds(COL_KA + hp * LANES, LANES)], k_scr, sems.at[0]),
                      pltpu.make_async_copy(proj_hbm.at[:, pl.ds(COL_VA + hp * LANES, LANES)], v_scr, sems.at[1])])
            dk_scr[...] = jnp.zeros_like(dk_scr)
            dv_scr[...] = jnp.zeros_like(dv_scr)

        row = lax.broadcasted_iota(jnp.int32, shape, 0)
        col = lax.broadcasted_iota(jnp.int32, shape, 1)
        m0 = col < HEAD_DIM
        later_keys = (row > col).astype(BF16)
        earlier_keys = (row < col).astype(BF16)
        q_both = _stack(_split_heads(q_ref[...] * SCALE))
        q_neg = _stack(_split_heads(q_ref[...] * (-SCALE)))
        dy_both = _stack(_split_heads(dy_ref[...]))
        seen = (c0_ref[...], c1_ref[...])

        def blocks_of(first, n, carry, masked, lane):
            r, dq = list(carry[0:2]), list(carry[2:4])
            blocks = range(n)
            chains = [(b, h) for b in blocks for h in range(2)]
            js = [first + b for b in blocks]
            rows = [pl.ds(pl.multiple_of(j * SB_BLOCK, SB_BLOCK), SB_BLOCK) for j in js]
            earlier = [(col + j * SB_BLOCK < row + i * SB_BLOCK) if masked else None for j in js]
            u = _unstack([_dot_nt(q_neg, k_scr[rw, :]) for rw in rows], chains)
            dw = _unstack([_dot_nt(dy_both, v_scr[rw, :]) for rw in rows], chains)
            lr = {ch: _log_rest(u[ch]) for ch in chains}
            lrm = {(b, h): jnp.where(earlier[b], lr[b, h], 0.0) if masked else lr[b, h] for b, h in chains}
            after = _unstack([_dot(_stack([lrm[ch].astype(BF16) for ch in chains]), later_keys)], chains)
            c = {}
            for h in range(2):
                ch = jnp.sum(jnp.where(col == lane, seen[h], 0.0), axis=1, keepdims=True)
                for b in reversed(blocks):
                    c[b, h] = ch
                    ch = ch + jnp.sum(lrm[b, h], axis=1, keepdims=True)
            w, de, lb = {}, {}, {}
            for b, h in chains:
                lb[b, h] = lr[b, h] - u[b, h]
                wv = jnp.exp(lb[b, h] + (after[b, h] + c[b, h]))
                w[b, h] = jnp.where(earlier[b], wv, 0.0) if masked else wv
                de[b, h] = dw[b, h] * w[b, h]
            before = _unstack([_dot(_stack([de[ch].astype(BF16) for ch in chains]), earlier_keys)], chains)
            dzb = {}
            for b, h in chains:
                dz = de[b, h] - jnp.exp(lb[b, h]) * (de[b, h] + (r[h] + before[b, h]))
                if masked:
                    dz = jnp.where(earlier[b], dz, 0.0)
                dzb[b, h] = dz.astype(BF16)
                r[h] = r[h] + jnp.sum(de[b, h], axis=1, keepdims=True)
            keys = pl.ds(pl.multiple_of(first * SB_BLOCK, SB_BLOCK), n * SB_BLOCK)
            dz_wide = [jnp.concatenate([dzb[b, h] for b in blocks], axis=1) for h in range(2)]
            w_wide = [jnp.concatenate([w[b, h].astype(BF16) for b in blocks], axis=1) for h in range(2)]
            dqc = _dot(_stack(dz_wide), k_scr[keys, :])
            dq = [dq[0] + dqc[:SB_BLOCK], dq[1] + dqc[SB_BLOCK:]]
            dk_scr[keys, :] += _dot_tn(_stack(dz_wide), q_both)
            dv_scr[keys, :] += _dot_tn(_stack(w_wide), dy_both)
            return r[0], r[1], dq[0], dq[1]

        zc = jnp.zeros((SB_BLOCK, 1), F32)
        zq = jnp.zeros(shape, F32)
        gi = i // SB_GROUP
        carry = lax.fori_loop(0, gi, lambda t, cr: blocks_of(t * SB_GROUP, SB_GROUP, cr, False, t), (zc, zc, zq, zq))
        carry = blocks_of(gi * SB_GROUP, SB_HALF, carry, True, SB_LANE_LOWER)
        carry = lax.cond(i % SB_GROUP >= SB_HALF,
                         lambda cr: blocks_of(gi * SB_GROUP + SB_HALF, SB_HALF, cr, True, SB_LANE_UPPER),
                         lambda cr: cr, carry)
        dq_ref[...] = (jnp.where(m0, carry[2], carry[3]) * SCALE).astype(dq_ref.dtype)

        @pl.when(i == nq - 1)
        def _():
            _copy_in([pltpu.make_async_copy(dk_scr, dk_hbm, sems.at[0]),
                      pltpu.make_async_copy(dv_scr, dv_hbm, sems.at[1])])

    blk = lambda c0: pl.BlockSpec(shape, lambda i: (i, c0))
    return pl.pallas_call(
        body, grid=(nq,),
        in_specs=[blk(COL_QA // LANES + hp), blk(0), blk(0), blk(hp), pl.BlockSpec(memory_space=pl.ANY)],
        out_specs=(blk(0), pl.BlockSpec(memory_space=pl.ANY), pl.BlockSpec(memory_space=pl.ANY)),
        out_shape=(jax.ShapeDtypeStruct((s, LANES), BF16), jax.ShapeDtypeStruct((s, LANES), F32),
                   jax.ShapeDtypeStruct((s, LANES), F32)),
        scratch_shapes=[pltpu.VMEM((s, LANES), BF16), pltpu.VMEM((s, LANES), BF16),
                        pltpu.VMEM((s, LANES), F32), pltpu.VMEM((s, LANES), F32), pltpu.SemaphoreType.DMA((2,))],
        compiler_params=_cparams(("arbitrary",)), name=name)(proj, carries[0], carries[1], dy_all, proj)


def _rel_index():
    p = np.arange(CH_BLOCK)[:, None]
    m = np.arange(CH_BAND)[None, :]
    return (np.clip(CH_PAD + p - m, -REL_CLIP, REL_CLIP) + REL_CLIP).astype(np.int32)


def _bias_expand(rel_bias, *, name):
    nh = rel_bias.shape[0]
    table = jnp.pad(rel_bias, ((0, 0), (0, REL_PAD - N_REL)))

    def body(tab_ref, rel_ref, out_ref):
        rid = lax.broadcasted_iota(jnp.int32, (REL_PAD, CH_BAND), 0)

        def step(p, _):
            onehot = (rid == rel_ref[pl.ds(p, 1), :]).astype(F32)
            out_ref[p] = lax.dot_general(tab_ref[...], onehot, (((1,), (0,)), ((), ())),
                                         precision=lax.Precision.HIGHEST, preferred_element_type=F32)
            return 0

        lax.fori_loop(0, CH_BLOCK, step, 0)

    out = pl.pallas_call(
        body, out_shape=jax.ShapeDtypeStruct((CH_BLOCK, nh, CH_BAND), F32),
        compiler_params=_cparams(), name=name)(table, jnp.asarray(_rel_index()))
    return out.transpose(1, 0, 2)


def _bias_fold(dbias, *, name):
    nh = dbias.shape[0]

    def body(db_ref, rel_ref, out_ref):
        rid = lax.broadcasted_iota(jnp.int32, (REL_PAD, CH_BAND), 0)

        def step(p, acc):
            onehot = (rid == rel_ref[pl.ds(p, 1), :]).astype(F32)
            return acc + lax.dot_general(db_ref[p], onehot, (((1,), (1,)), ((), ())),
                                         precision=lax.Precision.HIGHEST, preferred_element_type=F32)

        out_ref[...] = lax.fori_loop(0, CH_BLOCK, step, jnp.zeros((nh, REL_PAD), F32))

    out = pl.pallas_call(
        body, out_shape=jax.ShapeDtypeStruct((nh, REL_PAD), F32),
        compiler_params=_cparams(), name=name)(dbias.transpose(1, 0, 2), jnp.asarray(_rel_index()))
    return out[:, :N_REL]


def _band_valid(i):
    row = lax.broadcasted_iota(jnp.int32, (CH_BLOCK, CH_BAND), 0)
    col = lax.broadcasted_iota(jnp.int32, (CH_BLOCK, CH_BAND), 1)
    first = row < CH_BLOCK // 2
    window = (first & (col < CH_BAND - 64)) | (jnp.logical_not(first) & (col >= 64))
    return window & (col + i * CH_BLOCK >= CH_PAD)


def _ch_load(proj_hbm, hp, s, kw_ref, kraw_scr, kn_scr, vp_scr, sems):
    _copy_in([pltpu.make_async_copy(proj_hbm.at[:, pl.ds(COL_KB + hp * LANES, LANES)], kraw_scr, sems.at[0]),
              pltpu.make_async_copy(proj_hbm.at[:, pl.ds(COL_VB + hp * LANES, LANES)],
                                    vp_scr.at[pl.ds(CH_PAD, s), :], sems.at[1])])
    kn_scr[0:CH_PAD, :] = jnp.zeros((CH_PAD, LANES), BF16)
    vp_scr[0:CH_PAD, :] = jnp.zeros((CH_PAD, LANES), BF16)
    rb = min(512, s)

    def step(t, _):
        r = pl.multiple_of(t * rb, rb)
        kn = _pair_norm(kraw_scr[pl.ds(r, rb), :].astype(F32), kw_ref[...])[0]
        kn_scr[pl.ds(CH_PAD + r, rb), :] = kn.astype(BF16)
        return 0

    lax.fori_loop(0, s // rb, step, 0)


def _ch_probs(qk, bias, valid):
    sc = jnp.where(valid, qk * SCALE + bias, NEG)
    p = jnp.exp(sc - jnp.max(sc, axis=1, keepdims=True))
    return p, 1.0 / jnp.sum(p, axis=1, keepdims=True)


def _ch_fwd(proj, qw2, kw2, bias, hp, *, name):
    s = proj.shape[0]
    nb = s // CH_BLOCK
    shape = (CH_BLOCK, LANES)

    def body(q_ref, proj_hbm, qw_ref, kw_ref, bias_ref, y_ref, kraw_scr, kn_scr, vp_scr, sems):
        i = pl.program_id(0)

        @pl.when(i == 0)
        def _():
            _ch_load(proj_hbm, hp, s, kw_ref, kraw_scr, kn_scr, vp_scr, sems)

        band = pl.ds(pl.multiple_of(i * CH_BLOCK, CH_BLOCK), CH_BAND)
        kb, vb = kn_scr[band, :], vp_scr[band, :]
        valid = _band_valid(i)
        qn = _pair_norm(q_ref[...].astype(F32), qw_ref[...])[0]
        scores = _dot_nt(_stack(_split_heads(qn.astype(BF16))), kb)
        probs = [_ch_probs(scores[h * CH_BLOCK:(h + 1) * CH_BLOCK], bias_ref[h], valid) for h in range(2)]
        outs = _dot(_stack([p.astype(BF16) for p, _ in probs]), vb)
        outs = [outs[h * CH_BLOCK:(h + 1) * CH_BLOCK] * probs[h][1] for h in range(2)]
        y_ref[...] = jnp.where(_head0_lanes(shape), outs[0], outs[1]).astype(y_ref.dtype)

    return pl.pallas_call(
        body, grid=(nb,),
        in_specs=[pl.BlockSpec(shape, lambda i: (i, COL_QB // LANES + hp)), pl.BlockSpec(memory_space=pl.ANY),
                  pl.BlockSpec((1, LANES), lambda i: (0, 0)), pl.BlockSpec((1, LANES), lambda i: (0, 0)),
                  pl.BlockSpec((2, CH_BLOCK, CH_BAND), lambda i: (hp, 0, 0))],
        out_specs=pl.BlockSpec(shape, lambda i: (i, 0)),
        out_shape=jax.ShapeDtypeStruct((s, LANES), BF16),
        scratch_shapes=[pltpu.VMEM((s, LANES), BF16), pltpu.VMEM((s + CH_PAD, LANES), BF16),
                        pltpu.VMEM((s + CH_PAD, LANES), BF16), pltpu.SemaphoreType.DMA((2,))],
        compiler_params=_cparams(("arbitrary",)), name=name)(proj, proj, qw2, kw2, bias)


def _ch_bwd(proj, dy_all, qw2, kw2, bias, hp, *, name):
    s = proj.shape[0]
    nb = s // CH_BLOCK
    shape = (CH_BLOCK, LANES)
    rb = min(512, s)

    def body(q_ref, dy_ref, proj_hbm, qw_ref, kw_ref, bias_ref, dq_ref, dk_hbm, dv_hbm, dbias_ref, dqw_ref, dkw_ref,
             kraw_scr, kn_scr, vp_scr, dkn_scr, dvp_scr, sems):
        i = pl.program_id(0)

        @pl.when(i == 0)
        def _():
            _ch_load(proj_hbm, hp, s, kw_ref, kraw_scr, kn_scr, vp_scr, sems)
            dkn_scr[...] = jnp.zeros_like(dkn_scr)
            dvp_scr[...] = jnp.zeros_like(dvp_scr)
            dbias_ref[...] = jnp.zeros_like(dbias_ref)
            dqw_ref[...] = jnp.zeros_like(dqw_ref)

        band = pl.ds(pl.multiple_of(i * CH_BLOCK, CH_BLOCK), CH_BAND)
        kb, vb = kn_scr[band, :], vp_scr[band, :]
        valid = _band_valid(i)
        qn, qhat, qinv = _pair_norm(q_ref[...].astype(F32), qw_ref[...])
        q_both = _stack(_split_heads(qn.astype(BF16)))
        dy_both = _stack(_split_heads(dy_ref[...]))
        scores = _dot_nt(q_both, kb)
        dprobs = _dot_nt(dy_both, vb)
        pb, dsb = [], []
        for h in range(2):
            part = slice(h * CH_BLOCK, (h + 1) * CH_BLOCK)
            p, rl = _ch_probs(scores[part], bias_ref[h], valid)
            p = p * rl
            dp = dprobs[part]
            ds = p * (dp - jnp.sum(dp * p, axis=1, keepdims=True))
            dbias_ref[h] += ds
            pb.append(p.astype(BF16))
            dsb.append((ds * SCALE).astype(BF16))
        dsb = _stack(dsb)
        dqn = _dot(dsb, kb)
        dkn_scr[band, :] += _dot_tn(dsb, q_both)
        dvp_scr[band, :] += _dot_tn(_stack(pb), dy_both)
        dqn = jnp.where(_head0_lanes(shape), dqn[:CH_BLOCK], dqn[CH_BLOCK:])
        dq_ref[...] = _pair_norm_bwd(dqn, qhat, qinv, qw_ref[...]).astype(dq_ref.dtype)
        dqw_ref[...] += (dqn * qhat).reshape(CH_BLOCK // 8, 8, LANES).sum(axis=0)

        @pl.when(i == nb - 1)
        def _():
            def step(t, acc):
                r = pl.multiple_of(t * rb, rb)
                rows = pl.ds(CH_PAD + r, rb)
                _, khat, kinv = _pair_norm(kraw_scr[pl.ds(r, rb), :].astype(F32), kw_ref[...])
                dkn = dkn_scr[rows, :]
                dkn_scr[rows, :] = _pair_norm_bwd(dkn, khat, kinv, kw_ref[...])
                return acc + (dkn * khat).reshape(rb // 8, 8, LANES).sum(axis=0)

            dkw_ref[...] = lax.fori_loop(0, s // rb, step, jnp.zeros((8, LANES), F32))
            _copy_in([pltpu.make_async_copy(dkn_scr.at[pl.ds(CH_PAD, s), :], dk_hbm, sems.at[0]),
                      pltpu.make_async_copy(dvp_scr.at[pl.ds(CH_PAD, s), :], dv_hbm, sems.at[1])])

    blk = lambda c0: pl.BlockSpec(shape, lambda i: (i, c0))
    vec = pl.BlockSpec((1, LANES), lambda i: (0, 0))
    part = pl.BlockSpec((8, LANES), lambda i: (0, 0))
    hbm = pl.BlockSpec(memory_space=pl.ANY)
    return pl.pallas_call(
        body, grid=(nb,),
        in_specs=[blk(COL_QB // LANES + hp), blk(D_SB // LANES + hp), hbm, vec, vec,
                  pl.BlockSpec((2, CH_BLOCK, CH_BAND), lambda i: (hp, 0, 0))],
        out_specs=(blk(0), hbm, hbm, pl.BlockSpec((2, CH_BLOCK, CH_BAND), lambda i: (0, 0, 0)), part, part),
        out_shape=(jax.ShapeDtypeStruct((s, LANES), BF16), jax.ShapeDtypeStruct((s, LANES), F32),
                   jax.ShapeDtypeStruct((s, LANES), F32), jax.ShapeDtypeStruct((2, CH_BLOCK, CH_BAND), F32),
                   jax.ShapeDtypeStruct((8, LANES), F32), jax.ShapeDtypeStruct((8, LANES), F32)),
        scratch_shapes=[pltpu.VMEM((s, LANES), BF16), pltpu.VMEM((s + CH_PAD, LANES), BF16),
                        pltpu.VMEM((s + CH_PAD, LANES), BF16), pltpu.VMEM((s + CH_PAD, LANES), F32),
                        pltpu.VMEM((s + CH_PAD, LANES), F32), pltpu.SemaphoreType.DMA((2,))],
        compiler_params=_cparams(("arbitrary",)), name=name)(proj, dy_all, proj, qw2, kw2, bias)


def _shift_down(x, halo, k):
    out = pltpu.roll(x, k, 0)
    row = lax.broadcasted_iota(jnp.int32, x.shape, 0)
    for t in range(k):
        out = jnp.where(row == t, halo[8 - k + t:8 - k + t + 1, :], out)
    return out


def _shift_up(x, halo, k):
    n = x.shape[0]
    out = pltpu.roll(x, n - k, 0)
    row = lax.broadcasted_iota(jnp.int32, x.shape, 0)
    for t in range(k):
        out = jnp.where(row == n - k + t, halo[t:t + 1, :], out)
    return out


def _conv_specs(s, tm):
    cw = D_CV
    tile = lambda c0: pl.BlockSpec((tm, cw), lambda i: (i, c0))
    above = lambda c0: pl.BlockSpec((8, cw), lambda i: (jnp.maximum(i * (tm // 8) - 1, 0), c0))
    below = lambda c0: pl.BlockSpec((8, cw), lambda i: (jnp.minimum((i + 1) * (tm // 8), s // 8 - 1), c0))
    return tile, above, below


def _conv_fwd(proj, w8, *, name):
    s = proj.shape[0]
    tm = min(512, s)
    tile, above, _ = _conv_specs(s, tm)

    def body(gb_ref, gc_ref, xc_ref, gca_ref, xca_ref, w_ref, y_ref):
        hc = gc_ref[...].astype(F32) * xc_ref[...].astype(F32)
        top = jnp.where(pl.program_id(0) > 0, gca_ref[...].astype(F32) * xca_ref[...].astype(F32), 0.0)
        u = w_ref[0:1, :] * _shift_down(hc, top, 2) + w_ref[1:2, :] * _shift_down(hc, top, 1) + w_ref[2:3, :] * hc
        y_ref[...] = (gb_ref[...].astype(F32) * u).astype(y_ref.dtype)

    cb, cc, cx = COL_GB // D_CV, COL_GC // D_CV, COL_XC // D_CV
    return pl.pallas_call(
        body, grid=(s // tm,),
        in_specs=[tile(cb), tile(cc), tile(cx), above(cc), above(cx), pl.BlockSpec((8, D_CV), lambda i: (0, 0))],
        out_specs=pl.BlockSpec((tm, D_CV), lambda i: (i, 0)),
        out_shape=jax.ShapeDtypeStruct((s, D_CV), BF16),
        compiler_params=_cparams(("parallel",)), name=name)(proj, proj, proj, proj, proj, w8)


def _conv_bwd(proj, dy_all, w8, *, name):
    s = proj.shape[0]
    tm = min(512, s)
    nt = s // tm
    tile, above, below = _conv_specs(s, tm)

    def body(gb_ref, gc_ref, xc_ref, dy_ref, gca_ref, xca_ref, gbb_ref, dyb_ref, w_ref,
             dgb_ref, dgc_ref, dxc_ref, dw_ref):
        i = pl.program_id(0)
        gb, gc, xc = gb_ref[...].astype(F32), gc_ref[...].astype(F32), xc_ref[...].astype(F32)
        dy = dy_ref[...].astype(F32)
        hc = gc * xc
        top = jnp.where(i > 0, gca_ref[...].astype(F32) * xca_ref[...].astype(F32), 0.0)
        hc1, hc2 = _shift_down(hc, top, 1), _shift_down(hc, top, 2)
        u = w_ref[0:1, :] * hc2 + w_ref[1:2, :] * hc1 + w_ref[2:3, :] * hc
        du = dy * gb
        bottom = jnp.where(i < nt - 1, dyb_ref[...].astype(F32) * gbb_ref[...].astype(F32), 0.0)
        dhc = w_ref[2:3, :] * du + w_ref[1:2, :] * _shift_up(du, bottom, 1) + w_ref[0:1, :] * _shift_up(du, bottom, 2)
        dgb_ref[...] = (dy * u).astype(dgb_ref.dtype)
        dgc_ref[...] = (dhc * xc).astype(dgc_ref.dtype)
        dxc_ref[...] = (dhc * gc).astype(dxc_ref.dtype)

        @pl.when(i == 0)
        def _():
            dw_ref[...] = jnp.zeros_like(dw_ref)
        dw_ref[0:1, :] += jnp.sum(du * hc2, axis=0, keepdims=True)
        dw_ref[1:2, :] += jnp.sum(du * hc1, axis=0, keepdims=True)
        dw_ref[2:3, :] += jnp.sum(du * hc, axis=0, keepdims=True)

    cb, cc, cx = COL_GB // D_CV, COL_GC // D_CV, COL_XC // D_CV
    cy = (D_SB + D_CH) // D_CV
    out_tile = pl.BlockSpec((tm, D_CV), lambda i: (i, 0))
    act = jax.ShapeDtypeStruct((s, D_CV), BF16)
    return pl.pallas_call(
        body, grid=(nt,),
        in_specs=[tile(cb), tile(cc), tile(cx), tile(cy), above(cc), above(cx), below(cb), below(cy),
                  pl.BlockSpec((8, D_CV), lambda i: (0, 0))],
        out_specs=(out_tile, out_tile, out_tile, pl.BlockSpec((8, D_CV), lambda i: (0, 0))),
        out_shape=(act, act, act, jax.ShapeDtypeStruct((8, D_CV), F32)),
        compiler_params=_cparams(("arbitrary",)), name=name)(proj, proj, proj, dy_all, proj, proj, proj, dy_all, w8)


def _ffn_up(h, wg, wu, *, name):
    s, d = h.shape
    f = wg.shape[1]
    tm, tn = min(512, s), _tile(f, 1408)

    def body(h_ref, wg_ref, wu_ref, g_ref, u_ref, a_ref):
        hv = h_ref[...]
        g = _dot(hv, wg_ref[...])
        u = _dot(hv, wu_ref[...])
        g_ref[...] = g.astype(BF16)
        u_ref[...] = u.astype(BF16)
        a_ref[...] = (g * jax.nn.sigmoid(g) * u).astype(BF16)

    wspec = pl.BlockSpec((d, tn), lambda j, i: (0, j))
    ospec = pl.BlockSpec((tm, tn), lambda j, i: (i, j))
    act = jax.ShapeDtypeStruct((s, f), BF16)
    return pl.pallas_call(
        body, grid=(f // tn, s // tm),
        in_specs=[pl.BlockSpec((tm, d), lambda j, i: (i, 0)), wspec, wspec],
        out_specs=(ospec, ospec, ospec), out_shape=(act, act, act),
        compiler_params=_cparams(("parallel", "parallel")), name=name)(h, wg, wu)


def _ffn_dact(dx, wd, g, u, *, name):
    s, d = dx.shape
    f = wd.shape[0]
    tm, tf = min(512, s), _tile(f, 1408)

    def body(dx_ref, wd_ref, g_ref, u_ref, dg_ref, du_ref):
        da = _dot_nt(dx_ref[...].astype(BF16), wd_ref[...])
        gv, uv = g_ref[...].astype(F32), u_ref[...].astype(F32)
        sg = jax.nn.sigmoid(gv)
        dg_ref[...] = (da * uv * sg * (1.0 + gv * (1.0 - sg))).astype(BF16)
        du_ref[...] = (da * gv * sg).astype(BF16)

    tspec = pl.BlockSpec((tm, tf), lambda j, i: (i, j))
    act = jax.ShapeDtypeStruct((s, f), BF16)
    return pl.pallas_call(
        body, grid=(f // tf, s // tm),
        in_specs=[pl.BlockSpec((tm, d), lambda j, i: (i, 0)), pl.BlockSpec((tf, d), lambda j, i: (j, 0)), tspec, tspec],
        out_specs=(tspec, tspec), out_shape=(act, act),
        compiler_params=_cparams(("parallel", "parallel")), name=name)(dx, wd, g, u)


def _loss_head(y, target, *, name):
    s, d = y.shape
    tm = min(512, s)

    def body(y_ref, t_ref, dy_ref, sq_ref):
        err = y_ref[...] - t_ref[...]
        dy_ref[...] = err * (1.0 / d)

        @pl.when(pl.program_id(0) == 0)
        def _():
            sq_ref[...] = jnp.zeros_like(sq_ref)
        sq_ref[...] += (err * err).reshape(tm // 8, 8, d).sum(axis=0)

    tile = pl.BlockSpec((tm, d), lambda i: (i, 0))
    return pl.pallas_call(
        body, grid=(s // tm,), in_specs=[tile, tile],
        out_specs=(tile, pl.BlockSpec((8, d), lambda i: (0, 0))),
        out_shape=(jax.ShapeDtypeStruct((s, d), F32), jax.ShapeDtypeStruct((8, d), F32)),
        compiler_params=_cparams(("arbitrary",)), name=name)(y, target)


def _adamw(parts, w, m, v, *, name):
    r, c = w.shape
    tr = r if r <= 512 else _tile_rows(r)

    def body(p_ref, w_ref, m_ref, v_ref, g_ref, d_ref, nm_ref, nv_ref):
        g = p_ref[0].astype(F32)
        for dev in range(1, N_DEV):
            g = g + p_ref[dev].astype(F32)
        nm = ADAM_B1 * m_ref[...] + (1.0 - ADAM_B1) * g
        nv = ADAM_B2 * v_ref[...] + (1.0 - ADAM_B2) * (g * g)
        m_hat = nm / (1.0 - ADAM_B1 ** ADAM_STEP)
        v_hat = nv / (1.0 - ADAM_B2 ** ADAM_STEP)
        g_ref[...] = g
        d_ref[...] = -ADAM_LR * (m_hat / (jnp.sqrt(v_hat) + ADAM_EPS) + ADAM_WD * w_ref[...])
        nm_ref[...] = nm
        nv_ref[...] = nv

    tile = pl.BlockSpec((tr, c), lambda i: (i, 0))
    out = jax.ShapeDtypeStruct((r, c), F32)
    return pl.pallas_call(
        body, grid=(r // tr,),
        in_specs=[pl.BlockSpec((N_DEV, tr, c), lambda i: (0, i, 0)), tile, tile, tile],
        out_specs=(tile, tile, tile, tile), out_shape=(out, out, out, out),
        compiler_params=_cparams(("parallel",)), name=name)(parts, w, m, v)


def _tile_rows(r):
    for t in (512, 256, 128, 64, 32, 16, 8):
        if r % t == 0:
            return t
    return r


def _peer(k):
    x, y, c = (lax.axis_index(a) for a in MESH_AXES)
    px = 1 - x if k & 4 else x
    py = 1 - y if k & 2 else y
    pc = 1 - c if k & 1 else c
    return (px, py, pc), 4 * px + 2 * py + pc


def _exchange(src_of, dst_of, local_sem, send_sems, recv_sems):
    x, y, c = (lax.axis_index(a) for a in MESH_AXES)
    me = 4 * x + 2 * y + c
    mine = pltpu.make_async_copy(src_of(me), dst_of(me), local_sem)
    mine.start()
    sends = []
    for k in range(1, N_DEV):
        dev, idx = _peer(k)
        cp = pltpu.make_async_remote_copy(src_ref=src_of(idx), dst_ref=dst_of(me), send_sem=send_sems.at[k - 1],
                                          recv_sem=recv_sems.at[k - 1], device_id=dev,
                                          device_id_type=pl.DeviceIdType.MESH)
        cp.start()
        sends.append(cp)
    for k in range(1, N_DEV):
        dev, idx = _peer(k)
        pltpu.make_async_remote_copy(src_ref=src_of(idx), dst_ref=dst_of(idx), send_sem=send_sems.at[k - 1],
                                     recv_sem=recv_sems.at[k - 1], device_id=dev,
                                     device_id_type=pl.DeviceIdType.MESH).wait_recv()
    for cp in sends:
        cp.wait_send()
    mine.wait()


def _comm_call(body, out_shape, space, name, *args):
    spec = pl.BlockSpec(memory_space=space)
    return pl.pallas_call(
        body, in_specs=[spec] * len(args), out_specs=spec, out_shape=out_shape,
        scratch_shapes=[pltpu.SemaphoreType.DMA, pltpu.SemaphoreType.DMA((N_DEV - 1,)),
                        pltpu.SemaphoreType.DMA((N_DEV - 1,))],
        compiler_params=pltpu.CompilerParams(vmem_limit_bytes=VMEM_LIMIT_MIB * 2 ** 20), name=name)(*args)


def _all_gather(shard, *, name, space=pl.ANY):
    def body(x_ref, out_ref, local_sem, send_sems, recv_sems):
        _exchange(lambda d: x_ref, lambda d: out_ref.at[d], local_sem, send_sems, recv_sems)

    return _comm_call(body, jax.ShapeDtypeStruct((N_DEV,) + shard.shape, shard.dtype), space, name, shard)


def _all_to_all(slabs, *, name):
    def body(x_ref, out_ref, local_sem, send_sems, recv_sems):
        _exchange(lambda d: x_ref.at[d], lambda d: out_ref.at[d], local_sem, send_sems, recv_sems)

    return _comm_call(body, jax.ShapeDtypeStruct(slabs.shape, slabs.dtype), pl.ANY, name, slabs)


def _cols_from_dev(g, depth):
    rows = g.shape[1] // depth
    return g.reshape(N_DEV, depth, rows, g.shape[2]).transpose(1, 2, 0, 3).reshape(depth, rows, -1)


def _rows_from_dev(g, depth):
    r = g.shape[1] // depth
    return g.reshape(N_DEV, depth, r, g.shape[2]).transpose(1, 0, 2, 3).reshape(depth, N_DEV * r, g.shape[2])


def _cols_to_dev(g):
    depth, rows, n = g.shape
    return g.reshape(depth, rows, N_DEV, n // N_DEV).transpose(2, 0, 1, 3).reshape(N_DEV, depth * rows, n // N_DEV)


def _rows_to_dev(g):
    depth, rows, n = g.shape
    return g.reshape(depth, N_DEV, rows // N_DEV, n).transpose(1, 0, 2, 3).reshape(N_DEV, -1, n)


def _pack(arrays):
    flat = jnp.concatenate([a.reshape(-1) for a in arrays])
    pad = (-flat.shape[0]) % (8 * LANES)
    return jnp.pad(flat, (0, pad)).reshape(-1, LANES)


def _unpack(buf, shapes):
    flat = buf.reshape(-1)
    out, pos = [], 0
    for shp in shapes:
        n = int(np.prod(shp))
        out.append(flat[pos:pos + n].reshape(shp))
        pos += n
    return out


def _fold_heads(part8):
    t = part8.sum(axis=(0, 1))
    return t[:HEAD_DIM] + t[HEAD_DIM:]


def _forward_backward(x, target, wts):
    depth = wts["w_in"].shape[0]
    row = lambda a: a.reshape(1, -1)
    saved = []
    for l in range(depth):
        t = f"l{l}_"
        h = _norm_fwd([x], row(wts["attn_norm_w"][l]), group=D_MODEL, name=t + "attn_norm")
        proj = _mm([(h, wts["w_in"][l])], nt=False, name=t + "proj")
        qw2 = row(jnp.tile(wts["q_norm_w"][l], 2))
        kw2 = row(jnp.tile(wts["k_norm_w"][l], 2))
        bias = _bias_expand(wts["rel_bias"][l], name=t + "bias")
        w8 = jnp.pad(wts["conv_w"][l], ((0, 5), (0, 0)))
        sb = [_sb_fwd(proj, hp, name=t + f"sb{hp}") for hp in range(D_SB // LANES)]
        ys = [o[0] for o in sb]
        sb_carries = [o[1:] for o in sb]
        ys += [_ch_fwd(proj, qw2, kw2, bias, hp, name=t + f"ch{hp}") for hp in range(D_CH // LANES)]
        ys.append(_conv_fwd(proj, w8, name=t + "conv"))
        yn = _norm_fwd(ys, row(wts["out_norm_w"][l]), group=HEAD_DIM, name=t + "out_norm")
        x_mid = _mm([(yn, wts["w_out"][l])], nt=False, res=x, out_dtype=F32, name=t + "out_proj")
        h2 = _norm_fwd([x_mid], row(wts["ffn_norm_w"][l]), group=D_MODEL, name=t + "ffn_norm")
        g, u, a = _ffn_up(h2, wts["w_gate"][l], wts["w_up"][l], name=t + "ffn_up")
        x_out = _mm([(a, wts["w_down"][l])], nt=False, res=x_mid, out_dtype=F32, name=t + "ffn_down")
        saved.append(dict(x=x, h=h, proj=proj, sb_carries=sb_carries, qw2=qw2, kw2=kw2, bias=bias, w8=w8, ys=ys, yn=yn, x_mid=x_mid,
                          h2=h2, g=g, u=u, a=a))
        x = x_out

    dx, sq = _loss_head(x, target, name="loss_head")
    grads = {k: [None] * depth for k in
             ("attn_norm_w", "w_in", "q_norm_w", "k_norm_w", "rel_bias", "conv_w", "out_norm_w", "w_out",
              "ffn_norm_w", "w_gate", "w_up", "w_down")}
    for l in reversed(range(depth)):
        t = f"l{l}_b_"
        sv = saved[l]
        grads["w_down"][l] = _mm_tn(sv["a"], dx, name=t + "w_down")
        dg, du = _ffn_dact(dx, wts["w_down"][l], sv["g"], sv["u"], name=t + "ffn_dact")
        grads["w_gate"][l] = _mm_tn(sv["h2"], dg, name=t + "w_gate")
        grads["w_up"][l] = _mm_tn(sv["h2"], du, name=t + "w_up")
        dh2 = _mm([(dg, wts["w_gate"][l]), (du, wts["w_up"][l])], nt=True, name=t + "ffn_dh")
        dx_mid, dw8 = _norm_bwd([sv["x_mid"]], row(wts["ffn_norm_w"][l]), dh2, group=D_MODEL, res=dx,
                                name=t + "ffn_norm")
        grads["ffn_norm_w"][l] = dw8.sum(axis=0)
        grads["w_out"][l] = _mm_tn(sv["yn"], dx_mid, name=t + "w_out")
        dyn = _mm([(dx_mid, wts["w_out"][l])], nt=True, name=t + "out_dy")
        dy, dw8 = _norm_bwd(sv["ys"], row(wts["out_norm_w"][l]), dyn, group=HEAD_DIM, out_dtype=BF16,
                            name=t + "out_norm")
        grads["out_norm_w"][l] = dw8.sum(axis=0)
        proj = sv["proj"]
        n_sb, n_ch = D_SB // LANES, D_CH // LANES
        sb = [_sb_bwd(proj, sv["sb_carries"][hp], dy, hp, name=t + f"sb{hp}") for hp in range(n_sb)]
        ch = [_ch_bwd(proj, dy, sv["qw2"], sv["kw2"], sv["bias"], hp, name=t + f"ch{hp}") for hp in range(n_ch)]
        dgb, dgc, dxc, dcw = _conv_bwd(proj, dy, sv["w8"], name=t + "conv")
        grads["conv_w"][l] = dcw[:3]
        grads["q_norm_w"][l] = _fold_heads(jnp.stack([c[4] for c in ch]))
        grads["k_norm_w"][l] = _fold_heads(jnp.stack([c[5] for c in ch]))
        grads["rel_bias"][l] = _bias_fold(jnp.concatenate([c[3] for c in ch], axis=0), name=t + "bias")
        pieces = ([o[0] for o in sb] + [o[1] for o in sb] + [o[2] for o in sb]
                  + [o[0] for o in ch] + [o[1] for o in ch] + [o[2] for o in ch] + [dgb, dgc, dxc])
        dproj = jnp.concatenate([p.astype(BF16) for p in pieces], axis=1)
        grads["w_in"][l] = _mm_tn(sv["h"], dproj, name=t + "w_in")
        dh = _mm([(dproj, wts["w_in"][l])], nt=True, name=t + "proj_dh")
        dx, dw8 = _norm_bwd([sv["x"]], row(wts["attn_norm_w"][l]), dh, group=D_MODEL, res=dx_mid,
                            name=t + "attn_norm")
        grads["attn_norm_w"][l] = dw8.sum(axis=0)
    return sq, dx, {k: jnp.stack(v) for k, v in grads.items()}


SMALL = ("attn_norm_w", "q_norm_w", "k_norm_w", "rel_bias", "conv_w", "out_norm_w", "ffn_norm_w")
ORDER = ("attn_norm_w", "w_in", "q_norm_w", "k_norm_w", "rel_bias", "conv_w", "out_norm_w", "w_out",
         "ffn_norm_w", "w_gate", "w_up", "w_down")


def kernel(x, attn_norm_w, w_in, q_norm_w, k_norm_w, rel_bias, conv_w, out_norm_w, w_out, ffn_norm_w, w_gate, w_up, w_down, loss_target, m_attn_norm_w, m_w_in, m_q_norm_w, m_k_norm_w, m_rel_bias, m_conv_w, m_out_norm_w, m_w_out, m_ffn_norm_w, m_w_gate, m_w_up, m_w_down, v_attn_norm_w, v_w_in, v_q_norm_w, v_k_norm_w, v_rel_bias, v_conv_w, v_out_norm_w, v_w_out, v_ffn_norm_w, v_w_gate, v_w_up, v_w_down):
    w = dict(attn_norm_w=attn_norm_w, w_in=w_in, q_norm_w=q_norm_w, k_norm_w=k_norm_w, rel_bias=rel_bias,
             conv_w=conv_w, out_norm_w=out_norm_w, w_out=w_out, ffn_norm_w=ffn_norm_w, w_gate=w_gate, w_up=w_up,
             w_down=w_down)
    mom = dict(attn_norm_w=m_attn_norm_w, w_in=m_w_in, q_norm_w=m_q_norm_w, k_norm_w=m_k_norm_w, rel_bias=m_rel_bias,
               conv_w=m_conv_w, out_norm_w=m_out_norm_w, w_out=m_w_out, ffn_norm_w=m_ffn_norm_w, w_gate=m_w_gate,
               w_up=m_w_up, w_down=m_w_down)
    var = dict(attn_norm_w=v_attn_norm_w, w_in=v_w_in, q_norm_w=v_q_norm_w, k_norm_w=v_k_norm_w, rel_bias=v_rel_bias,
               conv_w=v_conv_w, out_norm_w=v_out_norm_w, w_out=v_w_out, ffn_norm_w=v_ffn_norm_w, w_gate=v_w_gate,
               w_up=v_w_up, w_down=v_w_down)
    depth = w_in.shape[0]
    seq = x.shape[1]
    me = 4 * lax.axis_index("x") + 2 * lax.axis_index("y") + lax.axis_index("c")
    flat2 = lambda a: a.reshape(-1, a.shape[-1])

    full = {k: w[k] for k in SMALL if k != "conv_w"}
    for k in ("w_in", "w_gate", "w_up"):
        full[k] = _cols_from_dev(_all_gather(flat2(w[k]).astype(BF16), name="gather_" + k), depth)
    for k in ("w_out", "w_down"):
        full[k] = _rows_from_dev(_all_gather(flat2(w[k]).astype(BF16), name="gather_" + k), depth)
    conv_shard = conv_w.shape[2]
    conv_all = _all_gather(_pack([conv_w]), name="gather_conv_w", space=pltpu.VMEM)
    conv_all = conv_all.reshape(N_DEV, -1)[:, :depth * 3 * conv_shard].reshape(N_DEV, depth, 3, conv_shard)
    full["conv_w"] = conv_all.transpose(1, 2, 0, 3).reshape(depth, 3, N_DEV * conv_shard)

    sq, dx, grads = _forward_backward(x.reshape(seq, D_MODEL), loss_target.reshape(seq, D_MODEL), full)
    loss = lax.psum(0.5 / D_MODEL * jnp.sum(sq), MESH_AXES)

    outs = {}
    for k in ("w_in", "w_gate", "w_up", "w_out", "w_down"):
        slabs = _cols_to_dev(grads[k]) if k in ("w_in", "w_gate", "w_up") else _rows_to_dev(grads[k])
        parts = _all_to_all(slabs.astype(BF16), name="scatter_" + k)
        res = _adamw(parts, flat2(w[k]), flat2(mom[k]), flat2(var[k]), name="adamw_" + k)
        outs[k] = [r.reshape(w[k].shape) for r in res]

    shapes = [grads[k].shape for k in SMALL]
    col0 = me * conv_shard

    def widen(a):
        return lax.dynamic_update_slice(jnp.zeros((depth, 3, N_DEV * conv_shard), F32), a, (0, 0, col0))

    parts = _all_gather(_pack([grads[k] for k in SMALL]), name="gather_small_grads", space=pltpu.VMEM)
    pw = _pack([widen(w[k]) if k == "conv_w" else w[k] for k in SMALL])
    pm = _pack([widen(mom[k]) if k == "conv_w" else mom[k] for k in SMALL])
    pv = _pack([widen(var[k]) if k == "conv_w" else var[k] for k in SMALL])
    res = [_unpack(r, shapes) for r in _adamw(parts, pw, pm, pv, name="adamw_small")]
    for idx, k in enumerate(SMALL):
        vals = [r[idx] for r in res]
        if k == "conv_w":
            vals = [lax.dynamic_slice(a, (0, 0, col0), (depth, 3, conv_shard)) for a in vals]
        outs[k] = vals

    result = [loss, dx.reshape(x.shape)]
    for part in range(4):
        result += [outs[k][part] for k in ORDER]
    return tuple(result)
```

```python
import numpy as np
import jax
import jax.numpy as jnp
from jax import lax
from jax.experimental import pallas as pl
from jax.experimental.pallas import tpu as pltpu

F32 = jnp.float32
BF16 = jnp.bfloat16

N_DEV = 8
D_MODEL = 1024
HEAD_DIM = 64
LANES = 128
SB_BLOCK = 128
SB_GROUP = 8
SB_HALF = SB_GROUP // 2
SB_LANE_LOWER, SB_LANE_UPPER = 127, 126
CH_BLOCK = 128
CH_PAD = 512
CH_BAND = 640
REL_CLIP = 128
N_REL = 2 * REL_CLIP + 1
REL_PAD = 384
D_SB, D_CH, D_CV = 256, 512, 256
COL_QA, COL_KA, COL_VA = 0, 256, 512
COL_QB, COL_KB, COL_VB = 768, 1280, 1792
COL_GB, COL_GC, COL_XC = 2304, 2560, 2816
EPS = 1e-6
NEG = -1e30
SCALE = HEAD_DIM ** -0.5

ADAM_LR, ADAM_B1, ADAM_B2, ADAM_EPS, ADAM_WD, ADAM_STEP = 0.001, 0.9, 0.999, 1e-08, 0.01, 10

VMEM_LIMIT_MIB = 56
MM_WEIGHT_TILE_BYTES = 8 * 2 ** 20
MESH_AXES = ("x", "y", "c")


def _cparams(sem=None):
    return pltpu.CompilerParams(dimension_semantics=sem, vmem_limit_bytes=VMEM_LIMIT_MIB * 2 ** 20)


def _dot(a, b):
    return lax.dot_general(a, b, (((1,), (0,)), ((), ())), preferred_element_type=F32)


def _dot_nt(a, b):
    return lax.dot_general(a, b, (((1,), (1,)), ((), ())), preferred_element_type=F32)


def _dot_tn(a, b):
    return lax.dot_general(a, b, (((0,), (0,)), ((), ())), preferred_element_type=F32)


def _tile(n, cap):
    if n <= cap:
        return n
    best = None
    for t in range(128, cap + 1, 128):
        if n % t == 0:
            best = t
    assert best is not None, (n, cap)
    return best


def _cat(vals):
    return vals[0] if len(vals) == 1 else jnp.concatenate(vals, axis=1)


def _mm(pairs, *, nt, name, res=None, out_dtype=BF16):
    m = pairs[0][0].shape[0]
    n = pairs[0][1].shape[0] if nt else pairs[0][1].shape[1]
    tm = min(512, m)
    tn = n
    while sum(a.shape[1] for a, _ in pairs) * tn * 2 > MM_WEIGHT_TILE_BYTES and tn % 256 == 0:
        tn //= 2
    npairs = len(pairs)

    def body(*refs):
        acc = None
        for p in range(npairs):
            a = refs[2 * p][...].astype(BF16)
            b = refs[2 * p + 1][...]
            d = _dot_nt(a, b) if nt else _dot(a, b)
            acc = d if acc is None else acc + d
        if res is not None:
            acc = acc + refs[2 * npairs][...]
        refs[-1][...] = acc.astype(refs[-1].dtype)

    in_specs, args = [], []
    for a, b in pairs:
        k = a.shape[1]
        in_specs.append(pl.BlockSpec((tm, k), lambda j, i: (i, 0)))
        if nt:
            in_specs.append(pl.BlockSpec((tn, k), lambda j, i: (j, 0)))
        else:
            in_specs.append(pl.BlockSpec((k, tn), lambda j, i: (0, j)))
        args += [a, b]
    if res is not None:
        in_specs.append(pl.BlockSpec((tm, tn), lambda j, i: (i, j)))
        args.append(res)
    return pl.pallas_call(
        body, grid=(n // tn, m // tm), in_specs=in_specs,
        out_specs=pl.BlockSpec((tm, tn), lambda j, i: (i, j)),
        out_shape=jax.ShapeDtypeStruct((m, n), out_dtype),
        compiler_params=_cparams(("parallel", "parallel")), name=name)(*args)


def _mm_tn(a, g, *, name):
    s, k = a.shape
    n = g.shape[1]
    tk, tn, ts = _tile(k, 1408), _tile(n, 1536), min(512, s)

    def body(a_ref, g_ref, o_ref):
        @pl.when(pl.program_id(2) == 0)
        def _():
            o_ref[...] = jnp.zeros_like(o_ref)
        o_ref[...] += _dot_tn(a_ref[...].astype(BF16), g_ref[...].astype(BF16))

    return pl.pallas_call(
        body, grid=(k // tk, n // tn, s // ts),
        in_specs=[pl.BlockSpec((ts, tk), lambda i, j, t: (t, i)),
                  pl.BlockSpec((ts, tn), lambda i, j, t: (t, j))],
        out_specs=pl.BlockSpec((tk, tn), lambda i, j, t: (i, j)),
        out_shape=jax.ShapeDtypeStruct((k, n), F32),
        compiler_params=_cparams(("parallel", "parallel", "arbitrary")), name=name)(a, g)


def _group_indicator(width, group):
    assert width // group <= LANES
    r = lax.broadcasted_iota(jnp.int32, (width, LANES), 0) // group
    c = lax.broadcasted_iota(jnp.int32, (width, LANES), 1)
    return (r == c).astype(BF16)


def _split_bf16(v):
    hi = v.astype(BF16)
    return hi, (v - hi.astype(F32)).astype(BF16)


def _group_mean(v, ind, group):
    hi, lo = _split_bf16(v)
    hi, lo = _split_bf16((_dot(hi, ind) + _dot(lo, ind)) * (1.0 / group))
    return _dot_nt(hi, ind) + _dot_nt(lo, ind)


def _norm_fwd(xs, w, *, group, name):
    s = xs[0].shape[0]
    width = sum(x.shape[1] for x in xs)
    tm = min(512, s)
    nx = len(xs)
    grouped = group != width

    def body(*refs):
        x = _cat([r[...].astype(F32) for r in refs[:nx]])
        if grouped:
            ms = _group_mean(x * x, refs[nx + 1][...], group)
        else:
            ms = jnp.mean(x * x, axis=1, keepdims=True)
        refs[-1][...] = (x * lax.rsqrt(ms + EPS) * refs[nx][...]).astype(BF16)

    in_specs = [pl.BlockSpec((tm, x.shape[1]), lambda i: (i, 0)) for x in xs]
    in_specs.append(pl.BlockSpec((1, width), lambda i: (0, 0)))
    args = list(xs) + [w]
    if grouped:
        in_specs.append(pl.BlockSpec((width, LANES), lambda i: (0, 0)))
        args.append(_group_indicator(width, group))
    return pl.pallas_call(
        body, grid=(s // tm,), in_specs=in_specs,
        out_specs=pl.BlockSpec((tm, width), lambda i: (i, 0)),
        out_shape=jax.ShapeDtypeStruct((s, width), BF16),
        compiler_params=_cparams(("parallel",)), name=name)(*args)


def _norm_bwd(xs, w, dy, *, group, name, res=None, out_dtype=F32):
    s = xs[0].shape[0]
    width = sum(x.shape[1] for x in xs)
    tm = min(512, s)
    nx = len(xs)
    grouped = group != width
    has_res = res is not None

    def body(*refs):
        x = _cat([r[...].astype(F32) for r in refs[:nx]])
        w_ref, dy_ref = refs[nx], refs[nx + 1]
        pos = nx + 2
        if has_res:
            res_ref = refs[pos]
            pos += 1
        if grouped:
            ind = refs[pos][...]
            mean = lambda v: _group_mean(v, ind, group)
        else:
            mean = lambda v: jnp.mean(v, axis=1, keepdims=True)
        dx_ref, dw_ref = refs[-2], refs[-1]
        inv = lax.rsqrt(mean(x * x) + EPS)
        xh = x * inv
        d = dy_ref[...].astype(F32)
        g = d * w_ref[...]
        dx = inv * (g - xh * mean(g * xh))
        if has_res:
            dx = dx + res_ref[...]
        dx_ref[...] = dx.astype(dx_ref.dtype)

        @pl.when(pl.program_id(0) == 0)
        def _():
            dw_ref[...] = jnp.zeros_like(dw_ref)
        dw_ref[...] += (d * xh).reshape(tm // 8, 8, width).sum(axis=0)

    in_specs = [pl.BlockSpec((tm, x.shape[1]), lambda i: (i, 0)) for x in xs]
    in_specs += [pl.BlockSpec((1, width), lambda i: (0, 0)), pl.BlockSpec((tm, width), lambda i: (i, 0))]
    args = list(xs) + [w, dy]
    if has_res:
        in_specs.append(pl.BlockSpec((tm, width), lambda i: (i, 0)))
        args.append(res)
    if grouped:
        in_specs.append(pl.BlockSpec((width, LANES), lambda i: (0, 0)))
        args.append(_group_indicator(width, group))
    return pl.pallas_call(
        body, grid=(s // tm,), in_specs=in_specs,
        out_specs=(pl.BlockSpec((tm, width), lambda i: (i, 0)), pl.BlockSpec((8, width), lambda i: (0, 0))),
        out_shape=(jax.ShapeDtypeStruct((s, width), out_dtype), jax.ShapeDtypeStruct((8, width), F32)),
        compiler_params=_cparams(("arbitrary",)), name=name)(*args)


def _head0_lanes(shape):
    return lax.broadcasted_iota(jnp.int32, shape, len(shape) - 1) < HEAD_DIM


def _split_heads(v):
    m0 = _head0_lanes(v.shape)
    zero = jnp.zeros_like(v)
    return jnp.where(m0, v, zero), jnp.where(m0, zero, v)


def _pair_sum(v):
    m0 = _head0_lanes(v.shape)
    s0 = jnp.sum(jnp.where(m0, v, 0.0), axis=1, keepdims=True)
    s1 = jnp.sum(jnp.where(m0, 0.0, v), axis=1, keepdims=True)
    return jnp.where(m0, s0, s1)


def _pair_norm(x, w2):
    inv = lax.rsqrt(_pair_sum(x * x) * (1.0 / HEAD_DIM) + EPS)
    xh = x * inv
    return xh * w2, xh, inv


def _pair_norm_bwd(dy, xh, inv, w2):
    g = dy * w2
    return inv * (g - xh * (_pair_sum(g * xh) * (1.0 / HEAD_DIM)))


def _copy_in(copies):
    for c in copies:
        c.start()
    for c in copies:
        c.wait()


def _stack(vals):
    return jnp.concatenate(list(vals), axis=0)


def _unstack(results, chains):
    blocks = []
    for res in results:
        blocks += [res[n * SB_BLOCK:(n + 1) * SB_BLOCK] for n in range(res.shape[0] // SB_BLOCK)]
    return dict(zip(chains, blocks))


def _log_rest(nz):
    return jnp.minimum(nz, 0.0) - jnp.log(1.0 + jnp.exp(-jnp.abs(nz)))


def _sb_fwd(proj, hp, *, name):
    s = proj.shape[0]
    nq = s // SB_BLOCK
    assert nq % SB_GROUP == 0 and nq // SB_GROUP <= SB_LANE_UPPER
    shape = (SB_BLOCK, SB_BLOCK)

    def body(q_ref, proj_hbm, y_ref, c0_ref, c1_ref, k_scr, v_scr, sems):
        i = pl.program_id(0)

        @pl.when(i == 0)
        def _():
            _copy_in([pltpu.make_async_copy(proj_hbm.at[:, pl.ds(COL_KA + hp * LANES, LANES)], k_scr, sems.at[0]),
                      pltpu.make_async_copy(proj_hbm.at[:, pl.ds(COL_VA + hp * LANES, LANES)], v_scr, sems.at[1])])

        row = lax.broadcasted_iota(jnp.int32, shape, 0)
        col = lax.broadcasted_iota(jnp.int32, shape, 1)
        m0 = col < HEAD_DIM
        later_keys = (row > col).astype(BF16)
        q_neg = _stack(_split_heads(q_ref[...] * (-SCALE)))

        def blocks_of(first, n, carry, masked, lane):
            c, acc, seen = list(carry[:2]), carry[2], list(carry[3:5])
            chains = [(b, h) for b in range(n) for h in range(2)]
            js = [first + b for b in reversed(range(n))]
            rows = [pl.ds(pl.multiple_of(j * SB_BLOCK, SB_BLOCK), SB_BLOCK) for j in js]
            earlier = [(col + j * SB_BLOCK < row + i * SB_BLOCK) if masked else None for j in js]
            nz = _unstack([_dot_nt(q_neg, k_scr[r, :]) for r in rows], chains)
            lr = {ch: _log_rest(nz[ch]) for ch in chains}
            lrm = {(b, h): jnp.where(earlier[b], lr[b, h], 0.0) if masked else lr[b, h] for b, h in chains}
            after = _unstack([_dot(_stack([lrm[ch].astype(BF16) for ch in chains]), later_keys)], chains)
            seen = [jnp.where(col == lane, c[h], seen[h]) for h in range(2)]
            w = {}
            for b, h in chains:
                wv = jnp.exp((lr[b, h] - nz[b, h]) + (after[b, h] + c[h]))
                w[b, h] = jnp.where(earlier[b], wv, 0.0) if masked else wv
                c[h] = c[h] + jnp.sum(lrm[b, h], axis=1, keepdims=True)
            pv = _unstack([_dot(_stack([w[b, 0].astype(BF16), w[b, 1].astype(BF16)]), v_scr[rows[b], :])
                           for b in range(n)], chains)
            for b in range(n):
                acc = acc + jnp.where(m0, pv[b, 0], pv[b, 1])
            return c[0], c[1], acc, seen[0], seen[1]

        zc = jnp.zeros((SB_BLOCK, 1), F32)
        zt = jnp.zeros(shape, F32)
        gi = i // SB_GROUP
        carry = lax.cond(i % SB_GROUP >= SB_HALF,
                         lambda cr: blocks_of(gi * SB_GROUP + SB_HALF, SB_HALF, cr, True, SB_LANE_UPPER),
                         lambda cr: cr, (zc, zc, zt, zt, zt))
        carry = blocks_of(gi * SB_GROUP, SB_HALF, carry, True, SB_LANE_LOWER)
        carry = lax.fori_loop(
            0, gi, lambda t, cr: blocks_of((gi - 1 - t) * SB_GROUP, SB_GROUP, cr, False, gi - 1 - t), carry)
        y_ref[...] = carry[2].astype(y_ref.dtype)
        c0_ref[...] = carry[3]
        c1_ref[...] = carry[4]

    return pl.pallas_call(
        body, grid=(nq,),
        in_specs=[pl.BlockSpec(shape, lambda i: (i, COL_QA // LANES + hp)), pl.BlockSpec(memory_space=pl.ANY)],
        out_specs=(pl.BlockSpec(shape, lambda i: (i, 0)),) * 3,
        out_shape=(jax.ShapeDtypeStruct((s, LANES), BF16), jax.ShapeDtypeStruct((s, LANES), F32),
                   jax.ShapeDtypeStruct((s, LANES), F32)),
        scratch_shapes=[pltpu.VMEM((s, LANES), BF16), pltpu.VMEM((s, LANES), BF16), pltpu.SemaphoreType.DMA((2,))],
        compiler_params=_cparams(("arbitrary",)), name=name)(proj, proj)


def _sb_bwd(proj, carries, dy_all, hp, *, name):
    s = proj.shape[0]
    nq = s // SB_BLOCK
    shape = (SB_BLOCK, SB_BLOCK)

    def body(q_ref, c0_ref, c1_ref, dy_ref, proj_hbm, dq_ref, dk_hbm, dv_hbm, k_scr, v_scr, dk_scr, dv_scr, sems):
        i = pl.program_id(0)

        @pl.when(i == 0)
        def _():
            _copy_in([pltpu.make_async_copy(proj_hbm.at[:, pl.ds(COL_KA + hp * LANES, LANES)], k_scr, sems.at[0]),
                      pltpu.make_async_copy(proj_hbm.at[:, pl.ds(COL_VA + hp * LANES, LANES)], v_scr, sems.at[1])])
            dk_scr[...] = jnp.zeros_like(dk_scr)
            dv_scr[...] = jnp.zeros_like(dv_scr)

        row = lax.broadcasted_iota(jnp.int32, shape, 0)
        col = lax.broadcasted_iota(jnp.int32, shape, 1)
        m0 = col < HEAD_DIM
        later_keys = (row > col).astype(BF16)
        earlier_keys = (row < col).astype(BF16)
        q_both = _stack(_split_heads(q_ref[...] * SCALE))
        q_neg = _stack(_split_heads(q_ref[...] * (-SCALE)))
        dy_both = _stack(_split_heads(dy_ref[...]))
        seen = (c0_ref[...], c1_ref[...])

        def blocks_of(first, n, carry, masked, lane):
            r, dq = list(carry[0:2]), list(carry[2:4])
            blocks = range(n)
            chains = [(b, h) for b in blocks for h in range(2)]
            js = [first + b for b in blocks]
            rows = [pl.ds(pl.multiple_of(j * SB_BLOCK, SB_BLOCK), SB_BLOCK) for j in js]
            earlier = [(col + j * SB_BLOCK < row + i * SB_BLOCK) if masked else None for j in js]
            u = _unstack([_dot_nt(q_neg, k_scr[rw, :]) for rw in rows], chains)
            dw = _unstack([_dot_nt(dy_both, v_scr[rw, :]) for rw in rows], chains)
            lr = {ch: _log_rest(u[ch]) for ch in chains}
            lrm = {(b, h): jnp.where(earlier[b], lr[b, h], 0.0) if masked else lr[b, h] for b, h in chains}
            after = _unstack([_dot(_stack([lrm[ch].astype(BF16) for ch in chains]), later_keys)], chains)
            c = {}
            for h in range(2):
                ch = jnp.sum(jnp.where(col == lane, seen[h], 0.0), axis=1, keepdims=True)
                for b in reversed(blocks):
                    c[b, h] = ch
                    ch = ch + jnp.sum(lrm[b, h], axis=1, keepdims=True)
            w, de, lb = {}, {}, {}
            for b, h in chains:
                lb[b, h] = lr[b, h] - u[b, h]
                wv = jnp.exp(lb[b, h] + (after[b, h] + c[b, h]))
                w[b, h] = jnp.where(earlier[b], wv, 0.0) if masked else wv
                de[b, h] = dw[b, h] * w[b, h]
            before = _unstack([_dot(_stack([de[ch].astype(BF16) for ch in chains]), earlier_keys)], chains)
            dzb = {}
            for b, h in chains:
                dz = de[b, h] - jnp.exp(lb[b, h]) * (de[b, h] + (r[h] + before[b, h]))
                if masked:
                    dz = jnp.where(earlier[b], dz, 0.0)
                dzb[b, h] = dz.astype(BF16)
                r[h] = r[h] + jnp.sum(de[b, h], axis=1, keepdims=True)
            keys = pl.ds(pl.multiple_of(first * SB_BLOCK, SB_BLOCK), n * SB_BLOCK)
            dz_wide = [jnp.concatenate([dzb[b, h] for b in blocks], axis=1) for h in range(2)]
            w_wide = [jnp.concatenate([w[b, h].astype(BF16) for b in blocks], axis=1) for h in range(2)]
            dqc = _dot(_stack(dz_wide), k_scr[keys, :])
            dq = [dq[0] + dqc[:SB_BLOCK], dq[1] + dqc[SB_BLOCK:]]
            dk_scr[keys, :] += _dot_tn(_stack(dz_wide), q_both)
            dv_scr[keys, :] += _dot_tn(_stack(w_wide), dy_both)
            return r[0], r[1], dq[0], dq[1]

        zc = jnp.zeros((SB_BLOCK, 1), F32)
        zq = jnp.zeros(shape, F32)
        gi = i // SB_GROUP
        carry = lax.fori_loop(0, gi, lambda t, cr: blocks_of(t * SB_GROUP, SB_GROUP, cr, False, t), (zc, zc, zq, zq))
        carry = blocks_of(gi * SB_GROUP, SB_HALF, carry, True, SB_LANE_LOWER)
        carry = lax.cond(i % SB_GROUP >= SB_HALF,
                         lambda cr: blocks_of(gi * SB_GROUP + SB_HALF, SB_HALF, cr, True, SB_LANE_UPPER),
                         lambda cr: cr, carry)
        dq_ref[...] = (jnp.where(m0, carry[2], carry[3]) * SCALE).astype(dq_ref.dtype)

        @pl.when(i == nq - 1)
        def _():
            _copy_in([pltpu.make_async_copy(dk_scr, dk_hbm, sems.at[0]),
                      pltpu.make_async_copy(dv_scr, dv_hbm, sems.at[1])])

    blk = lambda c0: pl.BlockSpec(shape, lambda i: (i, c0))
    return pl.pallas_call(
        body, grid=(nq,),
        in_specs=[blk(COL_QA // LANES + hp), blk(0), blk(0), blk(hp), pl.BlockSpec(memory_space=pl.ANY)],
        out_specs=(blk(0), pl.BlockSpec(memory_space=pl.ANY), pl.BlockSpec(memory_space=pl.ANY)),
        out_shape=(jax.ShapeDtypeStruct((s, LANES), BF16), jax.ShapeDtypeStruct((s, LANES), F32),
                   jax.ShapeDtypeStruct((s, LANES), F32)),
        scratch_shapes=[pltpu.VMEM((s, LANES), BF16), pltpu.VMEM((s, LANES), BF16),
                        pltpu.VMEM((s, LANES), F32), pltpu.VMEM((s, LANES), F32), pltpu.SemaphoreType.DMA((2,))],
        compiler_params=_cparams(("arbitrary",)), name=name)(proj, carries[0], carries[1], dy_all, proj)


CH_DIAG = CH_BAND + CH_BLOCK


def _diag_index():
    x = np.arange(CH_DIAG)[None, :]
    return (np.clip(CH_PAD + (CH_BLOCK - 1) - x, -REL_CLIP, REL_CLIP) + REL_CLIP).astype(np.int32)


def _diag_onehot(idx_ref):
    rid = lax.broadcasted_iota(jnp.int32, (REL_PAD, CH_DIAG), 0)
    return (rid == idx_ref[...]).astype(F32)


def _bias_expand(rel_bias, *, name):
    nh = rel_bias.shape[0]
    table = jnp.pad(rel_bias, ((0, 0), (0, REL_PAD - N_REL)))

    def body(tab_ref, idx_ref, out_ref):
        diag = lax.dot_general(tab_ref[...], _diag_onehot(idx_ref), (((1,), (0,)), ((), ())),
                               precision=lax.Precision.HIGHEST, preferred_element_type=F32)

        def step(p, _):
            out_ref[p] = pltpu.roll(diag, (CH_DIAG - (CH_BLOCK - 1) + p) % CH_DIAG, 1)[:, :CH_BAND]
            return 0

        lax.fori_loop(0, CH_BLOCK, step, 0)

    out = pl.pallas_call(
        body, out_shape=jax.ShapeDtypeStruct((CH_BLOCK, nh, CH_BAND), F32),
        compiler_params=_cparams(), name=name)(table, jnp.asarray(_diag_index()))
    return out.transpose(1, 0, 2)


def _bias_fold(dbias, *, name):
    nh = dbias.shape[0]

    def body(db_ref, idx_ref, out_ref):
        def step(p, acc):
            wide = jnp.concatenate([db_ref[p], jnp.zeros((nh, CH_BLOCK), F32)], axis=1)
            return acc + pltpu.roll(wide, CH_BLOCK - 1 - p, 1)

        diag = lax.fori_loop(0, CH_BLOCK, step, jnp.zeros((nh, CH_DIAG), F32))
        out_ref[...] = lax.dot_general(diag, _diag_onehot(idx_ref), (((1,), (1,)), ((), ())),
                                       precision=lax.Precision.HIGHEST, preferred_element_type=F32)

    out = pl.pallas_call(
        body, out_shape=jax.ShapeDtypeStruct((nh, REL_PAD), F32),
        compiler_params=_cparams(), name=name)(dbias.transpose(1, 0, 2), jnp.asarray(_diag_index()))
    return out[:, :N_REL]


def _band_valid(i):
    row = lax.broadcasted_iota(jnp.int32, (CH_BLOCK, CH_BAND), 0)
    col = lax.broadcasted_iota(jnp.int32, (CH_BLOCK, CH_BAND), 1)
    first = row < CH_BLOCK // 2
    window = (first & (col < CH_BAND - 64)) | (jnp.logical_not(first) & (col >= 64))
    return window & (col + i * CH_BLOCK >= CH_PAD)


def _ch_load(proj_hbm, hp, s, kw_ref, kraw_scr, kn_scr, vp_scr, sems):
    _copy_in([pltpu.make_async_copy(proj_hbm.at[:, pl.ds(COL_KB + hp * LANES, LANES)], kraw_scr, sems.at[0]),
              pltpu.make_async_copy(proj_hbm.at[:, pl.ds(COL_VB + hp * LANES, LANES)],
                                    vp_scr.at[pl.ds(CH_PAD, s), :], sems.at[1])])
    kn_scr[0:CH_PAD, :] = jnp.zeros((CH_PAD, LANES), BF16)
    vp_scr[0:CH_PAD, :] = jnp.zeros((CH_PAD, LANES), BF16)
    rb = min(512, s)

    def step(t, _):
        r = pl.multiple_of(t * rb, rb)
        kn = _pair_norm(kraw_scr[pl.ds(r, rb), :].astype(F32), kw_ref[...])[0]
        kn_scr[pl.ds(CH_PAD + r, rb), :] = kn.astype(BF16)
        return 0

    lax.fori_loop(0, s // rb, step, 0)


def _ch_probs(qk, bias, valid):
    sc = jnp.where(valid, qk * SCALE + bias, NEG)
    p = jnp.exp(sc - jnp.max(sc, axis=1, keepdims=True))
    return p, 1.0 / jnp.sum(p, axis=1, keepdims=True)


def _ch_fwd(proj, qw2, kw2, bias, hp, *, name):
    s = proj.shape[0]
    nb = s // CH_BLOCK
    shape = (CH_BLOCK, LANES)

    def body(q_ref, proj_hbm, qw_ref, kw_ref, bias_ref, y_ref, kraw_scr, kn_scr, vp_scr, sems):
        i = pl.program_id(0)

        @pl.when(i == 0)
        def _():
            _ch_load(proj_hbm, hp, s, kw_ref, kraw_scr, kn_scr, vp_scr, sems)

        band = pl.ds(pl.multiple_of(i * CH_BLOCK, CH_BLOCK), CH_BAND)
        kb, vb = kn_scr[band, :], vp_scr[band, :]
        valid = _band_valid(i)
        qn = _pair_norm(q_ref[...].astype(F32), qw_ref[...])[0]
        scores = _dot_nt(_stack(_split_heads(qn.astype(BF16))), kb)
        probs = [_ch_probs(scores[h * CH_BLOCK:(h + 1) * CH_BLOCK], bias_ref[h], valid) for h in range(2)]
        outs = _dot(_stack([p.astype(BF16) for p, _ in probs]), vb)
        outs = [outs[h * CH_BLOCK:(h + 1) * CH_BLOCK] * probs[h][1] for h in range(2)]
        y_ref[...] = jnp.where(_head0_lanes(shape), outs[0], outs[1]).astype(y_ref.dtype)

    return pl.pallas_call(
        body, grid=(nb,),
        in_specs=[pl.BlockSpec(shape, lambda i: (i, COL_QB // LANES + hp)), pl.BlockSpec(memory_space=pl.ANY),
                  pl.BlockSpec((1, LANES), lambda i: (0, 0)), pl.BlockSpec((1, LANES), lambda i: (0, 0)),
                  pl.BlockSpec((2, CH_BLOCK, CH_BAND), lambda i: (hp, 0, 0))],
        out_specs=pl.BlockSpec(shape, lambda i: (i, 0)),
        out_shape=jax.ShapeDtypeStruct((s, LANES), BF16),
        scratch_shapes=[pltpu.VMEM((s, LANES), BF16), pltpu.VMEM((s + CH_PAD, LANES), BF16),
                        pltpu.VMEM((s + CH_PAD, LANES), BF16), pltpu.SemaphoreType.DMA((2,))],
        compiler_params=_cparams(("arbitrary",)), name=name)(proj, proj, qw2, kw2, bias)


def _ch_bwd(proj, dy_all, qw2, kw2, bias, hp, *, name):
    s = proj.shape[0]
    nb = s // CH_BLOCK
    shape = (CH_BLOCK, LANES)
    rb = min(512, s)

    def body(q_ref, dy_ref, proj_hbm, qw_ref, kw_ref, bias_ref, dq_ref, dk_hbm, dv_hbm, dbias_ref, dqw_ref, dkw_ref,
             kraw_scr, kn_scr, vp_scr, dkn_scr, dvp_scr, sems):
        i = pl.program_id(0)

        @pl.when(i == 0)
        def _():
            _ch_load(proj_hbm, hp, s, kw_ref, kraw_scr, kn_scr, vp_scr, sems)
            dkn_scr[...] = jnp.zeros_like(dkn_scr)
            dvp_scr[...] = jnp.zeros_like(dvp_scr)
            dbias_ref[...] = jnp.zeros_like(dbias_ref)
            dqw_ref[...] = jnp.zeros_like(dqw_ref)

        band = pl.ds(pl.multiple_of(i * CH_BLOCK, CH_BLOCK), CH_BAND)
        kb, vb = kn_scr[band, :], vp_scr[band, :]
        valid = _band_valid(i)
        qn, qhat, qinv = _pair_norm(q_ref[...].astype(F32), qw_ref[...])
        q_both = _stack(_split_heads(qn.astype(BF16)))
        dy_both = _stack(_split_heads(dy_ref[...]))
        scores = _dot_nt(q_both, kb)
        dprobs = _dot_nt(dy_both, vb)
        pb, dsb = [], []
        for h in range(2):
            part = slice(h * CH_BLOCK, (h + 1) * CH_BLOCK)
            p, rl = _ch_probs(scores[part], bias_ref[h], valid)
            p = p * rl
            dp = dprobs[part]
            ds = p * (dp - jnp.sum(dp * p, axis=1, keepdims=True))
            dbias_ref[h] += ds
            pb.append(p.astype(BF16))
            dsb.append((ds * SCALE).astype(BF16))
        dsb = _stack(dsb)
        dqn = _dot(dsb, kb)
        dkn_scr[band, :] += _dot_tn(dsb, q_both)
        dvp_scr[band, :] += _dot_tn(_stack(pb), dy_both)
        dqn = jnp.where(_head0_lanes(shape), dqn[:CH_BLOCK], dqn[CH_BLOCK:])
        dq_ref[...] = _pair_norm_bwd(dqn, qhat, qinv, qw_ref[...]).astype(dq_ref.dtype)
        dqw_ref[...] += (dqn * qhat).reshape(CH_BLOCK // 8, 8, LANES).sum(axis=0)

        @pl.when(i == nb - 1)
        def _():
            def step(t, acc):
                r = pl.multiple_of(t * rb, rb)
                rows = pl.ds(CH_PAD + r, rb)
                _, khat, kinv = _pair_norm(kraw_scr[pl.ds(r, rb), :].astype(F32), kw_ref[...])
                dkn = dkn_scr[rows, :]
                dkn_scr[rows, :] = _pair_norm_bwd(dkn, khat, kinv, kw_ref[...])
                return acc + (dkn * khat).reshape(rb // 8, 8, LANES).sum(axis=0)

            dkw_ref[...] = lax.fori_loop(0, s // rb, step, jnp.zeros((8, LANES), F32))
            _copy_in([pltpu.make_async_copy(dkn_scr.at[pl.ds(CH_PAD, s), :], dk_hbm, sems.at[0]),
                      pltpu.make_async_copy(dvp_scr.at[pl.ds(CH_PAD, s), :], dv_hbm, sems.at[1])])

    blk = lambda c0: pl.BlockSpec(shape, lambda i: (i, c0))
    vec = pl.BlockSpec((1, LANES), lambda i: (0, 0))
    part = pl.BlockSpec((8, LANES), lambda i: (0, 0))
    hbm = pl.BlockSpec(memory_space=pl.ANY)
    return pl.pallas_call(
        body, grid=(nb,),
        in_specs=[blk(COL_QB // LANES + hp), blk(D_SB // LANES + hp), hbm, vec, vec,
                  pl.BlockSpec((2, CH_BLOCK, CH_BAND), lambda i: (hp, 0, 0))],
        out_specs=(blk(0), hbm, hbm, pl.BlockSpec((2, CH_BLOCK, CH_BAND), lambda i: (0, 0, 0)), part, part),
        out_shape=(jax.ShapeDtypeStruct((s, LANES), BF16), jax.ShapeDtypeStruct((s, LANES), F32),
                   jax.ShapeDtypeStruct((s, LANES), F32), jax.ShapeDtypeStruct((2, CH_BLOCK, CH_BAND), F32),
                   jax.ShapeDtypeStruct((8, LANES), F32), jax.ShapeDtypeStruct((8, LANES), F32)),
        scratch_shapes=[pltpu.VMEM((s, LANES), BF16), pltpu.VMEM((s + CH_PAD, LANES), BF16),
                        pltpu.VMEM((s + CH_PAD, LANES), BF16), pltpu.VMEM((s + CH_PAD, LANES), F32),
                        pltpu.VMEM((s + CH_PAD, LANES), F32), pltpu.SemaphoreType.DMA((2,))],
        compiler_params=_cparams(("arbitrary",)), name=name)(proj, dy_all, proj, qw2, kw2, bias)


def _shift_down(x, halo, k):
    out = pltpu.roll(x, k, 0)
    row = lax.broadcasted_iota(jnp.int32, x.shape, 0)
    for t in range(k):
        out = jnp.where(row == t, halo[8 - k + t:8 - k + t + 1, :], out)
    return out


def _shift_up(x, halo, k):
    n = x.shape[0]
    out = pltpu.roll(x, n - k, 0)
    row = lax.broadcasted_iota(jnp.int32, x.shape, 0)
    for t in range(k):
        out = jnp.where(row == n - k + t, halo[t:t + 1, :], out)
    return out


def _conv_specs(s, tm):
    cw = D_CV
    tile = lambda c0: pl.BlockSpec((tm, cw), lambda i: (i, c0))
    above = lambda c0: pl.BlockSpec((8, cw), lambda i: (jnp.maximum(i * (tm // 8) - 1, 0), c0))
    below = lambda c0: pl.BlockSpec((8, cw), lambda i: (jnp.minimum((i + 1) * (tm // 8), s // 8 - 1), c0))
    return tile, above, below


def _conv_fwd(proj, w8, *, name):
    s = proj.shape[0]
    tm = min(512, s)
    tile, above, _ = _conv_specs(s, tm)

    def body(gb_ref, gc_ref, xc_ref, gca_ref, xca_ref, w_ref, y_ref):
        hc = gc_ref[...].astype(F32) * xc_ref[...].astype(F32)
        top = jnp.where(pl.program_id(0) > 0, gca_ref[...].astype(F32) * xca_ref[...].astype(F32), 0.0)
        u = w_ref[0:1, :] * _shift_down(hc, top, 2) + w_ref[1:2, :] * _shift_down(hc, top, 1) + w_ref[2:3, :] * hc
        y_ref[...] = (gb_ref[...].astype(F32) * u).astype(y_ref.dtype)

    cb, cc, cx = COL_GB // D_CV, COL_GC // D_CV, COL_XC // D_CV
    return pl.pallas_call(
        body, grid=(s // tm,),
        in_specs=[tile(cb), tile(cc), tile(cx), above(cc), above(cx), pl.BlockSpec((8, D_CV), lambda i: (0, 0))],
        out_specs=pl.BlockSpec((tm, D_CV), lambda i: (i, 0)),
        out_shape=jax.ShapeDtypeStruct((s, D_CV), BF16),
        compiler_params=_cparams(("parallel",)), name=name)(proj, proj, proj, proj, proj, w8)


def _conv_bwd(proj, dy_all, w8, *, name):
    s = proj.shape[0]
    tm = min(512, s)
    nt = s // tm
    tile, above, below = _conv_specs(s, tm)

    def body(gb_ref, gc_ref, xc_ref, dy_ref, gca_ref, xca_ref, gbb_ref, dyb_ref, w_ref,
             dgb_ref, dgc_ref, dxc_ref, dw_ref):
        i = pl.program_id(0)
        gb, gc, xc = gb_ref[...].astype(F32), gc_ref[...].astype(F32), xc_ref[...].astype(F32)
        dy = dy_ref[...].astype(F32)
        hc = gc * xc
        top = jnp.where(i > 0, gca_ref[...].astype(F32) * xca_ref[...].astype(F32), 0.0)
        hc1, hc2 = _shift_down(hc, top, 1), _shift_down(hc, top, 2)
        u = w_ref[0:1, :] * hc2 + w_ref[1:2, :] * hc1 + w_ref[2:3, :] * hc
        du = dy * gb
        bottom = jnp.where(i < nt - 1, dyb_ref[...].astype(F32) * gbb_ref[...].astype(F32), 0.0)
        dhc = w_ref[2:3, :] * du + w_ref[1:2, :] * _shift_up(du, bottom, 1) + w_ref[0:1, :] * _shift_up(du, bottom, 2)
        dgb_ref[...] = (dy * u).astype(dgb_ref.dtype)
        dgc_ref[...] = (dhc * xc).astype(dgc_ref.dtype)
        dxc_ref[...] = (dhc * gc).astype(dxc_ref.dtype)

        @pl.when(i == 0)
        def _():
            dw_ref[...] = jnp.zeros_like(dw_ref)
        dw_ref[0:1, :] += jnp.sum(du * hc2, axis=0, keepdims=True)
        dw_ref[1:2, :] += jnp.sum(du * hc1, axis=0, keepdims=True)
        dw_ref[2:3, :] += jnp.sum(du * hc, axis=0, keepdims=True)

    cb, cc, cx = COL_GB // D_CV, COL_GC // D_CV, COL_XC // D_CV
    cy = (D_SB + D_CH) // D_CV
    out_tile = pl.BlockSpec((tm, D_CV), lambda i: (i, 0))
    act = jax.ShapeDtypeStruct((s, D_CV), BF16)
    return pl.pallas_call(
        body, grid=(nt,),
        in_specs=[tile(cb), tile(cc), tile(cx), tile(cy), above(cc), above(cx), below(cb), below(cy),
                  pl.BlockSpec((8, D_CV), lambda i: (0, 0))],
        out_specs=(out_tile, out_tile, out_tile, pl.BlockSpec((8, D_CV), lambda i: (0, 0))),
        out_shape=(act, act, act, jax.ShapeDtypeStruct((8, D_CV), F32)),
        compiler_params=_cparams(("arbitrary",)), name=name)(proj, proj, proj, dy_all, proj, proj, proj, dy_all, w8)


def _ffn_up(h, wg, wu, *, name):
    s, d = h.shape
    f = wg.shape[1]
    tm, tn = min(512, s), _tile(f, 1408)

    def body(h_ref, wg_ref, wu_ref, g_ref, u_ref, a_ref):
        hv = h_ref[...]
        g = _dot(hv, wg_ref[...])
        u = _dot(hv, wu_ref[...])
        g_ref[...] = g.astype(BF16)
        u_ref[...] = u.astype(BF16)
        a_ref[...] = (g * jax.nn.sigmoid(g) * u).astype(BF16)

    wspec = pl.BlockSpec((d, tn), lambda j, i: (0, j))
    ospec = pl.BlockSpec((tm, tn), lambda j, i: (i, j))
    act = jax.ShapeDtypeStruct((s, f), BF16)
    return pl.pallas_call(
        body, grid=(f // tn, s // tm),
        in_specs=[pl.BlockSpec((tm, d), lambda j, i: (i, 0)), wspec, wspec],
        out_specs=(ospec, ospec, ospec), out_shape=(act, act, act),
        compiler_params=_cparams(("parallel", "parallel")), name=name)(h, wg, wu)


def _ffn_dact(dx, wd, g, u, *, name):
    s, d = dx.shape
    f = wd.shape[0]
    tm, tf = min(512, s), _tile(f, 1408)

    def body(dx_ref, wd_ref, g_ref, u_ref, dg_ref, du_ref):
        da = _dot_nt(dx_ref[...].astype(BF16), wd_ref[...])
        gv, uv = g_ref[...].astype(F32), u_ref[...].astype(F32)
        sg = jax.nn.sigmoid(gv)
        dg_ref[...] = (da * uv * sg * (1.0 + gv * (1.0 - sg))).astype(BF16)
        du_ref[...] = (da * gv * sg).astype(BF16)

    tspec = pl.BlockSpec((tm, tf), lambda j, i: (i, j))
    act = jax.ShapeDtypeStruct((s, f), BF16)
    return pl.pallas_call(
        body, grid=(f // tf, s // tm),
        in_specs=[pl.BlockSpec((tm, d), lambda j, i: (i, 0)), pl.BlockSpec((tf, d), lambda j, i: (j, 0)), tspec, tspec],
        out_specs=(tspec, tspec), out_shape=(act, act),
        compiler_params=_cparams(("parallel", "parallel")), name=name)(dx, wd, g, u)


def _loss_head(y, target, *, name):
    s, d = y.shape
    tm = min(512, s)

    def body(y_ref, t_ref, dy_ref, sq_ref):
        err = y_ref[...] - t_ref[...]
        dy_ref[...] = err * (1.0 / d)

        @pl.when(pl.program_id(0) == 0)
        def _():
            sq_ref[...] = jnp.zeros_like(sq_ref)
        sq_ref[...] += (err * err).reshape(tm // 8, 8, d).sum(axis=0)

    tile = pl.BlockSpec((tm, d), lambda i: (i, 0))
    return pl.pallas_call(
        body, grid=(s // tm,), in_specs=[tile, tile],
        out_specs=(tile, pl.BlockSpec((8, d), lambda i: (0, 0))),
        out_shape=(jax.ShapeDtypeStruct((s, d), F32), jax.ShapeDtypeStruct((8, d), F32)),
        compiler_params=_cparams(("arbitrary",)), name=name)(y, target)


def _adamw(parts, w, m, v, *, name):
    r, c = w.shape
    tr = r if r <= 512 else _tile_rows(r)

    def body(p_ref, w_ref, m_ref, v_ref, g_ref, d_ref, nm_ref, nv_ref):
        g = p_ref[0].astype(F32)
        for dev in range(1, N_DEV):
            g = g + p_ref[dev].astype(F32)
        nm = ADAM_B1 * m_ref[...] + (1.0 - ADAM_B1) * g
        nv = ADAM_B2 * v_ref[...] + (1.0 - ADAM_B2) * (g * g)
        m_hat = nm / (1.0 - ADAM_B1 ** ADAM_STEP)
        v_hat = nv / (1.0 - ADAM_B2 ** ADAM_STEP)
        g_ref[...] = g
        d_ref[...] = -ADAM_LR * (m_hat / (jnp.sqrt(v_hat) + ADAM_EPS) + ADAM_WD * w_ref[...])
        nm_ref[...] = nm
        nv_ref[...] = nv

    tile = pl.BlockSpec((tr, c), lambda i: (i, 0))
    out = jax.ShapeDtypeStruct((r, c), F32)
    return pl.pallas_call(
        body, grid=(r // tr,),
        in_specs=[pl.BlockSpec((N_DEV, tr, c), lambda i: (0, i, 0)), tile, tile, tile],
        out_specs=(tile, tile, tile, tile), out_shape=(out, out, out, out),
        compiler_params=_cparams(("parallel",)), name=name)(parts, w, m, v)


def _tile_rows(r):
    for t in (512, 256, 128, 64, 32, 16, 8):
        if r % t == 0:
            return t
    return r


def _peer(k):
    x, y, c = (lax.axis_index(a) for a in MESH_AXES)
    px = 1 - x if k & 4 else x
    py = 1 - y if k & 2 else y
    pc = 1 - c if k & 1 else c
    return (px, py, pc), 4 * px + 2 * py + pc


def _exchange(src_of, dst_of, local_sem, send_sems, recv_sems):
    x, y, c = (lax.axis_index(a) for a in MESH_AXES)
    me = 4 * x + 2 * y + c
    mine = pltpu.make_async_copy(src_of(me), dst_of(me), local_sem)
    mine.start()
    sends = []
    for k in range(1, N_DEV):
        dev, idx = _peer(k)
        cp = pltpu.make_async_remote_copy(src_ref=src_of(idx), dst_ref=dst_of(me), send_sem=send_sems.at[k - 1],
                                          recv_sem=recv_sems.at[k - 1], device_id=dev,
                                          device_id_type=pl.DeviceIdType.MESH)
        cp.start()
        sends.append(cp)
    for k in range(1, N_DEV):
        dev, idx = _peer(k)
        pltpu.make_async_remote_copy(src_ref=src_of(idx), dst_ref=dst_of(idx), send_sem=send_sems.at[k - 1],
                                     recv_sem=recv_sems.at[k - 1], device_id=dev,
                                     device_id_type=pl.DeviceIdType.MESH).wait_recv()
    for cp in sends:
        cp.wait_send()
    mine.wait()


def _comm_call(body, out_shape, space, name, *args):
    spec = pl.BlockSpec(memory_space=space)
    return pl.pallas_call(
        body, in_specs=[spec] * len(args), out_specs=spec, out_shape=out_shape,
        scratch_shapes=[pltpu.SemaphoreType.DMA, pltpu.SemaphoreType.DMA((N_DEV - 1,)),
                        pltpu.SemaphoreType.DMA((N_DEV - 1,))],
        compiler_params=pltpu.CompilerParams(vmem_limit_bytes=VMEM_LIMIT_MIB * 2 ** 20), name=name)(*args)


def _gather_by_chip(x_ref, out_ref, local_sem, send_sems, recv_sems):
    x, y, c = (lax.axis_index(a) for a in MESH_AXES)
    me, sibling = (x, y, c), (x, y, 1 - c)
    chips = [(1 - x, y), (x, 1 - y), (1 - x, 1 - y)]

    def slab(dev):
        return out_ref.at[4 * dev[0] + 2 * dev[1] + dev[2]]

    def copy(k, owner, to, src=None):
        return pltpu.make_async_remote_copy(
            src_ref=slab(owner) if src is None else src, dst_ref=slab(owner), send_sem=send_sems.at[k],
            recv_sem=recv_sems.at[k], device_id=to, device_id_type=pl.DeviceIdType.MESH)

    mine = pltpu.make_async_copy(x_ref, slab(me), local_sem)
    mine.start()
    first = [copy(0, me, sibling, src=x_ref)] + [copy(1 + j, me, (*chip, c), src=x_ref) for j, chip in enumerate(chips)]
    for cp in first:
        cp.start()
    passed = [copy(4 + j, (*chip, c), sibling) for j, chip in enumerate(chips)]
    for j, chip in enumerate(chips):
        copy(1 + j, (*chip, c), me).wait_recv()
        passed[j].start()
    copy(0, sibling, me).wait_recv()
    for j, chip in enumerate(chips):
        copy(4 + j, (*chip, 1 - c), me).wait_recv()
    for cp in first + passed:
        cp.wait_send()
    mine.wait()


def _all_gather(shard, *, name, space=pl.ANY):
    def body(x_ref, out_ref, local_sem, send_sems, recv_sems):
        _gather_by_chip(x_ref, out_ref, local_sem, send_sems, recv_sems)

    return _comm_call(body, jax.ShapeDtypeStruct((N_DEV,) + shard.shape, shard.dtype), space, name, shard)


def _all_to_all(slabs, *, name):
    def body(x_ref, out_ref, local_sem, send_sems, recv_sems):
        _exchange(lambda d: x_ref.at[d], lambda d: out_ref.at[d], local_sem, send_sems, recv_sems)

    return _comm_call(body, jax.ShapeDtypeStruct(slabs.shape, slabs.dtype), pl.ANY, name, slabs)


def _cols_from_dev(g, depth):
    rows = g.shape[1] // depth
    return g.reshape(N_DEV, depth, rows, g.shape[2]).transpose(1, 2, 0, 3).reshape(depth, rows, -1)


def _rows_from_dev(g, depth):
    r = g.shape[1] // depth
    return g.reshape(N_DEV, depth, r, g.shape[2]).transpose(1, 0, 2, 3).reshape(depth, N_DEV * r, g.shape[2])


def _cols_to_dev(g):
    depth, rows, n = g.shape
    return g.reshape(depth, rows, N_DEV, n // N_DEV).transpose(2, 0, 1, 3).reshape(N_DEV, depth * rows, n // N_DEV)


def _rows_to_dev(g):
    depth, rows, n = g.shape
    return g.reshape(depth, N_DEV, rows // N_DEV, n).transpose(1, 0, 2, 3).reshape(N_DEV, -1, n)


def _pack(arrays):
    flat = jnp.concatenate([a.reshape(-1) for a in arrays])
    pad = (-flat.shape[0]) % (8 * LANES)
    return jnp.pad(flat, (0, pad)).reshape(-1, LANES)


def _unpack(buf, shapes):
    flat = buf.reshape(-1)
    out, pos = [], 0
    for shp in shapes:
        n = int(np.prod(shp))
        out.append(flat[pos:pos + n].reshape(shp))
        pos += n
    return out


def _fold_heads(part8):
    t = part8.sum(axis=(0, 1))
    return t[:HEAD_DIM] + t[HEAD_DIM:]


def _forward_backward(x, target, wts):
    depth = wts["w_in"].shape[0]
    row = lambda a: a.reshape(1, -1)
    saved = []
    for l in range(depth):
        t = f"l{l}_"
        h = _norm_fwd([x], row(wts["attn_norm_w"][l]), group=D_MODEL, name=t + "attn_norm")
        proj = _mm([(h, wts["w_in"][l])], nt=False, name=t + "proj")
        qw2 = row(jnp.tile(wts["q_norm_w"][l], 2))
        kw2 = row(jnp.tile(wts["k_norm_w"][l], 2))
        bias = _bias_expand(wts["rel_bias"][l], name=t + "bias")
        w8 = jnp.pad(wts["conv_w"][l], ((0, 5), (0, 0)))
        sb = [_sb_fwd(proj, hp, name=t + f"sb{hp}") for hp in range(D_SB // LANES)]
        ys = [o[0] for o in sb]
        sb_carries = [o[1:] for o in sb]
        ys += [_ch_fwd(proj, qw2, kw2, bias, hp, name=t + f"ch{hp}") for hp in range(D_CH // LANES)]
        ys.append(_conv_fwd(proj, w8, name=t + "conv"))
        yn = _norm_fwd(ys, row(wts["out_norm_w"][l]), group=HEAD_DIM, name=t + "out_norm")
        x_mid = _mm([(yn, wts["w_out"][l])], nt=False, res=x, out_dtype=F32, name=t + "out_proj")
        h2 = _norm_fwd([x_mid], row(wts["ffn_norm_w"][l]), group=D_MODEL, name=t + "ffn_norm")
        g, u, a = _ffn_up(h2, wts["w_gate"][l], wts["w_up"][l], name=t + "ffn_up")
        x_out = _mm([(a, wts["w_down"][l])], nt=False, res=x_mid, out_dtype=F32, name=t + "ffn_down")
        saved.append(dict(x=x, h=h, proj=proj, sb_carries=sb_carries, qw2=qw2, kw2=kw2, bias=bias, w8=w8, ys=ys, yn=yn, x_mid=x_mid,
                          h2=h2, g=g, u=u, a=a))
        x = x_out

    dx, sq = _loss_head(x, target, name="loss_head")
    grads = {k: [None] * depth for k in
             ("attn_norm_w", "w_in", "q_norm_w", "k_norm_w", "rel_bias", "conv_w", "out_norm_w", "w_out",
              "ffn_norm_w", "w_gate", "w_up", "w_down")}
    for l in reversed(range(depth)):
        t = f"l{l}_b_"
        sv = saved[l]
        grads["w_down"][l] = _mm_tn(sv["a"], dx, name=t + "w_down")
        dg, du = _ffn_dact(dx, wts["w_down"][l], sv["g"], sv["u"], name=t + "ffn_dact")
        grads["w_gate"][l] = _mm_tn(sv["h2"], dg, name=t + "w_gate")
        grads["w_up"][l] = _mm_tn(sv["h2"], du, name=t + "w_up")
        dh2 = _mm([(dg, wts["w_gate"][l]), (du, wts["w_up"][l])], nt=True, name=t + "ffn_dh")
        dx_mid, dw8 = _norm_bwd([sv["x_mid"]], row(wts["ffn_norm_w"][l]), dh2, group=D_MODEL, res=dx,
                                name=t + "ffn_norm")
        grads["ffn_norm_w"][l] = dw8.sum(axis=0)
        grads["w_out"][l] = _mm_tn(sv["yn"], dx_mid, name=t + "w_out")
        dyn = _mm([(dx_mid, wts["w_out"][l])], nt=True, name=t + "out_dy")
        dy, dw8 = _norm_bwd(sv["ys"], row(wts["out_norm_w"][l]), dyn, group=HEAD_DIM, out_dtype=BF16,
                            name=t + "out_norm")
        grads["out_norm_w"][l] = dw8.sum(axis=0)
        proj = sv["proj"]
        n_sb, n_ch = D_SB // LANES, D_CH // LANES
        sb = [_sb_bwd(proj, sv["sb_carries"][hp], dy, hp, name=t + f"sb{hp}") for hp in range(n_sb)]
        ch = [_ch_bwd(proj, dy, sv["qw2"], sv["kw2"], sv["bias"], hp, name=t + f"ch{hp}") for hp in range(n_ch)]
        dgb, dgc, dxc, dcw = _conv_bwd(proj, dy, sv["w8"], name=t + "conv")
        grads["conv_w"][l] = dcw[:3]
        grads["q_norm_w"][l] = _fold_heads(jnp.stack([c[4] for c in ch]))
        grads["k_norm_w"][l] = _fold_heads(jnp.stack([c[5] for c in ch]))
        grads["rel_bias"][l] = _bias_fold(jnp.concatenate([c[3] for c in ch], axis=0), name=t + "bias")
        pieces = ([o[0] for o in sb] + [o[1] for o in sb] + [o[2] for o in sb]
                  + [o[0] for o in ch] + [o[1] for o in ch] + [o[2] for o in ch] + [dgb, dgc, dxc])
        dproj = jnp.concatenate([p.astype(BF16) for p in pieces], axis=1)
        grads["w_in"][l] = _mm_tn(sv["h"], dproj, name=t + "w_in")
        dh = _mm([(dproj, wts["w_in"][l])], nt=True, name=t + "proj_dh")
        dx, dw8 = _norm_bwd([sv["x"]], row(wts["attn_norm_w"][l]), dh, group=D_MODEL, res=dx_mid,
                            name=t + "attn_norm")
        grads["attn_norm_w"][l] = dw8.sum(axis=0)
    return sq, dx, {k: jnp.stack(v) for k, v in grads.items()}


SMALL = ("attn_norm_w", "q_norm_w", "k_norm_w", "rel_bias", "conv_w", "out_norm_w", "ffn_norm_w")
ORDER = ("attn_norm_w", "w_in", "q_norm_w", "k_norm_w", "rel_bias", "conv_w", "out_norm_w", "w_out",
         "ffn_norm_w", "w_gate", "w_up", "w_down")


def kernel(x, attn_norm_w, w_in, q_norm_w, k_norm_w, rel_bias, conv_w, out_norm_w, w_out, ffn_norm_w, w_gate, w_up, w_down, loss_target, m_attn_norm_w, m_w_in, m_q_norm_w, m_k_norm_w, m_rel_bias, m_conv_w, m_out_norm_w, m_w_out, m_ffn_norm_w, m_w_gate, m_w_up, m_w_down, v_attn_norm_w, v_w_in, v_q_norm_w, v_k_norm_w, v_rel_bias, v_conv_w, v_out_norm_w, v_w_out, v_ffn_norm_w, v_w_gate, v_w_up, v_w_down):
    w = dict(attn_norm_w=attn_norm_w, w_in=w_in, q_norm_w=q_norm_w, k_norm_w=k_norm_w, rel_bias=rel_bias,
             conv_w=conv_w, out_norm_w=out_norm_w, w_out=w_out, ffn_norm_w=ffn_norm_w, w_gate=w_gate, w_up=w_up,
             w_down=w_down)
    mom = dict(attn_norm_w=m_attn_norm_w, w_in=m_w_in, q_norm_w=m_q_norm_w, k_norm_w=m_k_norm_w, rel_bias=m_rel_bias,
               conv_w=m_conv_w, out_norm_w=m_out_norm_w, w_out=m_w_out, ffn_norm_w=m_ffn_norm_w, w_gate=m_w_gate,
               w_up=m_w_up, w_down=m_w_down)
    var = dict(attn_norm_w=v_attn_norm_w, w_in=v_w_in, q_norm_w=v_q_norm_w, k_norm_w=v_k_norm_w, rel_bias=v_rel_bias,
               conv_w=v_conv_w, out_norm_w=v_out_norm_w, w_out=v_w_out, ffn_norm_w=v_ffn_norm_w, w_gate=v_w_gate,
               w_up=v_w_up, w_down=v_w_down)
    depth = w_in.shape[0]
    seq = x.shape[1]
    me = 4 * lax.axis_index("x") + 2 * lax.axis_index("y") + lax.axis_index("c")
    flat2 = lambda a: a.reshape(-1, a.shape[-1])

    full = {k: w[k] for k in SMALL if k != "conv_w"}
    for k in ("w_in", "w_gate", "w_up"):
        full[k] = _cols_from_dev(_all_gather(flat2(w[k]).astype(BF16), name="gather_" + k), depth)
    for k in ("w_out", "w_down"):
        full[k] = _rows_from_dev(_all_gather(flat2(w[k]).astype(BF16), name="gather_" + k), depth)
    conv_shard = conv_w.shape[2]
    conv_all = _all_gather(_pack([conv_w]), name="gather_conv_w", space=pltpu.VMEM)
    conv_all = conv_all.reshape(N_DEV, -1)[:, :depth * 3 * conv_shard].reshape(N_DEV, depth, 3, conv_shard)
    full["conv_w"] = conv_all.transpose(1, 2, 0, 3).reshape(depth, 3, N_DEV * conv_shard)

    sq, dx, grads = _forward_backward(x.reshape(seq, D_MODEL), loss_target.reshape(seq, D_MODEL), full)
    loss = lax.psum(0.5 / D_MODEL * jnp.sum(sq), MESH_AXES)

    outs = {}
    for k in ("w_in", "w_gate", "w_up", "w_out", "w_down"):
        slabs = _cols_to_dev(grads[k]) if k in ("w_in", "w_gate", "w_up") else _rows_to_dev(grads[k])
        parts = _all_to_all(slabs.astype(BF16), name="scatter_" + k)
        res = _adamw(parts, flat2(w[k]), flat2(mom[k]), flat2(var[k]), name="adamw_" + k)
        outs[k] = [r.reshape(w[k].shape) for r in res]

    shapes = [grads[k].shape for k in SMALL]
    col0 = me * conv_shard

    def widen(a):
        return lax.dynamic_update_slice(jnp.zeros((depth, 3, N_DEV * conv_shard), F32), a, (0, 0, col0))

    parts = _all_gather(_pack([grads[k] for k in SMALL]), name="gather_small_grads", space=pltpu.VMEM)
    pw = _pack([widen(w[k]) if k == "conv_w" else w[k] for k in SMALL])
    pm = _pack([widen(mom[k]) if k == "conv_w" else mom[k] for k in SMALL])
    pv = _pack([widen(var[k]) if k == "conv_w" else var[k] for k in SMALL])
    res = [_unpack(r, shapes) for r in _adamw(parts, pw, pm, pv, name="adamw_small")]
    for idx, k in enumerate(SMALL):
        vals = [r[idx] for r in res]
        if k == "conv_w":
            vals = [lax.dynamic_slice(a, (0, 0, col0), (depth, 3, conv_shard)) for a in vals]
        outs[k] = vals

    result = [loss, dx.reshape(x.shape)]
    for part in range(4):
        result += [outs[k][part] for k in ORDER]
    return tuple(result)
```

```python
import numpy as np
import jax
import jax.numpy as jnp
from jax import lax
from jax.experimental import pallas as pl
from jax.experimental.pallas import tpu as pltpu

F32 = jnp.float32
BF16 = jnp.bfloat16

N_DEV = 8
D_MODEL = 1024
HEAD_DIM = 64
LANES = 128
SB_BLOCK = 128
SB_GROUP = 8
SB_HALF = SB_GROUP // 2
SB_LANE_LOWER, SB_LANE_UPPER = 127, 126
CH_BLOCK = 128
CH_STEP = 2
CH_PAD = 512
CH_BAND = 640
REL_CLIP = 128
N_REL = 2 * REL_CLIP + 1
REL_PAD = 384
D_SB, D_CH, D_CV = 256, 512, 256
COL_QA, COL_KA, COL_VA = 0, 256, 512
COL_QB, COL_KB, COL_VB = 768, 1280, 1792
COL_GB, COL_GC, COL_XC = 2304, 2560, 2816
EPS = 1e-6
NEG = -1e30
SCALE = HEAD_DIM ** -0.5

ADAM_LR, ADAM_B1, ADAM_B2, ADAM_EPS, ADAM_WD, ADAM_STEP = 0.001, 0.9, 0.999, 1e-08, 0.01, 10

VMEM_LIMIT_MIB = 56
MM_WEIGHT_TILE_BYTES = 8 * 2 ** 20
MESH_AXES = ("x", "y", "c")


def _cparams(sem=None):
    return pltpu.CompilerParams(dimension_semantics=sem, vmem_limit_bytes=VMEM_LIMIT_MIB * 2 ** 20)


def _dot(a, b):
    return lax.dot_general(a, b, (((1,), (0,)), ((), ())), preferred_element_type=F32)


def _dot_nt(a, b):
    return lax.dot_general(a, b, (((1,), (1,)), ((), ())), preferred_element_type=F32)


def _dot_tn(a, b):
    return lax.dot_general(a, b, (((0,), (0,)), ((), ())), preferred_element_type=F32)


def _tile(n, cap):
    if n <= cap:
        return n
    best = None
    for t in range(128, cap + 1, 128):
        if n % t == 0:
            best = t
    assert best is not None, (n, cap)
    return best


def _cat(vals):
    return vals[0] if len(vals) == 1 else jnp.concatenate(vals, axis=1)


def _mm(pairs, *, nt, name, res=None, out_dtype=BF16):
    m = pairs[0][0].shape[0]
    n = pairs[0][1].shape[0] if nt else pairs[0][1].shape[1]
    tm = min(512, m)
    tn = n
    while sum(a.shape[1] for a, _ in pairs) * tn * 2 > MM_WEIGHT_TILE_BYTES and tn % 256 == 0:
        tn //= 2
    npairs = len(pairs)

    def body(*refs):
        acc = None
        for p in range(npairs):
            a = refs[2 * p][...].astype(BF16)
            b = refs[2 * p + 1][...]
            d = _dot_nt(a, b) if nt else _dot(a, b)
            acc = d if acc is None else acc + d
        if res is not None:
            acc = acc + refs[2 * npairs][...]
        refs[-1][...] = acc.astype(refs[-1].dtype)

    in_specs, args = [], []
    for a, b in pairs:
        k = a.shape[1]
        in_specs.append(pl.BlockSpec((tm, k), lambda j, i: (i, 0)))
        if nt:
            in_specs.append(pl.BlockSpec((tn, k), lambda j, i: (j, 0)))
        else:
            in_specs.append(pl.BlockSpec((k, tn), lambda j, i: (0, j)))
        args += [a, b]
    if res is not None:
        in_specs.append(pl.BlockSpec((tm, tn), lambda j, i: (i, j)))
        args.append(res)
    return pl.pallas_call(
        body, grid=(n // tn, m // tm), in_specs=in_specs,
        out_specs=pl.BlockSpec((tm, tn), lambda j, i: (i, j)),
        out_shape=jax.ShapeDtypeStruct((m, n), out_dtype),
        compiler_params=_cparams(("parallel", "parallel")), name=name)(*args)


def _mm_tn(a, g, *, name):
    s, k = a.shape
    n = g.shape[1]
    tk, tn, ts = _tile(k, 1408), _tile(n, 1536), min(512, s)

    def body(a_ref, g_ref, o_ref):
        @pl.when(pl.program_id(2) == 0)
        def _():
            o_ref[...] = jnp.zeros_like(o_ref)
        o_ref[...] += _dot_tn(a_ref[...].astype(BF16), g_ref[...].astype(BF16))

    return pl.pallas_call(
        body, grid=(k // tk, n // tn, s // ts),
        in_specs=[pl.BlockSpec((ts, tk), lambda i, j, t: (t, i)),
                  pl.BlockSpec((ts, tn), lambda i, j, t: (t, j))],
        out_specs=pl.BlockSpec((tk, tn), lambda i, j, t: (i, j)),
        out_shape=jax.ShapeDtypeStruct((k, n), F32),
        compiler_params=_cparams(("parallel", "parallel", "arbitrary")), name=name)(a, g)


def _group_indicator(width, group):
    assert width // group <= LANES
    r = lax.broadcasted_iota(jnp.int32, (width, LANES), 0) // group
    c = lax.broadcasted_iota(jnp.int32, (width, LANES), 1)
    return (r == c).astype(BF16)


def _split_bf16(v):
    hi = v.astype(BF16)
    return hi, (v - hi.astype(F32)).astype(BF16)


def _group_mean(v, ind, group):
    hi, lo = _split_bf16(v)
    hi, lo = _split_bf16((_dot(hi, ind) + _dot(lo, ind)) * (1.0 / group))
    return _dot_nt(hi, ind) + _dot_nt(lo, ind)


def _norm_fwd(xs, w, *, group, name):
    s = xs[0].shape[0]
    width = sum(x.shape[1] for x in xs)
    tm = min(512, s)
    nx = len(xs)
    grouped = group != width

    def body(*refs):
        x = _cat([r[...].astype(F32) for r in refs[:nx]])
        if grouped:
            ms = _group_mean(x * x, refs[nx + 1][...], group)
        else:
            ms = jnp.mean(x * x, axis=1, keepdims=True)
        refs[-1][...] = (x * lax.rsqrt(ms + EPS) * refs[nx][...]).astype(BF16)

    in_specs = [pl.BlockSpec((tm, x.shape[1]), lambda i: (i, 0)) for x in xs]
    in_specs.append(pl.BlockSpec((1, width), lambda i: (0, 0)))
    args = list(xs) + [w]
    if grouped:
        in_specs.append(pl.BlockSpec((width, LANES), lambda i: (0, 0)))
        args.append(_group_indicator(width, group))
    return pl.pallas_call(
        body, grid=(s // tm,), in_specs=in_specs,
        out_specs=pl.BlockSpec((tm, width), lambda i: (i, 0)),
        out_shape=jax.ShapeDtypeStruct((s, width), BF16),
        compiler_params=_cparams(("parallel",)), name=name)(*args)


def _norm_bwd(xs, w, dy, *, group, name, res=None, out_dtype=F32):
    s = xs[0].shape[0]
    width = sum(x.shape[1] for x in xs)
    tm = min(512, s)
    nx = len(xs)
    grouped = group != width
    has_res = res is not None

    def body(*refs):
        x = _cat([r[...].astype(F32) for r in refs[:nx]])
        w_ref, dy_ref = refs[nx], refs[nx + 1]
        pos = nx + 2
        if has_res:
            res_ref = refs[pos]
            pos += 1
        if grouped:
            ind = refs[pos][...]
            mean = lambda v: _group_mean(v, ind, group)
        else:
            mean = lambda v: jnp.mean(v, axis=1, keepdims=True)
        dx_ref, dw_ref = refs[-2], refs[-1]
        inv = lax.rsqrt(mean(x * x) + EPS)
        xh = x * inv
        d = dy_ref[...].astype(F32)
        g = d * w_ref[...]
        dx = inv * (g - xh * mean(g * xh))
        if has_res:
            dx = dx + res_ref[...]
        dx_ref[...] = dx.astype(dx_ref.dtype)

        @pl.when(pl.program_id(0) == 0)
        def _():
            dw_ref[...] = jnp.zeros_like(dw_ref)
        dw_ref[...] += (d * xh).reshape(tm // 8, 8, width).sum(axis=0)

    in_specs = [pl.BlockSpec((tm, x.shape[1]), lambda i: (i, 0)) for x in xs]
    in_specs += [pl.BlockSpec((1, width), lambda i: (0, 0)), pl.BlockSpec((tm, width), lambda i: (i, 0))]
    args = list(xs) + [w, dy]
    if has_res:
        in_specs.append(pl.BlockSpec((tm, width), lambda i: (i, 0)))
        args.append(res)
    if grouped:
        in_specs.append(pl.BlockSpec((width, LANES), lambda i: (0, 0)))
        args.append(_group_indicator(width, group))
    return pl.pallas_call(
        body, grid=(s // tm,), in_specs=in_specs,
        out_specs=(pl.BlockSpec((tm, width), lambda i: (i, 0)), pl.BlockSpec((8, width), lambda i: (0, 0))),
        out_shape=(jax.ShapeDtypeStruct((s, width), out_dtype), jax.ShapeDtypeStruct((8, width), F32)),
        compiler_params=_cparams(("arbitrary",)), name=name)(*args)


def _head0_lanes(shape):
    return lax.broadcasted_iota(jnp.int32, shape, len(shape) - 1) < HEAD_DIM


def _split_heads(v):
    m0 = _head0_lanes(v.shape)
    zero = jnp.zeros_like(v)
    return jnp.where(m0, v, zero), jnp.where(m0, zero, v)


def _pair_sum(v):
    m0 = _head0_lanes(v.shape)
    s0 = jnp.sum(jnp.where(m0, v, 0.0), axis=1, keepdims=True)
    s1 = jnp.sum(jnp.where(m0, 0.0, v), axis=1, keepdims=True)
    return jnp.where(m0, s0, s1)


def _pair_norm(x, w2):
    inv = lax.rsqrt(_pair_sum(x * x) * (1.0 / HEAD_DIM) + EPS)
    xh = x * inv
    return xh * w2, xh, inv


def _pair_norm_bwd(dy, xh, inv, w2):
    g = dy * w2
    return inv * (g - xh * (_pair_sum(g * xh) * (1.0 / HEAD_DIM)))


def _copy_in(copies):
    for c in copies:
        c.start()
    for c in copies:
        c.wait()


def _stack(vals):
    return jnp.concatenate(list(vals), axis=0)


def _unstack(results, chains):
    blocks = []
    for res in results:
        blocks += [res[n * SB_BLOCK:(n + 1) * SB_BLOCK] for n in range(res.shape[0] // SB_BLOCK)]
    return dict(zip(chains, blocks))


def _log_rest(nz):
    return jnp.minimum(nz, 0.0) - jnp.log(1.0 + jnp.exp(-jnp.abs(nz)))


def _sb_fwd(proj, hp, *, name):
    s = proj.shape[0]
    nq = s // SB_BLOCK
    assert nq % SB_GROUP == 0 and nq // SB_GROUP <= SB_LANE_UPPER
    shape = (SB_BLOCK, SB_BLOCK)

    def body(q_ref, proj_hbm, y_ref, c0_ref, c1_ref, k_scr, v_scr, sems):
        i = pl.program_id(0)

        @pl.when(i == 0)
        def _():
            _copy_in([pltpu.make_async_copy(proj_hbm.at[:, pl.ds(COL_KA + hp * LANES, LANES)], k_scr, sems.at[0]),
                      pltpu.make_async_copy(proj_hbm.at[:, pl.ds(COL_VA + hp * LANES, LANES)], v_scr, sems.at[1])])

        row = lax.broadcasted_iota(jnp.int32, shape, 0)
        col = lax.broadcasted_iota(jnp.int32, shape, 1)
        m0 = col < HEAD_DIM
        later_keys = (row > col).astype(BF16)
        q_neg = _stack(_split_heads(q_ref[...] * (-SCALE)))

        def blocks_of(first, n, carry, masked, lane):
            c, acc, seen = list(carry[:2]), carry[2], list(carry[3:5])
            chains = [(b, h) for b in range(n) for h in range(2)]
            js = [first + b for b in reversed(range(n))]
            rows = [pl.ds(pl.multiple_of(j * SB_BLOCK, SB_BLOCK), SB_BLOCK) for j in js]
            earlier = [(col + j * SB_BLOCK < row + i * SB_BLOCK) if masked else None for j in js]
            nz = _unstack([_dot_nt(q_neg, k_scr[r, :]) for r in rows], chains)
            lr = {ch: _log_rest(nz[ch]) for ch in chains}
            lrm = {(b, h): jnp.where(earlier[b], lr[b, h], 0.0) if masked else lr[b, h] for b, h in chains}
            after = _unstack([_dot(_stack([lrm[ch].astype(BF16) for ch in chains]), later_keys)], chains)
            seen = [jnp.where(col == lane, c[h], seen[h]) for h in range(2)]
            w = {}
            for b, h in chains:
                wv = jnp.exp((lr[b, h] - nz[b, h]) + (after[b, h] + c[h]))
                w[b, h] = jnp.where(earlier[b], wv, 0.0) if masked else wv
                c[h] = c[h] + jnp.sum(lrm[b, h], axis=1, keepdims=True)
            pv = _unstack([_dot(_stack([w[b, 0].astype(BF16), w[b, 1].astype(BF16)]), v_scr[rows[b], :])
                           for b in range(n)], chains)
            for b in range(n):
                acc = acc + jnp.where(m0, pv[b, 0], pv[b, 1])
            return c[0], c[1], acc, seen[0], seen[1]

        zc = jnp.zeros((SB_BLOCK, 1), F32)
        zt = jnp.zeros(shape, F32)
        gi = i // SB_GROUP
        carry = lax.cond(i % SB_GROUP >= SB_HALF,
                         lambda cr: blocks_of(gi * SB_GROUP + SB_HALF, SB_HALF, cr, True, SB_LANE_UPPER),
                         lambda cr: cr, (zc, zc, zt, zt, zt))
        carry = blocks_of(gi * SB_GROUP, SB_HALF, carry, True, SB_LANE_LOWER)
        carry = lax.fori_loop(
            0, gi, lambda t, cr: blocks_of((gi - 1 - t) * SB_GROUP, SB_GROUP, cr, False, gi - 1 - t), carry)
        y_ref[...] = carry[2].astype(y_ref.dtype)
        c0_ref[...] = carry[3]
        c1_ref[...] = carry[4]

    return pl.pallas_call(
        body, grid=(nq,),
        in_specs=[pl.BlockSpec(shape, lambda i: (i, COL_QA // LANES + hp)), pl.BlockSpec(memory_space=pl.ANY)],
        out_specs=(pl.BlockSpec(shape, lambda i: (i, 0)),) * 3,
        out_shape=(jax.ShapeDtypeStruct((s, LANES), BF16), jax.ShapeDtypeStruct((s, LANES), F32),
                   jax.ShapeDtypeStruct((s, LANES), F32)),
        scratch_shapes=[pltpu.VMEM((s, LANES), BF16), pltpu.VMEM((s, LANES), BF16), pltpu.SemaphoreType.DMA((2,))],
        compiler_params=_cparams(("arbitrary",)), name=name)(proj, proj)


def _sb_bwd(proj, carries, dy_all, hp, *, name):
    s = proj.shape[0]
    nq = s // SB_BLOCK
    shape = (SB_BLOCK, SB_BLOCK)

    def body(q_ref, c0_ref, c1_ref, dy_ref, proj_hbm, dq_ref, dk_hbm, dv_hbm, k_scr, v_scr, dk_scr, dv_scr, sems):
        i = pl.program_id(0)

        @pl.when(i == 0)
        def _():
            _copy_in([pltpu.make_async_copy(proj_hbm.at[:, pl.ds(COL_KA + hp * LANES, LANES)], k_scr, sems.at[0]),
                      pltpu.make_async_copy(proj_hbm.at[:, pl.ds(COL_VA + hp * LANES, LANES)], v_scr, sems.at[1])])
            dk_scr[...] = jnp.zeros_like(dk_scr)
            dv_scr[...] = jnp.zeros_like(dv_scr)

        row = lax.broadcasted_iota(jnp.int32, shape, 0)
        col = lax.broadcasted_iota(jnp.int32, shape, 1)
        m0 = col < HEAD_DIM
        later_keys = (row > col).astype(BF16)
        earlier_keys = (row < col).astype(BF16)
        q_both = _stack(_split_heads(q_ref[...] * SCALE))
        q_neg = _stack(_split_heads(q_ref[...] * (-SCALE)))
        dy_both = _stack(_split_heads(dy_ref[...]))
        seen = (c0_ref[...], c1_ref[...])

        def blocks_of(first, n, carry, masked, lane):
            r, dq = list(carry[0:2]), list(carry[2:4])
            blocks = range(n)
            chains = [(b, h) for b in blocks for h in range(2)]
            js = [first + b for b in blocks]
            rows = [pl.ds(pl.multiple_of(j * SB_BLOCK, SB_BLOCK), SB_BLOCK) for j in js]
            earlier = [(col + j * SB_BLOCK < row + i * SB_BLOCK) if masked else None for j in js]
            u = _unstack([_dot_nt(q_neg, k_scr[rw, :]) for rw in rows], chains)
            dw = _unstack([_dot_nt(dy_both, v_scr[rw, :]) for rw in rows], chains)
            lr = {ch: _log_rest(u[ch]) for ch in chains}
            lrm = {(b, h): jnp.where(earlier[b], lr[b, h], 0.0) if masked else lr[b, h] for b, h in chains}
            after = _unstack([_dot(_stack([lrm[ch].astype(BF16) for ch in chains]), later_keys)], chains)
            c = {}
            for h in range(2):
                ch = jnp.sum(jnp.where(col == lane, seen[h], 0.0), axis=1, keepdims=True)
                for b in reversed(blocks):
                    c[b, h] = ch
                    ch = ch + jnp.sum(lrm[b, h], axis=1, keepdims=True)
            w, de, lb = {}, {}, {}
            for b, h in chains:
                lb[b, h] = lr[b, h] - u[b, h]
                wv = jnp.exp(lb[b, h] + (after[b, h] + c[b, h]))
                w[b, h] = jnp.where(earlier[b], wv, 0.0) if masked else wv
                de[b, h] = dw[b, h] * w[b, h]
            before = _unstack([_dot(_stack([de[ch].astype(BF16) for ch in chains]), earlier_keys)], chains)
            dzb = {}
            for b, h in chains:
                dz = de[b, h] - jnp.exp(lb[b, h]) * (de[b, h] + (r[h] + before[b, h]))
                if masked:
                    dz = jnp.where(earlier[b], dz, 0.0)
                dzb[b, h] = dz.astype(BF16)
                r[h] = r[h] + jnp.sum(de[b, h], axis=1, keepdims=True)
            keys = pl.ds(pl.multiple_of(first * SB_BLOCK, SB_BLOCK), n * SB_BLOCK)
            dz_wide = [jnp.concatenate([dzb[b, h] for b in blocks], axis=1) for h in range(2)]
            w_wide = [jnp.concatenate([w[b, h].astype(BF16) for b in blocks], axis=1) for h in range(2)]
            dqc = _dot(_stack(dz_wide), k_scr[keys, :])
            dq = [dq[0] + dqc[:SB_BLOCK], dq[1] + dqc[SB_BLOCK:]]
            dk_scr[keys, :] += _dot_tn(_stack(dz_wide), q_both)
            dv_scr[keys, :] += _dot_tn(_stack(w_wide), dy_both)
            return r[0], r[1], dq[0], dq[1]

        zc = jnp.zeros((SB_BLOCK, 1), F32)
        zq = jnp.zeros(shape, F32)
        gi = i // SB_GROUP
        carry = lax.fori_loop(0, gi, lambda t, cr: blocks_of(t * SB_GROUP, SB_GROUP, cr, False, t), (zc, zc, zq, zq))
        carry = blocks_of(gi * SB_GROUP, SB_HALF, carry, True, SB_LANE_LOWER)
        carry = lax.cond(i % SB_GROUP >= SB_HALF,
                         lambda cr: blocks_of(gi * SB_GROUP + SB_HALF, SB_HALF, cr, True, SB_LANE_UPPER),
                         lambda cr: cr, carry)
        dq_ref[...] = (jnp.where(m0, carry[2], carry[3]) * SCALE).astype(dq_ref.dtype)

        @pl.when(i == nq - 1)
        def _():
            _copy_in([pltpu.make_async_copy(dk_scr, dk_hbm, sems.at[0]),
                      pltpu.make_async_copy(dv_scr, dv_hbm, sems.at[1])])

    blk = lambda c0: pl.BlockSpec(shape, lambda i: (i, c0))
    return pl.pallas_call(
        body, grid=(nq,),
        in_specs=[blk(COL_QA // LANES + hp), blk(0), blk(0), blk(hp), pl.BlockSpec(memory_space=pl.ANY)],
        out_specs=(blk(0), pl.BlockSpec(memory_space=pl.ANY), pl.BlockSpec(memory_space=pl.ANY)),
        out_shape=(jax.ShapeDtypeStruct((s, LANES), BF16), jax.ShapeDtypeStruct((s, LANES), F32),
                   jax.ShapeDtypeStruct((s, LANES), F32)),
        scratch_shapes=[pltpu.VMEM((s, LANES), BF16), pltpu.VMEM((s, LANES), BF16),
                        pltpu.VMEM((s, LANES), F32), pltpu.VMEM((s, LANES), F32), pltpu.SemaphoreType.DMA((2,))],
        compiler_params=_cparams(("arbitrary",)), name=name)(proj, carries[0], carries[1], dy_all, proj)


CH_DIAG = CH_BAND + CH_BLOCK


def _diag_index():
    x = np.arange(CH_DIAG)[None, :]
    return (np.clip(CH_PAD + (CH_BLOCK - 1) - x, -REL_CLIP, REL_CLIP) + REL_CLIP).astype(np.int32)


def _diag_onehot(idx_ref):
    rid = lax.broadcasted_iota(jnp.int32, (REL_PAD, CH_DIAG), 0)
    return (rid == idx_ref[...]).astype(F32)


def _bias_expand(rel_bias, *, name):
    nh = rel_bias.shape[0]
    table = jnp.pad(rel_bias, ((0, 0), (0, REL_PAD - N_REL)))

    def body(tab_ref, idx_ref, out_ref):
        diag = lax.dot_general(tab_ref[...], _diag_onehot(idx_ref), (((1,), (0,)), ((), ())),
                               precision=lax.Precision.HIGHEST, preferred_element_type=F32)

        def step(p, _):
            out_ref[p] = pltpu.roll(diag, (CH_DIAG - (CH_BLOCK - 1) + p) % CH_DIAG, 1)[:, :CH_BAND]
            return 0

        lax.fori_loop(0, CH_BLOCK, step, 0)

    out = pl.pallas_call(
        body, out_shape=jax.ShapeDtypeStruct((CH_BLOCK, nh, CH_BAND), F32),
        compiler_params=_cparams(), name=name)(table, jnp.asarray(_diag_index()))
    return out.transpose(1, 0, 2)


def _bias_fold(dbias, *, name):
    nh = dbias.shape[0]

    def body(db_ref, idx_ref, out_ref):
        def step(p, acc):
            wide = jnp.concatenate([db_ref[p], jnp.zeros((nh, CH_BLOCK), F32)], axis=1)
            return acc + pltpu.roll(wide, CH_BLOCK - 1 - p, 1)

        diag = lax.fori_loop(0, CH_BLOCK, step, jnp.zeros((nh, CH_DIAG), F32))
        out_ref[...] = lax.dot_general(diag, _diag_onehot(idx_ref), (((1,), (1,)), ((), ())),
                                       precision=lax.Precision.HIGHEST, preferred_element_type=F32)

    out = pl.pallas_call(
        body, out_shape=jax.ShapeDtypeStruct((nh, REL_PAD), F32),
        compiler_params=_cparams(), name=name)(dbias.transpose(1, 0, 2), jnp.asarray(_diag_index()))
    return out[:, :N_REL]


def _band_valid(i):
    row = lax.broadcasted_iota(jnp.int32, (CH_BLOCK, CH_BAND), 0)
    col = lax.broadcasted_iota(jnp.int32, (CH_BLOCK, CH_BAND), 1)
    first = row < CH_BLOCK // 2
    window = (first & (col < CH_BAND - 64)) | (jnp.logical_not(first) & (col >= 64))
    return window & (col + i * CH_BLOCK >= CH_PAD)


def _ch_load(proj_hbm, hp, s, kw_ref, kraw_scr, kn_scr, vp_scr, sems):
    _copy_in([pltpu.make_async_copy(proj_hbm.at[:, pl.ds(COL_KB + hp * LANES, LANES)], kraw_scr, sems.at[0]),
              pltpu.make_async_copy(proj_hbm.at[:, pl.ds(COL_VB + hp * LANES, LANES)],
                                    vp_scr.at[pl.ds(CH_PAD, s), :], sems.at[1])])
    kn_scr[0:CH_PAD, :] = jnp.zeros((CH_PAD, LANES), BF16)
    vp_scr[0:CH_PAD, :] = jnp.zeros((CH_PAD, LANES), BF16)
    rb = min(512, s)

    def step(t, _):
        r = pl.multiple_of(t * rb, rb)
        kn = _pair_norm(kraw_scr[pl.ds(r, rb), :].astype(F32), kw_ref[...])[0]
        kn_scr[pl.ds(CH_PAD + r, rb), :] = kn.astype(BF16)
        return 0

    lax.fori_loop(0, s // rb, step, 0)


def _ch_probs(qk, bias, valid):
    sc = jnp.where(valid, qk * SCALE + bias, NEG)
    p = jnp.exp(sc - jnp.max(sc, axis=1, keepdims=True))
    return p, 1.0 / jnp.sum(p, axis=1, keepdims=True)


def _ch_fwd(proj, qw2, kw2, bias, hp, *, name):
    s = proj.shape[0]
    nb = s // (CH_BLOCK * CH_STEP)
    shape = (CH_BLOCK * CH_STEP, LANES)
    subs = range(CH_STEP)

    def body(q_ref, proj_hbm, qw_ref, kw_ref, bias_ref, y_ref, kraw_scr, kn_scr, vp_scr, sems):
        i = pl.program_id(0)

        @pl.when(i == 0)
        def _():
            _ch_load(proj_hbm, hp, s, kw_ref, kraw_scr, kn_scr, vp_scr, sems)

        blk = [i * CH_STEP + t for t in subs]
        part = [slice(t * CH_BLOCK, (t + 1) * CH_BLOCK) for t in subs]
        bands = [pl.ds(pl.multiple_of(b * CH_BLOCK, CH_BLOCK), CH_BAND) for b in blk]
        valid = [_band_valid(b) for b in blk]
        qn = _pair_norm(q_ref[...].astype(F32), qw_ref[...])[0].astype(BF16)
        scores = [_dot_nt(_stack(_split_heads(qn[part[t]])), kn_scr[bands[t], :]) for t in subs]
        probs = {(t, h): _ch_probs(scores[t][h * CH_BLOCK:(h + 1) * CH_BLOCK], bias_ref[h], valid[t])
                 for t in subs for h in range(2)}
        outs = [_dot(_stack([probs[t, h][0].astype(BF16) for h in range(2)]), vp_scr[bands[t], :]) for t in subs]
        for t in subs:
            o = [outs[t][h * CH_BLOCK:(h + 1) * CH_BLOCK] * probs[t, h][1] for h in range(2)]
            y_ref[part[t], :] = jnp.where(_head0_lanes((CH_BLOCK, LANES)), o[0], o[1]).astype(y_ref.dtype)

    return pl.pallas_call(
        body, grid=(nb,),
        in_specs=[pl.BlockSpec(shape, lambda i: (i, COL_QB // LANES + hp)), pl.BlockSpec(memory_space=pl.ANY),
                  pl.BlockSpec((1, LANES), lambda i: (0, 0)), pl.BlockSpec((1, LANES), lambda i: (0, 0)),
                  pl.BlockSpec((2, CH_BLOCK, CH_BAND), lambda i: (hp, 0, 0))],
        out_specs=pl.BlockSpec(shape, lambda i: (i, 0)),
        out_shape=jax.ShapeDtypeStruct((s, LANES), BF16),
        scratch_shapes=[pltpu.VMEM((s, LANES), BF16), pltpu.VMEM((s + CH_PAD, LANES), BF16),
                        pltpu.VMEM((s + CH_PAD, LANES), BF16), pltpu.SemaphoreType.DMA((2,))],
        compiler_params=_cparams(("arbitrary",)), name=name)(proj, proj, qw2, kw2, bias)


def _ch_bwd(proj, dy_all, qw2, kw2, bias, hp, *, name):
    s = proj.shape[0]
    nb = s // (CH_BLOCK * CH_STEP)
    shape = (CH_BLOCK * CH_STEP, LANES)
    subs = range(CH_STEP)
    rb = min(512, s)

    def body(q_ref, dy_ref, proj_hbm, qw_ref, kw_ref, bias_ref, dq_ref, dk_hbm, dv_hbm, dbias_ref, dqw_ref, dkw_ref,
             kraw_scr, kn_scr, vp_scr, dkn_scr, dvp_scr, sems):
        i = pl.program_id(0)

        @pl.when(i == 0)
        def _():
            _ch_load(proj_hbm, hp, s, kw_ref, kraw_scr, kn_scr, vp_scr, sems)
            dkn_scr[...] = jnp.zeros_like(dkn_scr)
            dvp_scr[...] = jnp.zeros_like(dvp_scr)
            dbias_ref[...] = jnp.zeros_like(dbias_ref)
            dqw_ref[...] = jnp.zeros_like(dqw_ref)

        blk = [i * CH_STEP + t for t in subs]
        rows = [slice(t * CH_BLOCK, (t + 1) * CH_BLOCK) for t in subs]
        bands = [pl.ds(pl.multiple_of(b * CH_BLOCK, CH_BLOCK), CH_BAND) for b in blk]
        valid = [_band_valid(b) for b in blk]
        qn, qhat, qinv = _pair_norm(q_ref[...].astype(F32), qw_ref[...])
        qn = qn.astype(BF16)
        dy = dy_ref[...]
        q_both = [_stack(_split_heads(qn[rows[t]])) for t in subs]
        dy_both = [_stack(_split_heads(dy[rows[t]])) for t in subs]
        scores = [_dot_nt(q_both[t], kn_scr[bands[t], :]) for t in subs]
        dprobs = [_dot_nt(dy_both[t], vp_scr[bands[t], :]) for t in subs]
        pb, dsb = {}, {}
        for t in subs:
            for h in range(2):
                part = slice(h * CH_BLOCK, (h + 1) * CH_BLOCK)
                p, rl = _ch_probs(scores[t][part], bias_ref[h], valid[t])
                p = p * rl
                dp = dprobs[t][part]
                ds = p * (dp - jnp.sum(dp * p, axis=1, keepdims=True))
                dbias_ref[h] += ds
                pb[t, h] = p.astype(BF16)
                dsb[t, h] = (ds * SCALE).astype(BF16)
        dsb = [_stack([dsb[t, 0], dsb[t, 1]]) for t in subs]
        dqn = [_dot(dsb[t], kn_scr[bands[t], :]) for t in subs]
        dkc = [_dot_tn(dsb[t], q_both[t]) for t in subs]
        dvc = [_dot_tn(_stack([pb[t, 0], pb[t, 1]]), dy_both[t]) for t in subs]
        for t in subs:
            dkn_scr[bands[t], :] += dkc[t]
            dvp_scr[bands[t], :] += dvc[t]
        dqn = _stack([jnp.where(_head0_lanes((CH_BLOCK, LANES)), d[:CH_BLOCK], d[CH_BLOCK:]) for d in dqn])
        dq_ref[...] = _pair_norm_bwd(dqn, qhat, qinv, qw_ref[...]).astype(dq_ref.dtype)
        dqw_ref[...] += (dqn * qhat).reshape(CH_STEP * CH_BLOCK // 8, 8, LANES).sum(axis=0)

        @pl.when(i == nb - 1)
        def _():
            def step(t, acc):
                r = pl.multiple_of(t * rb, rb)
                rows = pl.ds(CH_PAD + r, rb)
                _, khat, kinv = _pair_norm(kraw_scr[pl.ds(r, rb), :].astype(F32), kw_ref[...])
                dkn = dkn_scr[rows, :]
                dkn_scr[rows, :] = _pair_norm_bwd(dkn, khat, kinv, kw_ref[...])
                return acc + (dkn * khat).reshape(rb // 8, 8, LANES).sum(axis=0)

            dkw_ref[...] = lax.fori_loop(0, s // rb, step, jnp.zeros((8, LANES), F32))
            _copy_in([pltpu.make_async_copy(dkn_scr.at[pl.ds(CH_PAD, s), :], dk_hbm, sems.at[0]),
                      pltpu.make_async_copy(dvp_scr.at[pl.ds(CH_PAD, s), :], dv_hbm, sems.at[1])])

    blk = lambda c0: pl.BlockSpec(shape, lambda i: (i, c0))
    vec = pl.BlockSpec((1, LANES), lambda i: (0, 0))
    part = pl.BlockSpec((8, LANES), lambda i: (0, 0))
    hbm = pl.BlockSpec(memory_space=pl.ANY)
    return pl.pallas_call(
        body, grid=(nb,),
        in_specs=[blk(COL_QB // LANES + hp), blk(D_SB // LANES + hp), hbm, vec, vec,
                  pl.BlockSpec((2, CH_BLOCK, CH_BAND), lambda i: (hp, 0, 0))],
        out_specs=(blk(0), hbm, hbm, pl.BlockSpec((2, CH_BLOCK, CH_BAND), lambda i: (0, 0, 0)), part, part),
        out_shape=(jax.ShapeDtypeStruct((s, LANES), BF16), jax.ShapeDtypeStruct((s, LANES), F32),
                   jax.ShapeDtypeStruct((s, LANES), F32), jax.ShapeDtypeStruct((2, CH_BLOCK, CH_BAND), F32),
                   jax.ShapeDtypeStruct((8, LANES), F32), jax.ShapeDtypeStruct((8, LANES), F32)),
        scratch_shapes=[pltpu.VMEM((s, LANES), BF16), pltpu.VMEM((s + CH_PAD, LANES), BF16),
                        pltpu.VMEM((s + CH_PAD, LANES), BF16), pltpu.VMEM((s + CH_PAD, LANES), F32),
                        pltpu.VMEM((s + CH_PAD, LANES), F32), pltpu.SemaphoreType.DMA((2,))],
        compiler_params=_cparams(("arbitrary",)), name=name)(proj, dy_all, proj, qw2, kw2, bias)


def _shift_down(x, halo, k):
    out = pltpu.roll(x, k, 0)
    row = lax.broadcasted_iota(jnp.int32, x.shape, 0)
    for t in range(k):
        out = jnp.where(row == t, halo[8 - k + t:8 - k + t + 1, :], out)
    return out


def _shift_up(x, halo, k):
    n = x.shape[0]
    out = pltpu.roll(x, n - k, 0)
    row = lax.broadcasted_iota(jnp.int32, x.shape, 0)
    for t in range(k):
        out = jnp.where(row == n - k + t, halo[t:t + 1, :], out)
    return out


def _conv_specs(s, tm):
    cw = D_CV
    tile = lambda c0: pl.BlockSpec((tm, cw), lambda i: (i, c0))
    above = lambda c0: pl.BlockSpec((8, cw), lambda i: (jnp.maximum(i * (tm // 8) - 1, 0), c0))
    below = lambda c0: pl.BlockSpec((8, cw), lambda i: (jnp.minimum((i + 1) * (tm // 8), s // 8 - 1), c0))
    return tile, above, below


def _conv_fwd(proj, w8, *, name):
    s = proj.shape[0]
    tm = min(512, s)
    tile, above, _ = _conv_specs(s, tm)

    def body(gb_ref, gc_ref, xc_ref, gca_ref, xca_ref, w_ref, y_ref):
        hc = gc_ref[...].astype(F32) * xc_ref[...].astype(F32)
        top = jnp.where(pl.program_id(0) > 0, gca_ref[...].astype(F32) * xca_ref[...].astype(F32), 0.0)
        u = w_ref[0:1, :] * _shift_down(hc, top, 2) + w_ref[1:2, :] * _shift_down(hc, top, 1) + w_ref[2:3, :] * hc
        y_ref[...] = (gb_ref[...].astype(F32) * u).astype(y_ref.dtype)

    cb, cc, cx = COL_GB // D_CV, COL_GC // D_CV, COL_XC // D_CV
    return pl.pallas_call(
        body, grid=(s // tm,),
        in_specs=[tile(cb), tile(cc), tile(cx), above(cc), above(cx), pl.BlockSpec((8, D_CV), lambda i: (0, 0))],
        out_specs=pl.BlockSpec((tm, D_CV), lambda i: (i, 0)),
        out_shape=jax.ShapeDtypeStruct((s, D_CV), BF16),
        compiler_params=_cparams(("parallel",)), name=name)(proj, proj, proj, proj, proj, w8)


def _conv_bwd(proj, dy_all, w8, *, name):
    s = proj.shape[0]
    tm = min(512, s)
    nt = s // tm
    tile, above, below = _conv_specs(s, tm)

    def body(gb_ref, gc_ref, xc_ref, dy_ref, gca_ref, xca_ref, gbb_ref, dyb_ref, w_ref,
             dgb_ref, dgc_ref, dxc_ref, dw_ref):
        i = pl.program_id(0)
        gb, gc, xc = gb_ref[...].astype(F32), gc_ref[...].astype(F32), xc_ref[...].astype(F32)
        dy = dy_ref[...].astype(F32)
        hc = gc * xc
        top = jnp.where(i > 0, gca_ref[...].astype(F32) * xca_ref[...].astype(F32), 0.0)
        hc1, hc2 = _shift_down(hc, top, 1), _shift_down(hc, top, 2)
        u = w_ref[0:1, :] * hc2 + w_ref[1:2, :] * hc1 + w_ref[2:3, :] * hc
        du = dy * gb
        bottom = jnp.where(i < nt - 1, dyb_ref[...].astype(F32) * gbb_ref[...].astype(F32), 0.0)
        dhc = w_ref[2:3, :] * du + w_ref[1:2, :] * _shift_up(du, bottom, 1) + w_ref[0:1, :] * _shift_up(du, bottom, 2)
        dgb_ref[...] = (dy * u).astype(dgb_ref.dtype)
        dgc_ref[...] = (dhc * xc).astype(dgc_ref.dtype)
        dxc_ref[...] = (dhc * gc).astype(dxc_ref.dtype)

        @pl.when(i == 0)
        def _():
            dw_ref[...] = jnp.zeros_like(dw_ref)
        dw_ref[0:1, :] += jnp.sum(du * hc2, axis=0, keepdims=True)
        dw_ref[1:2, :] += jnp.sum(du * hc1, axis=0, keepdims=True)
        dw_ref[2:3, :] += jnp.sum(du * hc, axis=0, keepdims=True)

    cb, cc, cx = COL_GB // D_CV, COL_GC // D_CV, COL_XC // D_CV
    cy = (D_SB + D_CH) // D_CV
    out_tile = pl.BlockSpec((tm, D_CV), lambda i: (i, 0))
    act = jax.ShapeDtypeStruct((s, D_CV), BF16)
    return pl.pallas_call(
        body, grid=(nt,),
        in_specs=[tile(cb), tile(cc), tile(cx), tile(cy), above(cc), above(cx), below(cb), below(cy),
                  pl.BlockSpec((8, D_CV), lambda i: (0, 0))],
        out_specs=(out_tile, out_tile, out_tile, pl.BlockSpec((8, D_CV), lambda i: (0, 0))),
        out_shape=(act, act, act, jax.ShapeDtypeStruct((8, D_CV), F32)),
        compiler_params=_cparams(("arbitrary",)), name=name)(proj, proj, proj, dy_all, proj, proj, proj, dy_all, w8)


def _ffn_up(h, wg, wu, *, name):
    s, d = h.shape
    f = wg.shape[1]
    tm, tn = min(512, s), _tile(f, 1408)

    def body(h_ref, wg_ref, wu_ref, g_ref, u_ref, a_ref):
        hv = h_ref[...]
        g = _dot(hv, wg_ref[...])
        u = _dot(hv, wu_ref[...])
        g_ref[...] = g.astype(BF16)
        u_ref[...] = u.astype(BF16)
        a_ref[...] = (g * jax.nn.sigmoid(g) * u).astype(BF16)

    wspec = pl.BlockSpec((d, tn), lambda j, i: (0, j))
    ospec = pl.BlockSpec((tm, tn), lambda j, i: (i, j))
    act = jax.ShapeDtypeStruct((s, f), BF16)
    return pl.pallas_call(
        body, grid=(f // tn, s // tm),
        in_specs=[pl.BlockSpec((tm, d), lambda j, i: (i, 0)), wspec, wspec],
        out_specs=(ospec, ospec, ospec), out_shape=(act, act, act),
        compiler_params=_cparams(("parallel", "parallel")), name=name)(h, wg, wu)


def _ffn_dact(dx, wd, g, u, *, name):
    s, d = dx.shape
    f = wd.shape[0]
    tm, tf = min(512, s), _tile(f, 1408)

    def body(dx_ref, wd_ref, g_ref, u_ref, dg_ref, du_ref):
        da = _dot_nt(dx_ref[...].astype(BF16), wd_ref[...])
        gv, uv = g_ref[...].astype(F32), u_ref[...].astype(F32)
        sg = jax.nn.sigmoid(gv)
        dg_ref[...] = (da * uv * sg * (1.0 + gv * (1.0 - sg))).astype(BF16)
        du_ref[...] = (da * gv * sg).astype(BF16)

    tspec = pl.BlockSpec((tm, tf), lambda j, i: (i, j))
    act = jax.ShapeDtypeStruct((s, f), BF16)
    return pl.pallas_call(
        body, grid=(f // tf, s // tm),
        in_specs=[pl.BlockSpec((tm, d), lambda j, i: (i, 0)), pl.BlockSpec((tf, d), lambda j, i: (j, 0)), tspec, tspec],
        out_specs=(tspec, tspec), out_shape=(act, act),
        compiler_params=_cparams(("parallel", "parallel")), name=name)(dx, wd, g, u)


def _loss_head(y, target, *, name):
    s, d = y.shape
    tm = min(512, s)

    def body(y_ref, t_ref, dy_ref, sq_ref):
        err = y_ref[...] - t_ref[...]
        dy_ref[...] = err * (1.0 / d)

        @pl.when(pl.program_id(0) == 0)
        def _():
            sq_ref[...] = jnp.zeros_like(sq_ref)
        sq_ref[...] += (err * err).reshape(tm // 8, 8, d).sum(axis=0)

    tile = pl.BlockSpec((tm, d), lambda i: (i, 0))
    return pl.pallas_call(
        body, grid=(s // tm,), in_specs=[tile, tile],
        out_specs=(tile, pl.BlockSpec((8, d), lambda i: (0, 0))),
        out_shape=(jax.ShapeDtypeStruct((s, d), F32), jax.ShapeDtypeStruct((8, d), F32)),
        compiler_params=_cparams(("arbitrary",)), name=name)(y, target)


def _adamw(parts, w, m, v, *, name):
    r, c = w.shape
    tr = r if r <= 512 else _tile_rows(r)

    def body(p_ref, w_ref, m_ref, v_ref, g_ref, d_ref, nm_ref, nv_ref):
        g = p_ref[0].astype(F32)
        for dev in range(1, N_DEV):
            g = g + p_ref[dev].astype(F32)
        nm = ADAM_B1 * m_ref[...] + (1.0 - ADAM_B1) * g
        nv = ADAM_B2 * v_ref[...] + (1.0 - ADAM_B2) * (g * g)
        m_hat = nm / (1.0 - ADAM_B1 ** ADAM_STEP)
        v_hat = nv / (1.0 - ADAM_B2 ** ADAM_STEP)
        g_ref[...] = g
        d_ref[...] = -ADAM_LR * (m_hat / (jnp.sqrt(v_hat) + ADAM_EPS) + ADAM_WD * w_ref[...])
        nm_ref[...] = nm
        nv_ref[...] = nv

    tile = pl.BlockSpec((tr, c), lambda i: (i, 0))
    out = jax.ShapeDtypeStruct((r, c), F32)
    return pl.pallas_call(
        body, grid=(r // tr,),
        in_specs=[pl.BlockSpec((N_DEV, tr, c), lambda i: (0, i, 0)), tile, tile, tile],
        out_specs=(tile, tile, tile, tile), out_shape=(out, out, out, out),
        compiler_params=_cparams(("parallel",)), name=name)(parts, w, m, v)


def _tile_rows(r):
    for t in (512, 256, 128, 64, 32, 16, 8):
        if r % t == 0:
            return t
    return r


def _peer(k):
    x, y, c = (lax.axis_index(a) for a in MESH_AXES)
    px = 1 - x if k & 4 else x
    py = 1 - y if k & 2 else y
    pc = 1 - c if k & 1 else c
    return (px, py, pc), 4 * px + 2 * py + pc


def _exchange(src_of, dst_of, local_sem, send_sems, recv_sems):
    x, y, c = (lax.axis_index(a) for a in MESH_AXES)
    me = 4 * x + 2 * y + c
    mine = pltpu.make_async_copy(src_of(me), dst_of(me), local_sem)
    mine.start()
    sends = []
    for k in range(1, N_DEV):
        dev, idx = _peer(k)
        cp = pltpu.make_async_remote_copy(src_ref=src_of(idx), dst_ref=dst_of(me), send_sem=send_sems.at[k - 1],
                                          recv_sem=recv_sems.at[k - 1], device_id=dev,
                                          device_id_type=pl.DeviceIdType.MESH)
        cp.start()
        sends.append(cp)
    for k in range(1, N_DEV):
        dev, idx = _peer(k)
        pltpu.make_async_remote_copy(src_ref=src_of(idx), dst_ref=dst_of(idx), send_sem=send_sems.at[k - 1],
                                     recv_sem=recv_sems.at[k - 1], device_id=dev,
                                     device_id_type=pl.DeviceIdType.MESH).wait_recv()
    for cp in sends:
        cp.wait_send()
    mine.wait()


def _comm_call(body, out_shape, space, name, *args):
    spec = pl.BlockSpec(memory_space=space)
    return pl.pallas_call(
        body, in_specs=[spec] * len(args), out_specs=spec, out_shape=out_shape,
        scratch_shapes=[pltpu.SemaphoreType.DMA, pltpu.SemaphoreType.DMA((N_DEV - 1,)),
                        pltpu.SemaphoreType.DMA((N_DEV - 1,))],
        compiler_params=pltpu.CompilerParams(vmem_limit_bytes=VMEM_LIMIT_MIB * 2 ** 20), name=name)(*args)


def _gather_by_chip(x_ref, out_ref, local_sem, send_sems, recv_sems):
    x, y, c = (lax.axis_index(a) for a in MESH_AXES)
    me, sibling = (x, y, c), (x, y, 1 - c)
    chips = [(1 - x, y), (x, 1 - y), (1 - x, 1 - y)]

    def slab(dev):
        return out_ref.at[4 * dev[0] + 2 * dev[1] + dev[2]]

    def copy(k, owner, to, src=None):
        return pltpu.make_async_remote_copy(
            src_ref=slab(owner) if src is None else src, dst_ref=slab(owner), send_sem=send_sems.at[k],
            recv_sem=recv_sems.at[k], device_id=to, device_id_type=pl.DeviceIdType.MESH)

    mine = pltpu.make_async_copy(x_ref, slab(me), local_sem)
    mine.start()
    first = [copy(0, me, sibling, src=x_ref)] + [copy(1 + j, me, (*chip, c), src=x_ref) for j, chip in enumerate(chips)]
    for cp in first:
        cp.start()
    passed = [copy(4 + j, (*chip, c), sibling) for j, chip in enumerate(chips)]
    for j, chip in enumerate(chips):
        copy(1 + j, (*chip, c), me).wait_recv()
        passed[j].start()
    copy(0, sibling, me).wait_recv()
    for j, chip in enumerate(chips):
        copy(4 + j, (*chip, 1 - c), me).wait_recv()
    for cp in first + passed:
        cp.wait_send()
    mine.wait()


def _all_gather(shard, *, name, space=pl.ANY):
    def body(x_ref, out_ref, local_sem, send_sems, recv_sems):
        _gather_by_chip(x_ref, out_ref, local_sem, send_sems, recv_sems)

    return _comm_call(body, jax.ShapeDtypeStruct((N_DEV,) + shard.shape, shard.dtype), space, name, shard)


def _all_to_all(slabs, *, name):
    def body(x_ref, out_ref, local_sem, send_sems, recv_sems):
        _exchange(lambda d: x_ref.at[d], lambda d: out_ref.at[d], local_sem, send_sems, recv_sems)

    return _comm_call(body, jax.ShapeDtypeStruct(slabs.shape, slabs.dtype), pl.ANY, name, slabs)


def _cols_from_dev(g, depth):
    rows = g.shape[1] // depth
    return g.reshape(N_DEV, depth, rows, g.shape[2]).transpose(1, 2, 0, 3).reshape(depth, rows, -1)


def _rows_from_dev(g, depth):
    r = g.shape[1] // depth
    return g.reshape(N_DEV, depth, r, g.shape[2]).transpose(1, 0, 2, 3).reshape(depth, N_DEV * r, g.shape[2])


def _cols_to_dev(g):
    depth, rows, n = g.shape
    return g.reshape(depth, rows, N_DEV, n // N_DEV).transpose(2, 0, 1, 3).reshape(N_DEV, depth * rows, n // N_DEV)


def _rows_to_dev(g):
    depth, rows, n = g.shape
    return g.reshape(depth, N_DEV, rows // N_DEV, n).transpose(1, 0, 2, 3).reshape(N_DEV, -1, n)


def _pack(arrays):
    flat = jnp.concatenate([a.reshape(-1) for a in arrays])
    pad = (-flat.shape[0]) % (8 * LANES)
    return jnp.pad(flat, (0, pad)).reshape(-1, LANES)


def _unpack(buf, shapes):
    flat = buf.reshape(-1)
    out, pos = [], 0
    for shp in shapes:
        n = int(np.prod(shp))
        out.append(flat[pos:pos + n].reshape(shp))
        pos += n
    return out


def _fold_heads(part8):
    t = part8.sum(axis=(0, 1))
    return t[:HEAD_DIM] + t[HEAD_DIM:]


def _forward_backward(x, target, wts):
    depth = wts["w_in"].shape[0]
    row = lambda a: a.reshape(1, -1)
    saved = []
    for l in range(depth):
        t = f"l{l}_"
        h = _norm_fwd([x], row(wts["attn_norm_w"][l]), group=D_MODEL, name=t + "attn_norm")
        proj = _mm([(h, wts["w_in"][l])], nt=False, name=t + "proj")
        qw2 = row(jnp.tile(wts["q_norm_w"][l], 2))
        kw2 = row(jnp.tile(wts["k_norm_w"][l], 2))
        bias = _bias_expand(wts["rel_bias"][l], name=t + "bias")
        w8 = jnp.pad(wts["conv_w"][l], ((0, 5), (0, 0)))
        sb = [_sb_fwd(proj, hp, name=t + f"sb{hp}") for hp in range(D_SB // LANES)]
        ys = [o[0] for o in sb]
        sb_carries = [o[1:] for o in sb]
        ys += [_ch_fwd(proj, qw2, kw2, bias, hp, name=t + f"ch{hp}") for hp in range(D_CH // LANES)]
        ys.append(_conv_fwd(proj, w8, name=t + "conv"))
        yn = _norm_fwd(ys, row(wts["out_norm_w"][l]), group=HEAD_DIM, name=t + "out_norm")
        x_mid = _mm([(yn, wts["w_out"][l])], nt=False, res=x, out_dtype=F32, name=t + "out_proj")
        h2 = _norm_fwd([x_mid], row(wts["ffn_norm_w"][l]), group=D_MODEL, name=t + "ffn_norm")
        g, u, a = _ffn_up(h2, wts["w_gate"][l], wts["w_up"][l], name=t + "ffn_up")
        x_out = _mm([(a, wts["w_down"][l])], nt=False, res=x_mid, out_dtype=F32, name=t + "ffn_down")
        saved.append(dict(x=x, h=h, proj=proj, sb_carries=sb_carries, qw2=qw2, kw2=kw2, bias=bias, w8=w8, ys=ys, yn=yn, x_mid=x_mid,
                          h2=h2, g=g, u=u, a=a))
        x = x_out

    dx, sq = _loss_head(x, target, name="loss_head")
    grads = {k: [None] * depth for k in
             ("attn_norm_w", "w_in", "q_norm_w", "k_norm_w", "rel_bias", "conv_w", "out_norm_w", "w_out",
              "ffn_norm_w", "w_gate", "w_up", "w_down")}
    for l in reversed(range(depth)):
        t = f"l{l}_b_"
        sv = saved[l]
        grads["w_down"][l] = _mm_tn(sv["a"], dx, name=t + "w_down")
        dg, du = _ffn_dact(dx, wts["w_down"][l], sv["g"], sv["u"], name=t + "ffn_dact")
        grads["w_gate"][l] = _mm_tn(sv["h2"], dg, name=t + "w_gate")
        grads["w_up"][l] = _mm_tn(sv["h2"], du, name=t + "w_up")
        dh2 = _mm([(dg, wts["w_gate"][l]), (du, wts["w_up"][l])], nt=True, name=t + "ffn_dh")
        dx_mid, dw8 = _norm_bwd([sv["x_mid"]], row(wts["ffn_norm_w"][l]), dh2, group=D_MODEL, res=dx,
                                name=t + "ffn_norm")
        grads["ffn_norm_w"][l] = dw8.sum(axis=0)
        grads["w_out"][l] = _mm_tn(sv["yn"], dx_mid, name=t + "w_out")
        dyn = _mm([(dx_mid, wts["w_out"][l])], nt=True, name=t + "out_dy")
        dy, dw8 = _norm_bwd(sv["ys"], row(wts["out_norm_w"][l]), dyn, group=HEAD_DIM, out_dtype=BF16,
                            name=t + "out_norm")
        grads["out_norm_w"][l] = dw8.sum(axis=0)
        proj = sv["proj"]
        n_sb, n_ch = D_SB // LANES, D_CH // LANES
        sb = [_sb_bwd(proj, sv["sb_carries"][hp], dy, hp, name=t + f"sb{hp}") for hp in range(n_sb)]
        ch = [_ch_bwd(proj, dy, sv["qw2"], sv["kw2"], sv["bias"], hp, name=t + f"ch{hp}") for hp in range(n_ch)]
        dgb, dgc, dxc, dcw = _conv_bwd(proj, dy, sv["w8"], name=t + "conv")
        grads["conv_w"][l] = dcw[:3]
        grads["q_norm_w"][l] = _fold_heads(jnp.stack([c[4] for c in ch]))
        grads["k_norm_w"][l] = _fold_heads(jnp.stack([c[5] for c in ch]))
        grads["rel_bias"][l] = _bias_fold(jnp.concatenate([c[3] for c in ch], axis=0), name=t + "bias")
        pieces = ([o[0] for o in sb] + [o[1] for o in sb] + [o[2] for o in sb]
                  + [o[0] for o in ch] + [o[1] for o in ch] + [o[2] for o in ch] + [dgb, dgc, dxc])
        dproj = jnp.concatenate([p.astype(BF16) for p in pieces], axis=1)
        grads["w_in"][l] = _mm_tn(sv["h"], dproj, name=t + "w_in")
        dh = _mm([(dproj, wts["w_in"][l])], nt=True, name=t + "proj_dh")
        dx, dw8 = _norm_bwd([sv["x"]], row(wts["attn_norm_w"][l]), dh, group=D_MODEL, res=dx_mid,
                            name=t + "attn_norm")
        grads["attn_norm_w"][l] = dw8.sum(axis=0)
    return sq, dx, {k: jnp.stack(v) for k, v in grads.items()}


SMALL = ("attn_norm_w", "q_norm_w", "k_norm_w", "rel_bias", "conv_w", "out_norm_w", "ffn_norm_w")
ORDER = ("attn_norm_w", "w_in", "q_norm_w", "k_norm_w", "rel_bias", "conv_w", "out_norm_w", "w_out",
         "ffn_norm_w", "w_gate", "w_up", "w_down")


def kernel(x, attn_norm_w, w_in, q_norm_w, k_norm_w, rel_bias, conv_w, out_norm_w, w_out, ffn_norm_w, w_gate, w_up, w_down, loss_target, m_attn_norm_w, m_w_in, m_q_norm_w, m_k_norm_w, m_rel_bias, m_conv_w, m_out_norm_w, m_w_out, m_ffn_norm_w, m_w_gate, m_w_up, m_w_down, v_attn_norm_w, v_w_in, v_q_norm_w, v_k_norm_w, v_rel_bias, v_conv_w, v_out_norm_w, v_w_out, v_ffn_norm_w, v_w_gate, v_w_up, v_w_down):
    w = dict(attn_norm_w=attn_norm_w, w_in=w_in, q_norm_w=q_norm_w, k_norm_w=k_norm_w, rel_bias=rel_bias,
             conv_w=conv_w, out_norm_w=out_norm_w, w_out=w_out, ffn_norm_w=ffn_norm_w, w_gate=w_gate, w_up=w_up,
             w_down=w_down)
    mom = dict(attn_norm_w=m_attn_norm_w, w_in=m_w_in, q_norm_w=m_q_norm_w, k_norm_w=m_k_norm_w, rel_bias=m_rel_bias,
               conv_w=m_conv_w, out_norm_w=m_out_norm_w, w_out=m_w_out, ffn_norm_w=m_ffn_norm_w, w_gate=m_w_gate,
               w_up=m_w_up, w_down=m_w_down)
    var = dict(attn_norm_w=v_attn_norm_w, w_in=v_w_in, q_norm_w=v_q_norm_w, k_norm_w=v_k_norm_w, rel_bias=v_rel_bias,
               conv_w=v_conv_w, out_norm_w=v_out_norm_w, w_out=v_w_out, ffn_norm_w=v_ffn_norm_w, w_gate=v_w_gate,
               w_up=v_w_up, w_down=v_w_down)
    depth = w_in.shape[0]
    seq = x.shape[1]
    me = 4 * lax.axis_index("x") + 2 * lax.axis_index("y") + lax.axis_index("c")
    flat2 = lambda a: a.reshape(-1, a.shape[-1])

    full = {k: w[k] for k in SMALL if k != "conv_w"}
    for k in ("w_in", "w_gate", "w_up"):
        full[k] = _cols_from_dev(_all_gather(flat2(w[k]).astype(BF16), name="gather_" + k), depth)
    for k in ("w_out", "w_down"):
        full[k] = _rows_from_dev(_all_gather(flat2(w[k]).astype(BF16), name="gather_" + k), depth)
    conv_shard = conv_w.shape[2]
    conv_all = _all_gather(_pack([conv_w]), name="gather_conv_w", space=pltpu.VMEM)
    conv_all = conv_all.reshape(N_DEV, -1)[:, :depth * 3 * conv_shard].reshape(N_DEV, depth, 3, conv_shard)
    full["conv_w"] = conv_all.transpose(1, 2, 0, 3).reshape(depth, 3, N_DEV * conv_shard)

    sq, dx, grads = _forward_backward(x.reshape(seq, D_MODEL), loss_target.reshape(seq, D_MODEL), full)
    loss = lax.psum(0.5 / D_MODEL * jnp.sum(sq), MESH_AXES)

    outs = {}
    for k in ("w_in", "w_gate", "w_up", "w_out", "w_down"):
        slabs = _cols_to_dev(grads[k]) if k in ("w_in", "w_gate", "w_up") else _rows_to_dev(grads[k])
        parts = _all_to_all(slabs.astype(BF16), name="scatter_" + k)
        res = _adamw(parts, flat2(w[k]), flat2(mom[k]), flat2(var[k]), name="adamw_" + k)
        outs[k] = [r.reshape(w[k].shape) for r in res]

    shapes = [grads[k].shape for k in SMALL]
    col0 = me * conv_shard

    def widen(a):
        return lax.dynamic_update_slice(jnp.zeros((depth, 3, N_DEV * conv_shard), F32), a, (0, 0, col0))

    parts = _all_gather(_pack([grads[k] for k in SMALL]), name="gather_small_grads", space=pltpu.VMEM)
    pw = _pack([widen(w[k]) if k == "conv_w" else w[k] for k in SMALL])
    pm = _pack([widen(mom[k]) if k == "conv_w" else mom[k] for k in SMALL])
    pv = _pack([widen(var[k]) if k == "conv_w" else var[k] for k in SMALL])
    res = [_unpack(r, shapes) for r in _adamw(parts, pw, pm, pv, name="adamw_small")]
    for idx, k in enumerate(SMALL):
        vals = [r[idx] for r in res]
        if k == "conv_w":
            vals = [lax.dynamic_slice(a, (0, 0, col0), (depth, 3, conv_shard)) for a in vals]
        outs[k] = vals

    result = [loss, dx.reshape(x.shape)]
    for part in range(4):
        result += [outs[k][part] for k in ORDER]
    return tuple(result)
```

```python
import numpy as np
import jax
import jax.numpy as jnp
from jax import lax
from jax.experimental import pallas as pl
from jax.experimental.pallas import tpu as pltpu

F32 = jnp.float32
BF16 = jnp.bfloat16

N_DEV = 8
D_MODEL = 1024
HEAD_DIM = 64
LANES = 128
SB_BLOCK = 128
SB_GROUP = 8
SB_HALF = SB_GROUP // 2
SB_LANE_LOWER, SB_LANE_UPPER = 127, 126
CH_BLOCK = 128
CH_STEP = 2
CH_STEP_FWD = 4
CH_PAD = 512
CH_BAND = 640
REL_CLIP = 128
N_REL = 2 * REL_CLIP + 1
REL_PAD = 384
D_SB, D_CH, D_CV = 256, 512, 256
COL_QA, COL_KA, COL_VA = 0, 256, 512
COL_QB, COL_KB, COL_VB = 768, 1280, 1792
COL_GB, COL_GC, COL_XC = 2304, 2560, 2816
EPS = 1e-6
NEG = -1e30
SCALE = HEAD_DIM ** -0.5

ADAM_LR, ADAM_B1, ADAM_B2, ADAM_EPS, ADAM_WD, ADAM_STEP = 0.001, 0.9, 0.999, 1e-08, 0.01, 10

VMEM_LIMIT_MIB = 56
MM_WEIGHT_TILE_BYTES = 8 * 2 ** 20
MESH_AXES = ("x", "y", "c")


def _cparams(sem=None):
    return pltpu.CompilerParams(dimension_semantics=sem, vmem_limit_bytes=VMEM_LIMIT_MIB * 2 ** 20)


def _dot(a, b):
    return lax.dot_general(a, b, (((1,), (0,)), ((), ())), preferred_element_type=F32)


def _dot_nt(a, b):
    return lax.dot_general(a, b, (((1,), (1,)), ((), ())), preferred_element_type=F32)


def _dot_tn(a, b):
    return lax.dot_general(a, b, (((0,), (0,)), ((), ())), preferred_element_type=F32)


def _tile(n, cap):
    if n <= cap:
        return n
    best = None
    for t in range(128, cap + 1, 128):
        if n % t == 0:
            best = t
    assert best is not None, (n, cap)
    return best


def _cat(vals):
    return vals[0] if len(vals) == 1 else jnp.concatenate(vals, axis=1)


def _mm(pairs, *, nt, name, res=None, out_dtype=BF16):
    m = pairs[0][0].shape[0]
    n = pairs[0][1].shape[0] if nt else pairs[0][1].shape[1]
    tm = min(512, m)
    tn = n
    while sum(a.shape[1] for a, _ in pairs) * tn * 2 > MM_WEIGHT_TILE_BYTES and tn % 256 == 0:
        tn //= 2
    npairs = len(pairs)

    def body(*refs):
        acc = None
        for p in range(npairs):
            a = refs[2 * p][...].astype(BF16)
            b = refs[2 * p + 1][...]
            d = _dot_nt(a, b) if nt else _dot(a, b)
            acc = d if acc is None else acc + d
        if res is not None:
            acc = acc + refs[2 * npairs][...]
        refs[-1][...] = acc.astype(refs[-1].dtype)

    in_specs, args = [], []
    for a, b in pairs:
        k = a.shape[1]
        in_specs.append(pl.BlockSpec((tm, k), lambda j, i: (i, 0)))
        if nt:
            in_specs.append(pl.BlockSpec((tn, k), lambda j, i: (j, 0)))
        else:
            in_specs.append(pl.BlockSpec((k, tn), lambda j, i: (0, j)))
        args += [a, b]
    if res is not None:
        in_specs.append(pl.BlockSpec((tm, tn), lambda j, i: (i, j)))
        args.append(res)
    return pl.pallas_call(
        body, grid=(n // tn, m // tm), in_specs=in_specs,
        out_specs=pl.BlockSpec((tm, tn), lambda j, i: (i, j)),
        out_shape=jax.ShapeDtypeStruct((m, n), out_dtype),
        compiler_params=_cparams(("parallel", "parallel")), name=name)(*args)


def _mm_tn(a, g, *, name):
    s, k = a.shape
    n = g.shape[1]
    tk, tn, ts = _tile(k, 1408), _tile(n, 1536), min(1024, s)

    def body(a_ref, g_ref, o_ref):
        @pl.when(pl.program_id(2) == 0)
        def _():
            o_ref[...] = jnp.zeros_like(o_ref)
        o_ref[...] += _dot_tn(a_ref[...].astype(BF16), g_ref[...].astype(BF16))

    return pl.pallas_call(
        body, grid=(k // tk, n // tn, s // ts),
        in_specs=[pl.BlockSpec((ts, tk), lambda i, j, t: (t, i)),
                  pl.BlockSpec((ts, tn), lambda i, j, t: (t, j))],
        out_specs=pl.BlockSpec((tk, tn), lambda i, j, t: (i, j)),
        out_shape=jax.ShapeDtypeStruct((k, n), F32),
        compiler_params=_cparams(("parallel", "parallel", "arbitrary")), name=name)(a, g)


def _group_indicator(width, group):
    assert width // group <= LANES
    r = lax.broadcasted_iota(jnp.int32, (width, LANES), 0) // group
    c = lax.broadcasted_iota(jnp.int32, (width, LANES), 1)
    return (r == c).astype(BF16)


def _split_bf16(v):
    hi = v.astype(BF16)
    return hi, (v - hi.astype(F32)).astype(BF16)


def _group_mean(v, ind, group):
    hi, lo = _split_bf16(v)
    hi, lo = _split_bf16((_dot(hi, ind) + _dot(lo, ind)) * (1.0 / group))
    return _dot_nt(hi, ind) + _dot_nt(lo, ind)


def _norm_fwd(xs, w, *, group, name):
    s = xs[0].shape[0]
    width = sum(x.shape[1] for x in xs)
    tm = min(512, s)
    nx = len(xs)
    grouped = group != width

    def body(*refs):
        x = _cat([r[...].astype(F32) for r in refs[:nx]])
        if grouped:
            ms = _group_mean(x * x, refs[nx + 1][...], group)
        else:
            ms = jnp.mean(x * x, axis=1, keepdims=True)
        refs[-1][...] = (x * lax.rsqrt(ms + EPS) * refs[nx][...]).astype(BF16)

    in_specs = [pl.BlockSpec((tm, x.shape[1]), lambda i: (i, 0)) for x in xs]
    in_specs.append(pl.BlockSpec((1, width), lambda i: (0, 0)))
    args = list(xs) + [w]
    if grouped:
        in_specs.append(pl.BlockSpec((width, LANES), lambda i: (0, 0)))
        args.append(_group_indicator(width, group))
    return pl.pallas_call(
        body, grid=(s // tm,), in_specs=in_specs,
        out_specs=pl.BlockSpec((tm, width), lambda i: (i, 0)),
        out_shape=jax.ShapeDtypeStruct((s, width), BF16),
        compiler_params=_cparams(("parallel",)), name=name)(*args)


def _norm_bwd(xs, w, dy, *, group, name, res=None, out_dtype=F32):
    s = xs[0].shape[0]
    width = sum(x.shape[1] for x in xs)
    tm = min(512, s)
    nx = len(xs)
    grouped = group != width
    has_res = res is not None

    def body(*refs):
        x = _cat([r[...].astype(F32) for r in refs[:nx]])
        w_ref, dy_ref = refs[nx], refs[nx + 1]
        pos = nx + 2
        if has_res:
            res_ref = refs[pos]
            pos += 1
        if grouped:
            ind = refs[pos][...]
            mean = lambda v: _group_mean(v, ind, group)
        else:
            mean = lambda v: jnp.mean(v, axis=1, keepdims=True)
        dx_ref, dw_ref = refs[-2], refs[-1]
        inv = lax.rsqrt(mean(x * x) + EPS)
        xh = x * inv
        d = dy_ref[...].astype(F32)
        g = d * w_ref[...]
        dx = inv * (g - xh * mean(g * xh))
        if has_res:
            dx = dx + res_ref[...]
        dx_ref[...] = dx.astype(dx_ref.dtype)

        @pl.when(pl.program_id(0) == 0)
        def _():
            dw_ref[...] = jnp.zeros_like(dw_ref)
        dw_ref[...] += (d * xh).reshape(tm // 8, 8, width).sum(axis=0)

    in_specs = [pl.BlockSpec((tm, x.shape[1]), lambda i: (i, 0)) for x in xs]
    in_specs += [pl.BlockSpec((1, width), lambda i: (0, 0)), pl.BlockSpec((tm, width), lambda i: (i, 0))]
    args = list(xs) + [w, dy]
    if has_res:
        in_specs.append(pl.BlockSpec((tm, width), lambda i: (i, 0)))
        args.append(res)
    if grouped:
        in_specs.append(pl.BlockSpec((width, LANES), lambda i: (0, 0)))
        args.append(_group_indicator(width, group))
    return pl.pallas_call(
        body, grid=(s // tm,), in_specs=in_specs,
        out_specs=(pl.BlockSpec((tm, width), lambda i: (i, 0)), pl.BlockSpec((8, width), lambda i: (0, 0))),
        out_shape=(jax.ShapeDtypeStruct((s, width), out_dtype), jax.ShapeDtypeStruct((8, width), F32)),
        compiler_params=_cparams(("arbitrary",)), name=name)(*args)


def _head0_lanes(shape):
    return lax.broadcasted_iota(jnp.int32, shape, len(shape) - 1) < HEAD_DIM


def _split_heads(v):
    m0 = _head0_lanes(v.shape)
    zero = jnp.zeros_like(v)
    return jnp.where(m0, v, zero), jnp.where(m0, zero, v)


def _pair_sum(v):
    m0 = _head0_lanes(v.shape)
    s0 = jnp.sum(jnp.where(m0, v, 0.0), axis=1, keepdims=True)
    s1 = jnp.sum(jnp.where(m0, 0.0, v), axis=1, keepdims=True)
    return jnp.where(m0, s0, s1)


def _pair_norm(x, w2):
    inv = lax.rsqrt(_pair_sum(x * x) * (1.0 / HEAD_DIM) + EPS)
    xh = x * inv
    return xh * w2, xh, inv


def _pair_norm_bwd(dy, xh, inv, w2):
    g = dy * w2
    return inv * (g - xh * (_pair_sum(g * xh) * (1.0 / HEAD_DIM)))


def _copy_in(copies):
    for c in copies:
        c.start()
    for c in copies:
        c.wait()


def _stack(vals):
    return jnp.concatenate(list(vals), axis=0)


def _unstack(results, chains):
    blocks = []
    for res in results:
        blocks += [res[n * SB_BLOCK:(n + 1) * SB_BLOCK] for n in range(res.shape[0] // SB_BLOCK)]
    return dict(zip(chains, blocks))


def _log_rest(nz):
    return jnp.minimum(nz, 0.0) - jnp.log(1.0 + jnp.exp(-jnp.abs(nz)))


def _sb_fwd(proj, hp, *, name):
    s = proj.shape[0]
    nq = s // SB_BLOCK
    assert nq % SB_GROUP == 0 and nq // SB_GROUP <= SB_LANE_UPPER
    shape = (SB_BLOCK, SB_BLOCK)

    def body(q_ref, proj_hbm, y_ref, c0_ref, c1_ref, k_scr, v_scr, sems):
        i = pl.program_id(0)

        @pl.when(i == 0)
        def _():
            _copy_in([pltpu.make_async_copy(proj_hbm.at[:, pl.ds(COL_KA + hp * LANES, LANES)], k_scr, sems.at[0]),
                      pltpu.make_async_copy(proj_hbm.at[:, pl.ds(COL_VA + hp * LANES, LANES)], v_scr, sems.at[1])])

        row = lax.broadcasted_iota(jnp.int32, shape, 0)
        col = lax.broadcasted_iota(jnp.int32, shape, 1)
        m0 = col < HEAD_DIM
        later_keys = (row > col).astype(BF16)
        q_neg = _stack(_split_heads(q_ref[...] * (-SCALE)))

        def blocks_of(first, n, carry, masked, lane):
            c, acc, seen = list(carry[:2]), carry[2], list(carry[3:5])
            chains = [(b, h) for b in range(n) for h in range(2)]
            js = [first + b for b in reversed(range(n))]
            rows = [pl.ds(pl.multiple_of(j * SB_BLOCK, SB_BLOCK), SB_BLOCK) for j in js]
            earlier = [(col + j * SB_BLOCK < row + i * SB_BLOCK) if masked else None for j in js]
            nz = _unstack([_dot_nt(q_neg, k_scr[r, :]) for r in rows], chains)
            lr = {ch: _log_rest(nz[ch]) for ch in chains}
            lrm = {(b, h): jnp.where(earlier[b], lr[b, h], 0.0) if masked else lr[b, h] for b, h in chains}
            after = _unstack([_dot(_stack([lrm[ch].astype(BF16) for ch in chains]), later_keys)], chains)
            seen = [jnp.where(col == lane, c[h], seen[h]) for h in range(2)]
            w = {}
            for b, h in chains:
                wv = jnp.exp((lr[b, h] - nz[b, h]) + (after[b, h] + c[h]))
                w[b, h] = jnp.where(earlier[b], wv, 0.0) if masked else wv
                c[h] = c[h] + jnp.sum(lrm[b, h], axis=1, keepdims=True)
            pv = _unstack([_dot(_stack([w[b, 0].astype(BF16), w[b, 1].astype(BF16)]), v_scr[rows[b], :])
                           for b in range(n)], chains)
            for b in range(n):
                acc = acc + jnp.where(m0, pv[b, 0], pv[b, 1])
            return c[0], c[1], acc, seen[0], seen[1]

        zc = jnp.zeros((SB_BLOCK, 1), F32)
        zt = jnp.zeros(shape, F32)
        gi = i // SB_GROUP
        carry = lax.cond(i % SB_GROUP >= SB_HALF,
                         lambda cr: blocks_of(gi * SB_GROUP + SB_HALF, SB_HALF, cr, True, SB_LANE_UPPER),
                         lambda cr: cr, (zc, zc, zt, zt, zt))
        carry = blocks_of(gi * SB_GROUP, SB_HALF, carry, True, SB_LANE_LOWER)
        carry = lax.fori_loop(
            0, gi, lambda t, cr: blocks_of((gi - 1 - t) * SB_GROUP, SB_GROUP, cr, False, gi - 1 - t), carry)
        y_ref[...] = carry[2].astype(y_ref.dtype)
        c0_ref[...] = carry[3]
        c1_ref[...] = carry[4]

    return pl.pallas_call(
        body, grid=(nq,),
        in_specs=[pl.BlockSpec(shape, lambda i: (i, COL_QA // LANES + hp)), pl.BlockSpec(memory_space=pl.ANY)],
        out_specs=(pl.BlockSpec(shape, lambda i: (i, 0)),) * 3,
        out_shape=(jax.ShapeDtypeStruct((s, LANES), BF16), jax.ShapeDtypeStruct((s, LANES), F32),
                   jax.ShapeDtypeStruct((s, LANES), F32)),
        scratch_shapes=[pltpu.VMEM((s, LANES), BF16), pltpu.VMEM((s, LANES), BF16), pltpu.SemaphoreType.DMA((2,))],
        compiler_params=_cparams(("arbitrary",)), name=name)(proj, proj)


def _sb_bwd(proj, carries, dy_all, hp, *, name):
    s = proj.shape[0]
    nq = s // SB_BLOCK
    shape = (SB_BLOCK, SB_BLOCK)

    def body(q_ref, c0_ref, c1_ref, dy_ref, proj_hbm, dq_ref, dk_hbm, dv_hbm, k_scr, v_scr, dk_scr, dv_scr, sems):
        i = pl.program_id(0)

        @pl.when(i == 0)
        def _():
            _copy_in([pltpu.make_async_copy(proj_hbm.at[:, pl.ds(COL_KA + hp * LANES, LANES)], k_scr, sems.at[0]),
                      pltpu.make_async_copy(proj_hbm.at[:, pl.ds(COL_VA + hp * LANES, LANES)], v_scr, sems.at[1])])
            dk_scr[...] = jnp.zeros_like(dk_scr)
            dv_scr[...] = jnp.zeros_like(dv_scr)

        row = lax.broadcasted_iota(jnp.int32, shape, 0)
        col = lax.broadcasted_iota(jnp.int32, shape, 1)
        m0 = col < HEAD_DIM
        later_keys = (row > col).astype(BF16)
        earlier_keys = (row < col).astype(BF16)
        q_both = _stack(_split_heads(q_ref[...] * SCALE))
        q_neg = _stack(_split_heads(q_ref[...] * (-SCALE)))
        dy_both = _stack(_split_heads(dy_ref[...]))
        seen = (c0_ref[...], c1_ref[...])

        def blocks_of(first, n, carry, masked, lane):
            r, dq = list(carry[0:2]), list(carry[2:4])
            blocks = range(n)
            chains = [(b, h) for b in blocks for h in range(2)]
            js = [first + b for b in blocks]
            rows = [pl.ds(pl.multiple_of(j * SB_BLOCK, SB_BLOCK), SB_BLOCK) for j in js]
            earlier = [(col + j * SB_BLOCK < row + i * SB_BLOCK) if masked else None for j in js]
            u = _unstack([_dot_nt(q_neg, k_scr[rw, :]) for rw in rows], chains)
            dw = _unstack([_dot_nt(dy_both, v_scr[rw, :]) for rw in rows], chains)
            lr = {ch: _log_rest(u[ch]) for ch in chains}
            lrm = {(b, h): jnp.where(earlier[b], lr[b, h], 0.0) if masked else lr[b, h] for b, h in chains}
            after = _unstack([_dot(_stack([lrm[ch].astype(BF16) for ch in chains]), later_keys)], chains)
            c = {}
            for h in range(2):
                ch = jnp.sum(jnp.where(col == lane, seen[h], 0.0), axis=1, keepdims=True)
                for b in reversed(blocks):
                    c[b, h] = ch
                    ch = ch + jnp.sum(lrm[b, h], axis=1, keepdims=True)
            w, de, lb = {}, {}, {}
            for b, h in chains:
                lb[b, h] = lr[b, h] - u[b, h]
                wv = jnp.exp(lb[b, h] + (after[b, h] + c[b, h]))
                w[b, h] = jnp.where(earlier[b], wv, 0.0) if masked else wv
                de[b, h] = dw[b, h] * w[b, h]
            before = _unstack([_dot(_stack([de[ch].astype(BF16) for ch in chains]), earlier_keys)], chains)
            dzb = {}
            for b, h in chains:
                dz = de[b, h] - jnp.exp(lb[b, h]) * (de[b, h] + (r[h] + before[b, h]))
                if masked:
                    dz = jnp.where(earlier[b], dz, 0.0)
                dzb[b, h] = dz.astype(BF16)
                r[h] = r[h] + jnp.sum(de[b, h], axis=1, keepdims=True)
            keys = pl.ds(pl.multiple_of(first * SB_BLOCK, SB_BLOCK), n * SB_BLOCK)
            dz_wide = [jnp.concatenate([dzb[b, h] for b in blocks], axis=1) for h in range(2)]
            w_wide = [jnp.concatenate([w[b, h].astype(BF16) for b in blocks], axis=1) for h in range(2)]
            dqc = _dot(_stack(dz_wide), k_scr[keys, :])
            dq = [dq[0] + dqc[:SB_BLOCK], dq[1] + dqc[SB_BLOCK:]]
            dk_scr[keys, :] += _dot_tn(_stack(dz_wide), q_both)
            dv_scr[keys, :] += _dot_tn(_stack(w_wide), dy_both)
            return r[0], r[1], dq[0], dq[1]

        zc = jnp.zeros((SB_BLOCK, 1), F32)
        zq = jnp.zeros(shape, F32)
        gi = i // SB_GROUP
        carry = lax.fori_loop(0, gi, lambda t, cr: blocks_of(t * SB_GROUP, SB_GROUP, cr, False, t), (zc, zc, zq, zq))
        carry = blocks_of(gi * SB_GROUP, SB_HALF, carry, True, SB_LANE_LOWER)
        carry = lax.cond(i % SB_GROUP >= SB_HALF,
                         lambda cr: blocks_of(gi * SB_GROUP + SB_HALF, SB_HALF, cr, True, SB_LANE_UPPER),
                         lambda cr: cr, carry)
        dq_ref[...] = (jnp.where(m0, carry[2], carry[3]) * SCALE).astype(dq_ref.dtype)

        @pl.when(i == nq - 1)
        def _():
            _copy_in([pltpu.make_async_copy(dk_scr, dk_hbm, sems.at[0]),
                      pltpu.make_async_copy(dv_scr, dv_hbm, sems.at[1])])

    blk = lambda c0: pl.BlockSpec(shape, lambda i: (i, c0))
    return pl.pallas_call(
        body, grid=(nq,),
        in_specs=[blk(COL_QA // LANES + hp), blk(0), blk(0), blk(hp), pl.BlockSpec(memory_space=pl.ANY)],
        out_specs=(blk(0), pl.BlockSpec(memory_space=pl.ANY), pl.BlockSpec(memory_space=pl.ANY)),
        out_shape=(jax.ShapeDtypeStruct((s, LANES), BF16), jax.ShapeDtypeStruct((s, LANES), F32),
                   jax.ShapeDtypeStruct((s, LANES), F32)),
        scratch_shapes=[pltpu.VMEM((s, LANES), BF16), pltpu.VMEM((s, LANES), BF16),
                        pltpu.VMEM((s, LANES), F32), pltpu.VMEM((s, LANES), F32), pltpu.SemaphoreType.DMA((2,))],
        compiler_params=_cparams(("arbitrary",)), name=name)(proj, carries[0], carries[1], dy_all, proj)


CH_DIAG = CH_BAND + CH_BLOCK


def _diag_index():
    x = np.arange(CH_DIAG)[None, :]
    return (np.clip(CH_PAD + (CH_BLOCK - 1) - x, -REL_CLIP, REL_CLIP) + REL_CLIP).astype(np.int32)


def _diag_onehot(idx_ref):
    rid = lax.broadcasted_iota(jnp.int32, (REL_PAD, CH_DIAG), 0)
    return (rid == idx_ref[...]).astype(F32)


def _bias_expand(rel_bias, *, name):
    nh = rel_bias.shape[0]
    table = jnp.pad(rel_bias, ((0, 0), (0, REL_PAD - N_REL)))

    def body(tab_ref, idx_ref, out_ref):
        diag = lax.dot_general(tab_ref[...], _diag_onehot(idx_ref), (((1,), (0,)), ((), ())),
                               precision=lax.Precision.HIGHEST, preferred_element_type=F32)

        def step(p, _):
            out_ref[p] = pltpu.roll(diag, (CH_DIAG - (CH_BLOCK - 1) + p) % CH_DIAG, 1)[:, :CH_BAND]
            return 0

        lax.fori_loop(0, CH_BLOCK, step, 0)

    out = pl.pallas_call(
        body, out_shape=jax.ShapeDtypeStruct((CH_BLOCK, nh, CH_BAND), F32),
        compiler_params=_cparams(), name=name)(table, jnp.asarray(_diag_index()))
    return out.transpose(1, 0, 2)


def _bias_fold(dbias, *, name):
    nh = dbias.shape[0]

    def body(db_ref, idx_ref, out_ref):
        def step(p, acc):
            wide = jnp.concatenate([db_ref[p], jnp.zeros((nh, CH_BLOCK), F32)], axis=1)
            return acc + pltpu.roll(wide, CH_BLOCK - 1 - p, 1)

        diag = lax.fori_loop(0, CH_BLOCK, step, jnp.zeros((nh, CH_DIAG), F32))
        out_ref[...] = lax.dot_general(diag, _diag_onehot(idx_ref), (((1,), (1,)), ((), ())),
                                       precision=lax.Precision.HIGHEST, preferred_element_type=F32)

    out = pl.pallas_call(
        body, out_shape=jax.ShapeDtypeStruct((nh, REL_PAD), F32),
        compiler_params=_cparams(), name=name)(dbias.transpose(1, 0, 2), jnp.asarray(_diag_index()))
    return out[:, :N_REL]


def _band_valid(i):
    row = lax.broadcasted_iota(jnp.int32, (CH_BLOCK, CH_BAND), 0)
    col = lax.broadcasted_iota(jnp.int32, (CH_BLOCK, CH_BAND), 1)
    first = row < CH_BLOCK // 2
    window = (first & (col < CH_BAND - 64)) | (jnp.logical_not(first) & (col >= 64))
    return window & (col + i * CH_BLOCK >= CH_PAD)


def _ch_load(proj_hbm, hp, s, kw_ref, kraw_scr, kn_scr, vp_scr, sems):
    _copy_in([pltpu.make_async_copy(proj_hbm.at[:, pl.ds(COL_KB + hp * LANES, LANES)], kraw_scr, sems.at[0]),
              pltpu.make_async_copy(proj_hbm.at[:, pl.ds(COL_VB + hp * LANES, LANES)],
                                    vp_scr.at[pl.ds(CH_PAD, s), :], sems.at[1])])
    kn_scr[0:CH_PAD, :] = jnp.zeros((CH_PAD, LANES), BF16)
    vp_scr[0:CH_PAD, :] = jnp.zeros((CH_PAD, LANES), BF16)
    rb = min(512, s)

    def step(t, _):
        r = pl.multiple_of(t * rb, rb)
        kn = _pair_norm(kraw_scr[pl.ds(r, rb), :].astype(F32), kw_ref[...])[0]
        kn_scr[pl.ds(CH_PAD + r, rb), :] = kn.astype(BF16)
        return 0

    lax.fori_loop(0, s // rb, step, 0)


def _ch_probs(qk, bias, valid):
    sc = jnp.where(valid, qk * SCALE + bias, NEG)
    p = jnp.exp(sc - jnp.max(sc, axis=1, keepdims=True))
    return p, 1.0 / jnp.sum(p, axis=1, keepdims=True)


def _ch_fwd(proj, qw2, kw2, bias, hp, *, name):
    s = proj.shape[0]
    nb = s // (CH_BLOCK * CH_STEP_FWD)
    shape = (CH_BLOCK * CH_STEP_FWD, LANES)
    subs = range(CH_STEP_FWD)

    def body(q_ref, proj_hbm, qw_ref, kw_ref, bias_ref, y_ref, kraw_scr, kn_scr, vp_scr, sems):
        i = pl.program_id(0)

        @pl.when(i == 0)
        def _():
            _ch_load(proj_hbm, hp, s, kw_ref, kraw_scr, kn_scr, vp_scr, sems)

        blk = [i * CH_STEP_FWD + t for t in subs]
        part = [slice(t * CH_BLOCK, (t + 1) * CH_BLOCK) for t in subs]
        bands = [pl.ds(pl.multiple_of(b * CH_BLOCK, CH_BLOCK), CH_BAND) for b in blk]
        valid = [_band_valid(b) for b in blk]
        qn = _pair_norm(q_ref[...].astype(F32), qw_ref[...])[0].astype(BF16)
        scores = [_dot_nt(_stack(_split_heads(qn[part[t]])), kn_scr[bands[t], :]) for t in subs]
        probs = {(t, h): _ch_probs(scores[t][h * CH_BLOCK:(h + 1) * CH_BLOCK], bias_ref[h], valid[t])
                 for t in subs for h in range(2)}
        outs = [_dot(_stack([probs[t, h][0].astype(BF16) for h in range(2)]), vp_scr[bands[t], :]) for t in subs]
        for t in subs:
            o = [outs[t][h * CH_BLOCK:(h + 1) * CH_BLOCK] * probs[t, h][1] for h in range(2)]
            y_ref[part[t], :] = jnp.where(_head0_lanes((CH_BLOCK, LANES)), o[0], o[1]).astype(y_ref.dtype)

    return pl.pallas_call(
        body, grid=(nb,),
        in_specs=[pl.BlockSpec(shape, lambda i: (i, COL_QB // LANES + hp)), pl.BlockSpec(memory_space=pl.ANY),
                  pl.BlockSpec((1, LANES), lambda i: (0, 0)), pl.BlockSpec((1, LANES), lambda i: (0, 0)),
                  pl.BlockSpec((2, CH_BLOCK, CH_BAND), lambda i: (hp, 0, 0))],
        out_specs=pl.BlockSpec(shape, lambda i: (i, 0)),
        out_shape=jax.ShapeDtypeStruct((s, LANES), BF16),
        scratch_shapes=[pltpu.VMEM((s, LANES), BF16), pltpu.VMEM((s + CH_PAD, LANES), BF16),
                        pltpu.VMEM((s + CH_PAD, LANES), BF16), pltpu.SemaphoreType.DMA((2,))],
        compiler_params=_cparams(("arbitrary",)), name=name)(proj, proj, qw2, kw2, bias)


def _ch_bwd(proj, dy_all, qw2, kw2, bias, hp, *, name):
    s = proj.shape[0]
    nb = s // (CH_BLOCK * CH_STEP)
    shape = (CH_BLOCK * CH_STEP, LANES)
    subs = range(CH_STEP)
    rb = min(512, s)

    def body(q_ref, dy_ref, proj_hbm, qw_ref, kw_ref, bias_ref, dq_ref, dk_hbm, dv_hbm, dbias_ref, dqw_ref, dkw_ref,
             kraw_scr, kn_scr, vp_scr, dkn_scr, dvp_scr, sems):
        i = pl.program_id(0)

        @pl.when(i == 0)
        def _():
            _ch_load(proj_hbm, hp, s, kw_ref, kraw_scr, kn_scr, vp_scr, sems)
            dkn_scr[...] = jnp.zeros_like(dkn_scr)
            dvp_scr[...] = jnp.zeros_like(dvp_scr)
            dbias_ref[...] = jnp.zeros_like(dbias_ref)
            dqw_ref[...] = jnp.zeros_like(dqw_ref)

        blk = [i * CH_STEP + t for t in subs]
        rows = [slice(t * CH_BLOCK, (t + 1) * CH_BLOCK) for t in subs]
        bands = [pl.ds(pl.multiple_of(b * CH_BLOCK, CH_BLOCK), CH_BAND) for b in blk]
        valid = [_band_valid(b) for b in blk]
        qn, qhat, qinv = _pair_norm(q_ref[...].astype(F32), qw_ref[...])
        qn = qn.astype(BF16)
        dy = dy_ref[...]
        q_both = [_stack(_split_heads(qn[rows[t]])) for t in subs]
        dy_both = [_stack(_split_heads(dy[rows[t]])) for t in subs]
        scores = [_dot_nt(q_both[t], kn_scr[bands[t], :]) for t in subs]
        dprobs = [_dot_nt(dy_both[t], vp_scr[bands[t], :]) for t in subs]
        pb, dsb = {}, {}
        for t in subs:
            for h in range(2):
                part = slice(h * CH_BLOCK, (h + 1) * CH_BLOCK)
                p, rl = _ch_probs(scores[t][part], bias_ref[h], valid[t])
                p = p * rl
                dp = dprobs[t][part]
                ds = p * (dp - jnp.sum(dp * p, axis=1, keepdims=True))
                dbias_ref[h] += ds
                pb[t, h] = p.astype(BF16)
                dsb[t, h] = (ds * SCALE).astype(BF16)
        dsb = [_stack([dsb[t, 0], dsb[t, 1]]) for t in subs]
        dqn = [_dot(dsb[t], kn_scr[bands[t], :]) for t in subs]
        dkc = [_dot_tn(dsb[t], q_both[t]) for t in subs]
        dvc = [_dot_tn(_stack([pb[t, 0], pb[t, 1]]), dy_both[t]) for t in subs]
        for t in subs:
            dkn_scr[bands[t], :] += dkc[t]
            dvp_scr[bands[t], :] += dvc[t]
        dqn = _stack([jnp.where(_head0_lanes((CH_BLOCK, LANES)), d[:CH_BLOCK], d[CH_BLOCK:]) for d in dqn])
        dq_ref[...] = _pair_norm_bwd(dqn, qhat, qinv, qw_ref[...]).astype(dq_ref.dtype)
        dqw_ref[...] += (dqn * qhat).reshape(CH_STEP * CH_BLOCK // 8, 8, LANES).sum(axis=0)

        @pl.when(i == nb - 1)
        def _():
            def step(t, acc):
                r = pl.multiple_of(t * rb, rb)
                rows = pl.ds(CH_PAD + r, rb)
                _, khat, kinv = _pair_norm(kraw_scr[pl.ds(r, rb), :].astype(F32), kw_ref[...])
                dkn = dkn_scr[rows, :]
                dkn_scr[rows, :] = _pair_norm_bwd(dkn, khat, kinv, kw_ref[...])
                return acc + (dkn * khat).reshape(rb // 8, 8, LANES).sum(axis=0)

            dkw_ref[...] = lax.fori_loop(0, s // rb, step, jnp.zeros((8, LANES), F32))
            _copy_in([pltpu.make_async_copy(dkn_scr.at[pl.ds(CH_PAD, s), :], dk_hbm, sems.at[0]),
                      pltpu.make_async_copy(dvp_scr.at[pl.ds(CH_PAD, s), :], dv_hbm, sems.at[1])])

    blk = lambda c0: pl.BlockSpec(shape, lambda i: (i, c0))
    vec = pl.BlockSpec((1, LANES), lambda i: (0, 0))
    part = pl.BlockSpec((8, LANES), lambda i: (0, 0))
    hbm = pl.BlockSpec(memory_space=pl.ANY)
    return pl.pallas_call(
        body, grid=(nb,),
        in_specs=[blk(COL_QB // LANES + hp), blk(D_SB // LANES + hp), hbm, vec, vec,
                  pl.BlockSpec((2, CH_BLOCK, CH_BAND), lambda i: (hp, 0, 0))],
        out_specs=(blk(0), hbm, hbm, pl.BlockSpec((2, CH_BLOCK, CH_BAND), lambda i: (0, 0, 0)), part, part),
        out_shape=(jax.ShapeDtypeStruct((s, LANES), BF16), jax.ShapeDtypeStruct((s, LANES), F32),
                   jax.ShapeDtypeStruct((s, LANES), F32), jax.ShapeDtypeStruct((2, CH_BLOCK, CH_BAND), F32),
                   jax.ShapeDtypeStruct((8, LANES), F32), jax.ShapeDtypeStruct((8, LANES), F32)),
        scratch_shapes=[pltpu.VMEM((s, LANES), BF16), pltpu.VMEM((s + CH_PAD, LANES), BF16),
                        pltpu.VMEM((s + CH_PAD, LANES), BF16), pltpu.VMEM((s + CH_PAD, LANES), F32),
                        pltpu.VMEM((s + CH_PAD, LANES), F32), pltpu.SemaphoreType.DMA((2,))],
        compiler_params=_cparams(("arbitrary",)), name=name)(proj, dy_all, proj, qw2, kw2, bias)


def _shift_down(x, halo, k):
    out = pltpu.roll(x, k, 0)
    row = lax.broadcasted_iota(jnp.int32, x.shape, 0)
    for t in range(k):
        out = jnp.where(row == t, halo[8 - k + t:8 - k + t + 1, :], out)
    return out


def _shift_up(x, halo, k):
    n = x.shape[0]
    out = pltpu.roll(x, n - k, 0)
    row = lax.broadcasted_iota(jnp.int32, x.shape, 0)
    for t in range(k):
        out = jnp.where(row == n - k + t, halo[t:t + 1, :], out)
    return out


def _conv_specs(s, tm):
    cw = D_CV
    tile = lambda c0: pl.BlockSpec((tm, cw), lambda i: (i, c0))
    above = lambda c0: pl.BlockSpec((8, cw), lambda i: (jnp.maximum(i * (tm // 8) - 1, 0), c0))
    below = lambda c0: pl.BlockSpec((8, cw), lambda i: (jnp.minimum((i + 1) * (tm // 8), s // 8 - 1), c0))
    return tile, above, below


def _conv_fwd(proj, w8, *, name):
    s = proj.shape[0]
    tm = min(512, s)
    tile, above, _ = _conv_specs(s, tm)

    def body(gb_ref, gc_ref, xc_ref, gca_ref, xca_ref, w_ref, y_ref):
        hc = gc_ref[...].astype(F32) * xc_ref[...].astype(F32)
        top = jnp.where(pl.program_id(0) > 0, gca_ref[...].astype(F32) * xca_ref[...].astype(F32), 0.0)
        u = w_ref[0:1, :] * _shift_down(hc, top, 2) + w_ref[1:2, :] * _shift_down(hc, top, 1) + w_ref[2:3, :] * hc
        y_ref[...] = (gb_ref[...].astype(F32) * u).astype(y_ref.dtype)

    cb, cc, cx = COL_GB // D_CV, COL_GC // D_CV, COL_XC // D_CV
    return pl.pallas_call(
        body, grid=(s // tm,),
        in_specs=[tile(cb), tile(cc), tile(cx), above(cc), above(cx), pl.BlockSpec((8, D_CV), lambda i: (0, 0))],
        out_specs=pl.BlockSpec((tm, D_CV), lambda i: (i, 0)),
        out_shape=jax.ShapeDtypeStruct((s, D_CV), BF16),
        compiler_params=_cparams(("parallel",)), name=name)(proj, proj, proj, proj, proj, w8)


def _conv_bwd(proj, dy_all, w8, *, name):
    s = proj.shape[0]
    tm = min(512, s)
    nt = s // tm
    tile, above, below = _conv_specs(s, tm)

    def body(gb_ref, gc_ref, xc_ref, dy_ref, gca_ref, xca_ref, gbb_ref, dyb_ref, w_ref,
             dgb_ref, dgc_ref, dxc_ref, dw_ref):
        i = pl.program_id(0)
        gb, gc, xc = gb_ref[...].astype(F32), gc_ref[...].astype(F32), xc_ref[...].astype(F32)
        dy = dy_ref[...].astype(F32)
        hc = gc * xc
        top = jnp.where(i > 0, gca_ref[...].astype(F32) * xca_ref[...].astype(F32), 0.0)
        hc1, hc2 = _shift_down(hc, top, 1), _shift_down(hc, top, 2)
        u = w_ref[0:1, :] * hc2 + w_ref[1:2, :] * hc1 + w_ref[2:3, :] * hc
        du = dy * gb
        bottom = jnp.where(i < nt - 1, dyb_ref[...].astype(F32) * gbb_ref[...].astype(F32), 0.0)
        dhc = w_ref[2:3, :] * du + w_ref[1:2, :] * _shift_up(du, bottom, 1) + w_ref[0:1, :] * _shift_up(du, bottom, 2)
        dgb_ref[...] = (dy * u).astype(dgb_ref.dtype)
        dgc_ref[...] = (dhc * xc).astype(dgc_ref.dtype)
        dxc_ref[...] = (dhc * gc).astype(dxc_ref.dtype)

        @pl.when(i == 0)
        def _():
            dw_ref[...] = jnp.zeros_like(dw_ref)
        dw_ref[0:1, :] += jnp.sum(du * hc2, axis=0, keepdims=True)
        dw_ref[1:2, :] += jnp.sum(du * hc1, axis=0, keepdims=True)
        dw_ref[2:3, :] += jnp.sum(du * hc, axis=0, keepdims=True)

    cb, cc, cx = COL_GB // D_CV, COL_GC // D_CV, COL_XC // D_CV
    cy = (D_SB + D_CH) // D_CV
    out_tile = pl.BlockSpec((tm, D_CV), lambda i: (i, 0))
    act = jax.ShapeDtypeStruct((s, D_CV), BF16)
    return pl.pallas_call(
        body, grid=(nt,),
        in_specs=[tile(cb), tile(cc), tile(cx), tile(cy), above(cc), above(cx), below(cb), below(cy),
                  pl.BlockSpec((8, D_CV), lambda i: (0, 0))],
        out_specs=(out_tile, out_tile, out_tile, pl.BlockSpec((8, D_CV), lambda i: (0, 0))),
        out_shape=(act, act, act, jax.ShapeDtypeStruct((8, D_CV), F32)),
        compiler_params=_cparams(("arbitrary",)), name=name)(proj, proj, proj, dy_all, proj, proj, proj, dy_all, w8)


def _ffn_up(h, wg, wu, *, name):
    s, d = h.shape
    f = wg.shape[1]
    tm, tn = min(512, s), _tile(f, 1408)

    def body(h_ref, wg_ref, wu_ref, g_ref, u_ref, a_ref):
        hv = h_ref[...]
        g = _dot(hv, wg_ref[...])
        u = _dot(hv, wu_ref[...])
        g_ref[...] = g.astype(BF16)
        u_ref[...] = u.astype(BF16)
        a_ref[...] = (g * jax.nn.sigmoid(g) * u).astype(BF16)

    wspec = pl.BlockSpec((d, tn), lambda j, i: (0, j))
    ospec = pl.BlockSpec((tm, tn), lambda j, i: (i, j))
    act = jax.ShapeDtypeStruct((s, f), BF16)
    return pl.pallas_call(
        body, grid=(f // tn, s // tm),
        in_specs=[pl.BlockSpec((tm, d), lambda j, i: (i, 0)), wspec, wspec],
        out_specs=(ospec, ospec, ospec), out_shape=(act, act, act),
        compiler_params=_cparams(("parallel", "parallel")), name=name)(h, wg, wu)


def _ffn_dact(dx, wd, g, u, *, name):
    s, d = dx.shape
    f = wd.shape[0]
    tm, tf = min(512, s), _tile(f, 1408)

    def body(dx_ref, wd_ref, g_ref, u_ref, dg_ref, du_ref):
        da = _dot_nt(dx_ref[...].astype(BF16), wd_ref[...])
        gv, uv = g_ref[...].astype(F32), u_ref[...].astype(F32)
        sg = jax.nn.sigmoid(gv)
        dg_ref[...] = (da * uv * sg * (1.0 + gv * (1.0 - sg))).astype(BF16)
        du_ref[...] = (da * gv * sg).astype(BF16)

    tspec = pl.BlockSpec((tm, tf), lambda j, i: (i, j))
    act = jax.ShapeDtypeStruct((s, f), BF16)
    return pl.pallas_call(
        body, grid=(f // tf, s // tm),
        in_specs=[pl.BlockSpec((tm, d), lambda j, i: (i, 0)), pl.BlockSpec((tf, d), lambda j, i: (j, 0)), tspec, tspec],
        out_specs=(tspec, tspec), out_shape=(act, act),
        compiler_params=_cparams(("parallel", "parallel")), name=name)(dx, wd, g, u)


def _loss_head(y, target, *, name):
    s, d = y.shape
    tm = min(512, s)

    def body(y_ref, t_ref, dy_ref, sq_ref):
        err = y_ref[...] - t_ref[...]
        dy_ref[...] = err * (1.0 / d)

        @pl.when(pl.program_id(0) == 0)
        def _():
            sq_ref[...] = jnp.zeros_like(sq_ref)
        sq_ref[...] += (err * err).reshape(tm // 8, 8, d).sum(axis=0)

    tile = pl.BlockSpec((tm, d), lambda i: (i, 0))
    return pl.pallas_call(
        body, grid=(s // tm,), in_specs=[tile, tile],
        out_specs=(tile, pl.BlockSpec((8, d), lambda i: (0, 0))),
        out_shape=(jax.ShapeDtypeStruct((s, d), F32), jax.ShapeDtypeStruct((8, d), F32)),
        compiler_params=_cparams(("arbitrary",)), name=name)(y, target)


def _adamw(parts, w, m, v, *, name):
    r, c = w.shape
    tr = r if r <= 512 else _tile_rows(r)

    def body(p_ref, w_ref, m_ref, v_ref, g_ref, d_ref, nm_ref, nv_ref):
        g = p_ref[0].astype(F32)
        for dev in range(1, N_DEV):
            g = g + p_ref[dev].astype(F32)
        nm = ADAM_B1 * m_ref[...] + (1.0 - ADAM_B1) * g
        nv = ADAM_B2 * v_ref[...] + (1.0 - ADAM_B2) * (g * g)
        m_hat = nm / (1.0 - ADAM_B1 ** ADAM_STEP)
        v_hat = nv / (1.0 - ADAM_B2 ** ADAM_STEP)
        g_ref[...] = g
        d_ref[...] = -ADAM_LR * (m_hat / (jnp.sqrt(v_hat) + ADAM_EPS) + ADAM_WD * w_ref[...])
        nm_ref[...] = nm
        nv_ref[...] = nv

    tile = pl.BlockSpec((tr, c), lambda i: (i, 0))
    out = jax.ShapeDtypeStruct((r, c), F32)
    return pl.pallas_call(
        body, grid=(r // tr,),
        in_specs=[pl.BlockSpec((N_DEV, tr, c), lambda i: (0, i, 0)), tile, tile, tile],
        out_specs=(tile, tile, tile, tile), out_shape=(out, out, out, out),
        compiler_params=_cparams(("parallel",)), name=name)(parts, w, m, v)


def _tile_rows(r):
    for t in (512, 256, 128, 64, 32, 16, 8):
        if r % t == 0:
            return t
    return r


def _peer(k):
    x, y, c = (lax.axis_index(a) for a in MESH_AXES)
    px = 1 - x if k & 4 else x
    py = 1 - y if k & 2 else y
    pc = 1 - c if k & 1 else c
    return (px, py, pc), 4 * px + 2 * py + pc


def _exchange(src_of, dst_of, local_sem, send_sems, recv_sems):
    x, y, c = (lax.axis_index(a) for a in MESH_AXES)
    me = 4 * x + 2 * y + c
    mine = pltpu.make_async_copy(src_of(me), dst_of(me), local_sem)
    mine.start()
    sends = []
    for k in range(1, N_DEV):
        dev, idx = _peer(k)
        cp = pltpu.make_async_remote_copy(src_ref=src_of(idx), dst_ref=dst_of(me), send_sem=send_sems.at[k - 1],
                                          recv_sem=recv_sems.at[k - 1], device_id=dev,
                                          device_id_type=pl.DeviceIdType.MESH)
        cp.start()
        sends.append(cp)
    for k in range(1, N_DEV):
        dev, idx = _peer(k)
        pltpu.make_async_remote_copy(src_ref=src_of(idx), dst_ref=dst_of(idx), send_sem=send_sems.at[k - 1],
                                     recv_sem=recv_sems.at[k - 1], device_id=dev,
                                     device_id_type=pl.DeviceIdType.MESH).wait_recv()
    for cp in sends:
        cp.wait_send()
    mine.wait()


def _comm_call(body, out_shape, space, name, *args):
    spec = pl.BlockSpec(memory_space=space)
    return pl.pallas_call(
        body, in_specs=[spec] * len(args), out_specs=spec, out_shape=out_shape,
        scratch_shapes=[pltpu.SemaphoreType.DMA, pltpu.SemaphoreType.DMA((N_DEV - 1,)),
                        pltpu.SemaphoreType.DMA((N_DEV - 1,))],
        compiler_params=pltpu.CompilerParams(vmem_limit_bytes=VMEM_LIMIT_MIB * 2 ** 20), name=name)(*args)


def _gather_by_chip(x_ref, out_ref, local_sem, send_sems, recv_sems):
    x, y, c = (lax.axis_index(a) for a in MESH_AXES)
    me, sibling = (x, y, c), (x, y, 1 - c)
    chips = [(1 - x, y), (x, 1 - y), (1 - x, 1 - y)]

    def slab(dev):
        return out_ref.at[4 * dev[0] + 2 * dev[1] + dev[2]]

    def copy(k, owner, to, src=None):
        return pltpu.make_async_remote_copy(
            src_ref=slab(owner) if src is None else src, dst_ref=slab(owner), send_sem=send_sems.at[k],
            recv_sem=recv_sems.at[k], device_id=to, device_id_type=pl.DeviceIdType.MESH)

    mine = pltpu.make_async_copy(x_ref, slab(me), local_sem)
    mine.start()
    first = [copy(0, me, sibling, src=x_ref)] + [copy(1 + j, me, (*chip, c), src=x_ref) for j, chip in enumerate(chips)]
    for cp in first:
        cp.start()
    passed = [copy(4 + j, (*chip, c), sibling) for j, chip in enumerate(chips)]
    for j, chip in enumerate(chips):
        copy(1 + j, (*chip, c), me).wait_recv()
        passed[j].start()
    copy(0, sibling, me).wait_recv()
    for j, chip in enumerate(chips):
        copy(4 + j, (*chip, 1 - c), me).wait_recv()
    for cp in first + passed:
        cp.wait_send()
    mine.wait()


def _all_gather(shard, *, name, space=pl.ANY):
    def body(x_ref, out_ref, local_sem, send_sems, recv_sems):
        _gather_by_chip(x_ref, out_ref, local_sem, send_sems, recv_sems)

    return _comm_call(body, jax.ShapeDtypeStruct((N_DEV,) + shard.shape, shard.dtype), space, name, shard)


def _all_to_all(slabs, *, name):
    def body(x_ref, out_ref, local_sem, send_sems, recv_sems):
        _exchange(lambda d: x_ref.at[d], lambda d: out_ref.at[d], local_sem, send_sems, recv_sems)

    return _comm_call(body, jax.ShapeDtypeStruct(slabs.shape, slabs.dtype), pl.ANY, name, slabs)


def _cols_from_dev(g, depth):
    rows = g.shape[1] // depth
    return g.reshape(N_DEV, depth, rows, g.shape[2]).transpose(1, 2, 0, 3).reshape(depth, rows, -1)


def _rows_from_dev(g, depth):
    r = g.shape[1] // depth
    return g.reshape(N_DEV, depth, r, g.shape[2]).transpose(1, 0, 2, 3).reshape(depth, N_DEV * r, g.shape[2])


def _cols_to_dev(g):
    depth, rows, n = g.shape
    return g.reshape(depth, rows, N_DEV, n // N_DEV).transpose(2, 0, 1, 3).reshape(N_DEV, depth * rows, n // N_DEV)


def _rows_to_dev(g):
    depth, rows, n = g.shape
    return g.reshape(depth, N_DEV, rows // N_DEV, n).transpose(1, 0, 2, 3).reshape(N_DEV, -1, n)


def _pack(arrays):
    flat = jnp.concatenate([a.reshape(-1) for a in arrays])
    pad = (-flat.shape[0]) % (8 * LANES)
    return jnp.pad(flat, (0, pad)).reshape(-1, LANES)


def _unpack(buf, shapes):
    flat = buf.reshape(-1)
    out, pos = [], 0
    for shp in shapes:
        n = int(np.prod(shp))
        out.append(flat[pos:pos + n].reshape(shp))
        pos += n
    return out


def _fold_heads(part8):
    t = part8.sum(axis=(0, 1))
    return t[:HEAD_DIM] + t[HEAD_DIM:]


def _forward_backward(x, target, wts):
    depth = wts["w_in"].shape[0]
    row = lambda a: a.reshape(1, -1)
    saved = []
    for l in range(depth):
        t = f"l{l}_"
        h = _norm_fwd([x], row(wts["attn_norm_w"][l]), group=D_MODEL, name=t + "attn_norm")
        proj = _mm([(h, wts["w_in"][l])], nt=False, name=t + "proj")
        qw2 = row(jnp.tile(wts["q_norm_w"][l], 2))
        kw2 = row(jnp.tile(wts["k_norm_w"][l], 2))
        bias = _bias_expand(wts["rel_bias"][l], name=t + "bias")
        w8 = jnp.pad(wts["conv_w"][l], ((0, 5), (0, 0)))
        sb = [_sb_fwd(proj, hp, name=t + f"sb{hp}") for hp in range(D_SB // LANES)]
        ys = [o[0] for o in sb]
        sb_carries = [o[1:] for o in sb]
        ys += [_ch_fwd(proj, qw2, kw2, bias, hp, name=t + f"ch{hp}") for hp in range(D_CH // LANES)]
        ys.append(_conv_fwd(proj, w8, name=t + "conv"))
        yn = _norm_fwd(ys, row(wts["out_norm_w"][l]), group=HEAD_DIM, name=t + "out_norm")
        x_mid = _mm([(yn, wts["w_out"][l])], nt=False, res=x, out_dtype=F32, name=t + "out_proj")
        h2 = _norm_fwd([x_mid], row(wts["ffn_norm_w"][l]), group=D_MODEL, name=t + "ffn_norm")
        g, u, a = _ffn_up(h2, wts["w_gate"][l], wts["w_up"][l], name=t + "ffn_up")
        x_out = _mm([(a, wts["w_down"][l])], nt=False, res=x_mid, out_dtype=F32, name=t + "ffn_down")
        saved.append(dict(x=x, h=h, proj=proj, sb_carries=sb_carries, qw2=qw2, kw2=kw2, bias=bias, w8=w8, ys=ys, yn=yn, x_mid=x_mid,
                          h2=h2, g=g, u=u, a=a))
        x = x_out

    dx, sq = _loss_head(x, target, name="loss_head")
    grads = {k: [None] * depth for k in
             ("attn_norm_w", "w_in", "q_norm_w", "k_norm_w", "rel_bias", "conv_w", "out_norm_w", "w_out",
              "ffn_norm_w", "w_gate", "w_up", "w_down")}
    for l in reversed(range(depth)):
        t = f"l{l}_b_"
        sv = saved[l]
        grads["w_down"][l] = _mm_tn(sv["a"], dx, name=t + "w_down")
        dg, du = _ffn_dact(dx, wts["w_down"][l], sv["g"], sv["u"], name=t + "ffn_dact")
        grads["w_gate"][l] = _mm_tn(sv["h2"], dg, name=t + "w_gate")
        grads["w_up"][l] = _mm_tn(sv["h2"], du, name=t + "w_up")
        dh2 = _mm([(dg, wts["w_gate"][l]), (du, wts["w_up"][l])], nt=True, name=t + "ffn_dh")
        dx_mid, dw8 = _norm_bwd([sv["x_mid"]], row(wts["ffn_norm_w"][l]), dh2, group=D_MODEL, res=dx,
                                name=t + "ffn_norm")
        grads["ffn_norm_w"][l] = dw8.sum(axis=0)
        grads["w_out"][l] = _mm_tn(sv["yn"], dx_mid, name=t + "w_out")
        dyn = _mm([(dx_mid, wts["w_out"][l])], nt=True, name=t + "out_dy")
        dy, dw8 = _norm_bwd(sv["ys"], row(wts["out_norm_w"][l]), dyn, group=HEAD_DIM, out_dtype=BF16,
                            name=t + "out_norm")
        grads["out_norm_w"][l] = dw8.sum(axis=0)
        proj = sv["proj"]
        n_sb, n_ch = D_SB // LANES, D_CH // LANES
        sb = [_sb_bwd(proj, sv["sb_carries"][hp], dy, hp, name=t + f"sb{hp}") for hp in range(n_sb)]
        ch = [_ch_bwd(proj, dy, sv["qw2"], sv["kw2"], sv["bias"], hp, name=t + f"ch{hp}") for hp in range(n_ch)]
        dgb, dgc, dxc, dcw = _conv_bwd(proj, dy, sv["w8"], name=t + "conv")
        grads["conv_w"][l] = dcw[:3]
        grads["q_norm_w"][l] = _fold_heads(jnp.stack([c[4] for c in ch]))
        grads["k_norm_w"][l] = _fold_heads(jnp.stack([c[5] for c in ch]))
        grads["rel_bias"][l] = _bias_fold(jnp.concatenate([c[3] for c in ch], axis=0), name=t + "bias")
        pieces = ([o[0] for o in sb] + [o[1] for o in sb] + [o[2] for o in sb]
                  + [o[0] for o in ch] + [o[1] for o in ch] + [o[2] for o in ch] + [dgb, dgc, dxc])
        dproj = jnp.concatenate([p.astype(BF16) for p in pieces], axis=1)
        grads["w_in"][l] = _mm_tn(sv["h"], dproj, name=t + "w_in")
        dh = _mm([(dproj, wts["w_in"][l])], nt=True, name=t + "proj_dh")
        dx, dw8 = _norm_bwd([sv["x"]], row(wts["attn_norm_w"][l]), dh, group=D_MODEL, res=dx_mid,
                            name=t + "attn_norm")
        grads["attn_norm_w"][l] = dw8.sum(axis=0)
    return sq, dx, {k: jnp.stack(v) for k, v in grads.items()}


SMALL = ("attn_norm_w", "q_norm_w", "k_norm_w", "rel_bias", "conv_w", "out_norm_w", "ffn_norm_w")
ORDER = ("attn_norm_w", "w_in", "q_norm_w", "k_norm_w", "rel_bias", "conv_w", "out_norm_w", "w_out",
         "ffn_norm_w", "w_gate", "w_up", "w_down")


def kernel(x, attn_norm_w, w_in, q_norm_w, k_norm_w, rel_bias, conv_w, out_norm_w, w_out, ffn_norm_w, w_gate, w_up, w_down, loss_target, m_attn_norm_w, m_w_in, m_q_norm_w, m_k_norm_w, m_rel_bias, m_conv_w, m_out_norm_w, m_w_out, m_ffn_norm_w, m_w_gate, m_w_up, m_w_down, v_attn_norm_w, v_w_in, v_q_norm_w, v_k_norm_w, v_rel_bias, v_conv_w, v_out_norm_w, v_w_out, v_ffn_norm_w, v_w_gate, v_w_up, v_w_down):
    w = dict(attn_norm_w=attn_norm_w, w_in=w_in, q_norm_w=q_norm_w, k_norm_w=k_norm_w, rel_bias=rel_bias,
             conv_w=conv_w, out_norm_w=out_norm_w, w_out=w_out, ffn_norm_w=ffn_norm_w, w_gate=w_gate, w_up=w_up,
             w_down=w_down)
    mom = dict(attn_norm_w=m_attn_norm_w, w_in=m_w_in, q_norm_w=m_q_norm_w, k_norm_w=m_k_norm_w, rel_bias=m_rel_bias,
               conv_w=m_conv_w, out_norm_w=m_out_norm_w, w_out=m_w_out, ffn_norm_w=m_ffn_norm_w, w_gate=m_w_gate,
               w_up=m_w_up, w_down=m_w_down)
    var = dict(attn_norm_w=v_attn_norm_w, w_in=v_w_in, q_norm_w=v_q_norm_w, k_norm_w=v_k_norm_w, rel_bias=v_rel_bias,
               conv_w=v_conv_w, out_norm_w=v_out_norm_w, w_out=v_w_out, ffn_norm_w=v_ffn_norm_w, w_gate=v_w_gate,
               w_up=v_w_up, w_down=v_w_down)
    depth = w_in.shape[0]
    seq = x.shape[1]
    me = 4 * lax.axis_index("x") + 2 * lax.axis_index("y") + lax.axis_index("c")
    flat2 = lambda a: a.reshape(-1, a.shape[-1])

    full = {k: w[k] for k in SMALL if k != "conv_w"}
    for k in ("w_in", "w_gate", "w_up"):
        full[k] = _cols_from_dev(_all_gather(flat2(w[k]).astype(BF16), name="gather_" + k), depth)
    for k in ("w_out", "w_down"):
        full[k] = _rows_from_dev(_all_gather(flat2(w[k]).astype(BF16), name="gather_" + k), depth)
    conv_shard = conv_w.shape[2]
    conv_all = _all_gather(_pack([conv_w]), name="gather_conv_w", space=pltpu.VMEM)
    conv_all = conv_all.reshape(N_DEV, -1)[:, :depth * 3 * conv_shard].reshape(N_DEV, depth, 3, conv_shard)
    full["conv_w"] = conv_all.transpose(1, 2, 0, 3).reshape(depth, 3, N_DEV * conv_shard)

    sq, dx, grads = _forward_backward(x.reshape(seq, D_MODEL), loss_target.reshape(seq, D_MODEL), full)
    loss = lax.psum(0.5 / D_MODEL * jnp.sum(sq), MESH_AXES)

    outs = {}
    for k in ("w_in", "w_gate", "w_up", "w_out", "w_down"):
        slabs = _cols_to_dev(grads[k]) if k in ("w_in", "w_gate", "w_up") else _rows_to_dev(grads[k])
        parts = _all_to_all(slabs.astype(BF16), name="scatter_" + k)
        res = _adamw(parts, flat2(w[k]), flat2(mom[k]), flat2(var[k]), name="adamw_" + k)
        outs[k] = [r.reshape(w[k].shape) for r in res]

    shapes = [grads[k].shape for k in SMALL]
    col0 = me * conv_shard

    def widen(a):
        return lax.dynamic_update_slice(jnp.zeros((depth, 3, N_DEV * conv_shard), F32), a, (0, 0, col0))

    parts = _all_gather(_pack([grads[k] for k in SMALL]), name="gather_small_grads", space=pltpu.VMEM)
    pw = _pack([widen(w[k]) if k == "conv_w" else w[k] for k in SMALL])
    pm = _pack([widen(mom[k]) if k == "conv_w" else mom[k] for k in SMALL])
    pv = _pack([widen(var[k]) if k == "conv_w" else var[k] for k in SMALL])
    res = [_unpack(r, shapes) for r in _adamw(parts, pw, pm, pv, name="adamw_small")]
    for idx, k in enumerate(SMALL):
        vals = [r[idx] for r in res]
        if k == "conv_w":
            vals = [lax.dynamic_slice(a, (0, 0, col0), (depth, 3, conv_shard)) for a in vals]
        outs[k] = vals

    result = [loss, dx.reshape(x.shape)]
    for part in range(4):
        result += [outs[k][part] for k in ORDER]
    return tuple(result)
```

```python
import numpy as np
import jax
import jax.numpy as jnp
from jax import lax
from jax.experimental import pallas as pl
from jax.experimental.pallas import tpu as pltpu

F32 = jnp.float32
BF16 = jnp.bfloat16

N_DEV = 8
D_MODEL = 1024
HEAD_DIM = 64
LANES = 128
SB_BLOCK = 128
SB_GROUP = 8
SB_STEP = 4
SB_HALF = SB_GROUP // 2
SB_LANE_LOWER, SB_LANE_UPPER = 127, 126
CH_BLOCK = 128
CH_STEP = 2
CH_STEP_FWD = 4
CH_PAD = 512
CH_BAND = 640
REL_CLIP = 128
N_REL = 2 * REL_CLIP + 1
REL_PAD = 384
D_SB, D_CH, D_CV = 256, 512, 256
COL_QA, COL_KA, COL_VA = 0, 256, 512
COL_QB, COL_KB, COL_VB = 768, 1280, 1792
COL_GB, COL_GC, COL_XC = 2304, 2560, 2816
EPS = 1e-6
NEG = -1e30
SCALE = HEAD_DIM ** -0.5

ADAM_LR, ADAM_B1, ADAM_B2, ADAM_EPS, ADAM_WD, ADAM_STEP = 0.001, 0.9, 0.999, 1e-08, 0.01, 10

VMEM_LIMIT_MIB = 56
MM_WEIGHT_TILE_BYTES = 8 * 2 ** 20
MESH_AXES = ("x", "y", "c")


def _cparams(sem=None):
    return pltpu.CompilerParams(dimension_semantics=sem, vmem_limit_bytes=VMEM_LIMIT_MIB * 2 ** 20)


def _dot(a, b):
    return lax.dot_general(a, b, (((1,), (0,)), ((), ())), preferred_element_type=F32)


def _dot_nt(a, b):
    return lax.dot_general(a, b, (((1,), (1,)), ((), ())), preferred_element_type=F32)


def _dot_tn(a, b):
    return lax.dot_general(a, b, (((0,), (0,)), ((), ())), preferred_element_type=F32)


def _tile(n, cap):
    if n <= cap:
        return n
    best = None
    for t in range(128, cap + 1, 128):
        if n % t == 0:
            best = t
    assert best is not None, (n, cap)
    return best


def _cat(vals):
    return vals[0] if len(vals) == 1 else jnp.concatenate(vals, axis=1)


def _mm(pairs, *, nt, name, res=None, out_dtype=BF16):
    m = pairs[0][0].shape[0]
    n = pairs[0][1].shape[0] if nt else pairs[0][1].shape[1]
    tm = min(512, m)
    tn = n
    while sum(a.shape[1] for a, _ in pairs) * tn * 2 > MM_WEIGHT_TILE_BYTES and tn % 256 == 0:
        tn //= 2
    npairs = len(pairs)

    def body(*refs):
        acc = None
        for p in range(npairs):
            a = refs[2 * p][...].astype(BF16)
            b = refs[2 * p + 1][...]
            d = _dot_nt(a, b) if nt else _dot(a, b)
            acc = d if acc is None else acc + d
        if res is not None:
            acc = acc + refs[2 * npairs][...]
        refs[-1][...] = acc.astype(refs[-1].dtype)

    in_specs, args = [], []
    for a, b in pairs:
        k = a.shape[1]
        in_specs.append(pl.BlockSpec((tm, k), lambda j, i: (i, 0)))
        if nt:
            in_specs.append(pl.BlockSpec((tn, k), lambda j, i: (j, 0)))
        else:
            in_specs.append(pl.BlockSpec((k, tn), lambda j, i: (0, j)))
        args += [a, b]
    if res is not None:
        in_specs.append(pl.BlockSpec((tm, tn), lambda j, i: (i, j)))
        args.append(res)
    return pl.pallas_call(
        body, grid=(n // tn, m // tm), in_specs=in_specs,
        out_specs=pl.BlockSpec((tm, tn), lambda j, i: (i, j)),
        out_shape=jax.ShapeDtypeStruct((m, n), out_dtype),
        compiler_params=_cparams(("parallel", "parallel")), name=name)(*args)


def _mm_tn(a, g, *, name):
    s, k = a.shape
    n = g.shape[1]
    tk, tn, ts = _tile(k, 1408), _tile(n, 1536), min(1024, s)

    def body(a_ref, g_ref, o_ref):
        @pl.when(pl.program_id(2) == 0)
        def _():
            o_ref[...] = jnp.zeros_like(o_ref)
        o_ref[...] += _dot_tn(a_ref[...].astype(BF16), g_ref[...].astype(BF16))

    return pl.pallas_call(
        body, grid=(k // tk, n // tn, s // ts),
        in_specs=[pl.BlockSpec((ts, tk), lambda i, j, t: (t, i)),
                  pl.BlockSpec((ts, tn), lambda i, j, t: (t, j))],
        out_specs=pl.BlockSpec((tk, tn), lambda i, j, t: (i, j)),
        out_shape=jax.ShapeDtypeStruct((k, n), F32),
        compiler_params=_cparams(("parallel", "parallel", "arbitrary")), name=name)(a, g)


def _group_indicator(width, group):
    assert width // group <= LANES
    r = lax.broadcasted_iota(jnp.int32, (width, LANES), 0) // group
    c = lax.broadcasted_iota(jnp.int32, (width, LANES), 1)
    return (r == c).astype(BF16)


def _split_bf16(v):
    hi = v.astype(BF16)
    return hi, (v - hi.astype(F32)).astype(BF16)


def _group_mean(v, ind, group):
    hi, lo = _split_bf16(v)
    hi, lo = _split_bf16((_dot(hi, ind) + _dot(lo, ind)) * (1.0 / group))
    return _dot_nt(hi, ind) + _dot_nt(lo, ind)


def _norm_fwd(xs, w, *, group, name):
    s = xs[0].shape[0]
    width = sum(x.shape[1] for x in xs)
    tm = min(512, s)
    nx = len(xs)
    grouped = group != width

    def body(*refs):
        x = _cat([r[...].astype(F32) for r in refs[:nx]])
        if grouped:
            ms = _group_mean(x * x, refs[nx + 1][...], group)
        else:
            ms = jnp.mean(x * x, axis=1, keepdims=True)
        refs[-1][...] = (x * lax.rsqrt(ms + EPS) * refs[nx][...]).astype(BF16)

    in_specs = [pl.BlockSpec((tm, x.shape[1]), lambda i: (i, 0)) for x in xs]
    in_specs.append(pl.BlockSpec((1, width), lambda i: (0, 0)))
    args = list(xs) + [w]
    if grouped:
        in_specs.append(pl.BlockSpec((width, LANES), lambda i: (0, 0)))
        args.append(_group_indicator(width, group))
    return pl.pallas_call(
        body, grid=(s // tm,), in_specs=in_specs,
        out_specs=pl.BlockSpec((tm, width), lambda i: (i, 0)),
        out_shape=jax.ShapeDtypeStruct((s, width), BF16),
        compiler_params=_cparams(("parallel",)), name=name)(*args)


def _norm_bwd(xs, w, dy, *, group, name, res=None, out_dtype=F32):
    s = xs[0].shape[0]
    width = sum(x.shape[1] for x in xs)
    tm = min(512, s)
    nx = len(xs)
    grouped = group != width
    has_res = res is not None

    def body(*refs):
        x = _cat([r[...].astype(F32) for r in refs[:nx]])
        w_ref, dy_ref = refs[nx], refs[nx + 1]
        pos = nx + 2
        if has_res:
            res_ref = refs[pos]
            pos += 1
        if grouped:
            ind = refs[pos][...]
            mean = lambda v: _group_mean(v, ind, group)
        else:
            mean = lambda v: jnp.mean(v, axis=1, keepdims=True)
        dx_ref, dw_ref = refs[-2], refs[-1]
        inv = lax.rsqrt(mean(x * x) + EPS)
        xh = x * inv
        d = dy_ref[...].astype(F32)
        g = d * w_ref[...]
        dx = inv * (g - xh * mean(g * xh))
        if has_res:
            dx = dx + res_ref[...]
        dx_ref[...] = dx.astype(dx_ref.dtype)

        @pl.when(pl.program_id(0) == 0)
        def _():
            dw_ref[...] = jnp.zeros_like(dw_ref)
        dw_ref[...] += (d * xh).reshape(tm // 8, 8, width).sum(axis=0)

    in_specs = [pl.BlockSpec((tm, x.shape[1]), lambda i: (i, 0)) for x in xs]
    in_specs += [pl.BlockSpec((1, width), lambda i: (0, 0)), pl.BlockSpec((tm, width), lambda i: (i, 0))]
    args = list(xs) + [w, dy]
    if has_res:
        in_specs.append(pl.BlockSpec((tm, width), lambda i: (i, 0)))
        args.append(res)
    if grouped:
        in_specs.append(pl.BlockSpec((width, LANES), lambda i: (0, 0)))
        args.append(_group_indicator(width, group))
    return pl.pallas_call(
        body, grid=(s // tm,), in_specs=in_specs,
        out_specs=(pl.BlockSpec((tm, width), lambda i: (i, 0)), pl.BlockSpec((8, width), lambda i: (0, 0))),
        out_shape=(jax.ShapeDtypeStruct((s, width), out_dtype), jax.ShapeDtypeStruct((8, width), F32)),
        compiler_params=_cparams(("arbitrary",)), name=name)(*args)


def _head0_lanes(shape):
    return lax.broadcasted_iota(jnp.int32, shape, len(shape) - 1) < HEAD_DIM


def _split_heads(v):
    m0 = _head0_lanes(v.shape)
    zero = jnp.zeros_like(v)
    return jnp.where(m0, v, zero), jnp.where(m0, zero, v)


def _pair_sum(v):
    m0 = _head0_lanes(v.shape)
    s0 = jnp.sum(jnp.where(m0, v, 0.0), axis=1, keepdims=True)
    s1 = jnp.sum(jnp.where(m0, 0.0, v), axis=1, keepdims=True)
    return jnp.where(m0, s0, s1)


def _pair_norm(x, w2):
    inv = lax.rsqrt(_pair_sum(x * x) * (1.0 / HEAD_DIM) + EPS)
    xh = x * inv
    return xh * w2, xh, inv


def _pair_norm_bwd(dy, xh, inv, w2):
    g = dy * w2
    return inv * (g - xh * (_pair_sum(g * xh) * (1.0 / HEAD_DIM)))


def _copy_in(copies):
    for c in copies:
        c.start()
    for c in copies:
        c.wait()


def _stack(vals):
    return jnp.concatenate(list(vals), axis=0)


def _unstack(results, chains):
    blocks = []
    for res in results:
        blocks += [res[n * SB_BLOCK:(n + 1) * SB_BLOCK] for n in range(res.shape[0] // SB_BLOCK)]
    return dict(zip(chains, blocks))


def _log_rest(nz):
    return jnp.minimum(nz, 0.0) - jnp.log(1.0 + jnp.exp(-jnp.abs(nz)))


def _sb_fwd(proj, hp, *, name):
    s = proj.shape[0]
    nq = s // SB_BLOCK
    assert nq % SB_GROUP == 0 and nq // SB_GROUP <= SB_LANE_UPPER
    shape = (SB_BLOCK, SB_BLOCK)

    def body(i, q_ref, proj_hbm, y_ref, c0_ref, c1_ref, k_scr, v_scr, sems):
        @pl.when(i == 0)
        def _():
            _copy_in([pltpu.make_async_copy(proj_hbm.at[:, pl.ds(COL_KA + hp * LANES, LANES)], k_scr, sems.at[0]),
                      pltpu.make_async_copy(proj_hbm.at[:, pl.ds(COL_VA + hp * LANES, LANES)], v_scr, sems.at[1])])

        row = lax.broadcasted_iota(jnp.int32, shape, 0)
        col = lax.broadcasted_iota(jnp.int32, shape, 1)
        m0 = col < HEAD_DIM
        later_keys = (row > col).astype(BF16)
        q_neg = _stack(_split_heads(q_ref[...] * (-SCALE)))

        def blocks_of(first, n, carry, masked, lane):
            c, acc, seen = list(carry[:2]), carry[2], list(carry[3:5])
            chains = [(b, h) for b in range(n) for h in range(2)]
            js = [first + b for b in reversed(range(n))]
            rows = [pl.ds(pl.multiple_of(j * SB_BLOCK, SB_BLOCK), SB_BLOCK) for j in js]
            earlier = [(col + j * SB_BLOCK < row + i * SB_BLOCK) if masked else None for j in js]
            nz = _unstack([_dot_nt(q_neg, k_scr[r, :]) for r in rows], chains)
            lr = {ch: _log_rest(nz[ch]) for ch in chains}
            lrm = {(b, h): jnp.where(earlier[b], lr[b, h], 0.0) if masked else lr[b, h] for b, h in chains}
            after = _unstack([_dot(_stack([lrm[ch].astype(BF16) for ch in chains]), later_keys)], chains)
            seen = [jnp.where(col == lane, c[h], seen[h]) for h in range(2)]
            w = {}
            for b, h in chains:
                wv = jnp.exp((lr[b, h] - nz[b, h]) + (after[b, h] + c[h]))
                w[b, h] = jnp.where(earlier[b], wv, 0.0) if masked else wv
                c[h] = c[h] + jnp.sum(lrm[b, h], axis=1, keepdims=True)
            pv = _unstack([_dot(_stack([w[b, 0].astype(BF16), w[b, 1].astype(BF16)]), v_scr[rows[b], :])
                           for b in range(n)], chains)
            for b in range(n):
                acc = acc + jnp.where(m0, pv[b, 0], pv[b, 1])
            return c[0], c[1], acc, seen[0], seen[1]

        zc = jnp.zeros((SB_BLOCK, 1), F32)
        zt = jnp.zeros(shape, F32)
        gi = i // SB_GROUP
        carry = lax.cond(i % SB_GROUP >= SB_HALF,
                         lambda cr: blocks_of(gi * SB_GROUP + SB_HALF, SB_HALF, cr, True, SB_LANE_UPPER),
                         lambda cr: cr, (zc, zc, zt, zt, zt))
        carry = blocks_of(gi * SB_GROUP, SB_HALF, carry, True, SB_LANE_LOWER)
        carry = lax.fori_loop(
            0, gi, lambda t, cr: blocks_of((gi - 1 - t) * SB_GROUP, SB_GROUP, cr, False, gi - 1 - t), carry)
        y_ref[...] = carry[2].astype(y_ref.dtype)
        c0_ref[...] = carry[3]
        c1_ref[...] = carry[4]

    def steps(q_ref, proj_hbm, y_ref, c0_ref, c1_ref, *scratch):
        def one(t, _):
            rows = pl.ds(pl.multiple_of(t * SB_BLOCK, SB_BLOCK), SB_BLOCK)
            body(pl.program_id(0) * SB_STEP + t, q_ref.at[rows, :], proj_hbm, y_ref.at[rows, :],
                 c0_ref.at[rows, :], c1_ref.at[rows, :], *scratch)
            return 0

        lax.fori_loop(0, SB_STEP, one, 0)

    tile = (SB_STEP * SB_BLOCK, LANES)
    return pl.pallas_call(
        steps, grid=(nq // SB_STEP,),
        in_specs=[pl.BlockSpec(tile, lambda i: (i, COL_QA // LANES + hp)), pl.BlockSpec(memory_space=pl.ANY)],
        out_specs=(pl.BlockSpec(tile, lambda i: (i, 0)),) * 3,
        out_shape=(jax.ShapeDtypeStruct((s, LANES), BF16), jax.ShapeDtypeStruct((s, LANES), F32),
                   jax.ShapeDtypeStruct((s, LANES), F32)),
        scratch_shapes=[pltpu.VMEM((s, LANES), BF16), pltpu.VMEM((s, LANES), BF16), pltpu.SemaphoreType.DMA((2,))],
        compiler_params=_cparams(("arbitrary",)), name=name)(proj, proj)


def _sb_bwd(proj, carries, dy_all, hp, *, name):
    s = proj.shape[0]
    nq = s // SB_BLOCK
    shape = (SB_BLOCK, SB_BLOCK)

    def body(i, q_ref, c0_ref, c1_ref, dy_ref, proj_hbm, dq_ref, dk_hbm, dv_hbm, k_scr, v_scr, dk_scr, dv_scr, sems):
        @pl.when(i == 0)
        def _():
            _copy_in([pltpu.make_async_copy(proj_hbm.at[:, pl.ds(COL_KA + hp * LANES, LANES)], k_scr, sems.at[0]),
                      pltpu.make_async_copy(proj_hbm.at[:, pl.ds(COL_VA + hp * LANES, LANES)], v_scr, sems.at[1])])
            dk_scr[...] = jnp.zeros_like(dk_scr)
            dv_scr[...] = jnp.zeros_like(dv_scr)

        row = lax.broadcasted_iota(jnp.int32, shape, 0)
        col = lax.broadcasted_iota(jnp.int32, shape, 1)
        m0 = col < HEAD_DIM
        later_keys = (row > col).astype(BF16)
        earlier_keys = (row < col).astype(BF16)
        q_both = _stack(_split_heads(q_ref[...] * SCALE))
        q_neg = _stack(_split_heads(q_ref[...] * (-SCALE)))
        dy_both = _stack(_split_heads(dy_ref[...]))
        seen = (c0_ref[...], c1_ref[...])

        def blocks_of(first, n, carry, masked, lane):
            r, dq = list(carry[0:2]), list(carry[2:4])
            blocks = range(n)
            chains = [(b, h) for b in blocks for h in range(2)]
            js = [first + b for b in blocks]
            rows = [pl.ds(pl.multiple_of(j * SB_BLOCK, SB_BLOCK), SB_BLOCK) for j in js]
            earlier = [(col + j * SB_BLOCK < row + i * SB_BLOCK) if masked else None for j in js]
            u = _unstack([_dot_nt(q_neg, k_scr[rw, :]) for rw in rows], chains)
            dw = _unstack([_dot_nt(dy_both, v_scr[rw, :]) for rw in rows], chains)
            lr = {ch: _log_rest(u[ch]) for ch in chains}
            lrm = {(b, h): jnp.where(earlier[b], lr[b, h], 0.0) if masked else lr[b, h] for b, h in chains}
            after = _unstack([_dot(_stack([lrm[ch].astype(BF16) for ch in chains]), later_keys)], chains)
            c = {}
            for h in range(2):
                ch = jnp.sum(jnp.where(col == lane, seen[h], 0.0), axis=1, keepdims=True)
                for b in reversed(blocks):
                    c[b, h] = ch
                    ch = ch + jnp.sum(lrm[b, h], axis=1, keepdims=True)
            w, de, lb = {}, {}, {}
            for b, h in chains:
                lb[b, h] = lr[b, h] - u[b, h]
                wv = jnp.exp(lb[b, h] + (after[b, h] + c[b, h]))
                w[b, h] = jnp.where(earlier[b], wv, 0.0) if masked else wv
                de[b, h] = dw[b, h] * w[b, h]
            before = _unstack([_dot(_stack([de[ch].astype(BF16) for ch in chains]), earlier_keys)], chains)
            dzb = {}
            for b, h in chains:
                dz = de[b, h] - jnp.exp(lb[b, h]) * (de[b, h] + (r[h] + before[b, h]))
                if masked:
                    dz = jnp.where(earlier[b], dz, 0.0)
                dzb[b, h] = dz.astype(BF16)
                r[h] = r[h] + jnp.sum(de[b, h], axis=1, keepdims=True)
            keys = pl.ds(pl.multiple_of(first * SB_BLOCK, SB_BLOCK), n * SB_BLOCK)
            dz_wide = [jnp.concatenate([dzb[b, h] for b in blocks], axis=1) for h in range(2)]
            w_wide = [jnp.concatenate([w[b, h].astype(BF16) for b in blocks], axis=1) for h in range(2)]
            dqc = _dot(_stack(dz_wide), k_scr[keys, :])
            dq = [dq[0] + dqc[:SB_BLOCK], dq[1] + dqc[SB_BLOCK:]]
            dk_scr[keys, :] += _dot_tn(_stack(dz_wide), q_both)
            dv_scr[keys, :] += _dot_tn(_stack(w_wide), dy_both)
            return r[0], r[1], dq[0], dq[1]

        zc = jnp.zeros((SB_BLOCK, 1), F32)
        zq = jnp.zeros(shape, F32)
        gi = i // SB_GROUP
        carry = lax.fori_loop(0, gi, lambda t, cr: blocks_of(t * SB_GROUP, SB_GROUP, cr, False, t), (zc, zc, zq, zq))
        carry = blocks_of(gi * SB_GROUP, SB_HALF, carry, True, SB_LANE_LOWER)
        carry = lax.cond(i % SB_GROUP >= SB_HALF,
                         lambda cr: blocks_of(gi * SB_GROUP + SB_HALF, SB_HALF, cr, True, SB_LANE_UPPER),
                         lambda cr: cr, carry)
        dq_ref[...] = (jnp.where(m0, carry[2], carry[3]) * SCALE).astype(dq_ref.dtype)

        @pl.when(i == nq - 1)
        def _():
            _copy_in([pltpu.make_async_copy(dk_scr, dk_hbm, sems.at[0]),
                      pltpu.make_async_copy(dv_scr, dv_hbm, sems.at[1])])

    def steps(q_ref, c0_ref, c1_ref, dy_ref, proj_hbm, dq_ref, *rest):
        def one(t, _):
            rows = pl.ds(pl.multiple_of(t * SB_BLOCK, SB_BLOCK), SB_BLOCK)
            body(pl.program_id(0) * SB_STEP + t, q_ref.at[rows, :], c0_ref.at[rows, :], c1_ref.at[rows, :],
                 dy_ref.at[rows, :], proj_hbm, dq_ref.at[rows, :], *rest)
            return 0

        lax.fori_loop(0, SB_STEP, one, 0)

    blk = lambda c0: pl.BlockSpec((SB_STEP * SB_BLOCK, LANES), lambda i: (i, c0))
    return pl.pallas_call(
        steps, grid=(nq // SB_STEP,),
        in_specs=[blk(COL_QA // LANES + hp), blk(0), blk(0), blk(hp), pl.BlockSpec(memory_space=pl.ANY)],
        out_specs=(blk(0), pl.BlockSpec(memory_space=pl.ANY), pl.BlockSpec(memory_space=pl.ANY)),
        out_shape=(jax.ShapeDtypeStruct((s, LANES), BF16), jax.ShapeDtypeStruct((s, LANES), F32),
                   jax.ShapeDtypeStruct((s, LANES), F32)),
        scratch_shapes=[pltpu.VMEM((s, LANES), BF16), pltpu.VMEM((s, LANES), BF16),
                        pltpu.VMEM((s, LANES), F32), pltpu.VMEM((s, LANES), F32), pltpu.SemaphoreType.DMA((2,))],
        compiler_params=_cparams(("arbitrary",)), name=name)(proj, carries[0], carries[1], dy_all, proj)


CH_DIAG = CH_BAND + CH_BLOCK


def _diag_index():
    x = np.arange(CH_DIAG)[None, :]
    return (np.clip(CH_PAD + (CH_BLOCK - 1) - x, -REL_CLIP, REL_CLIP) + REL_CLIP).astype(np.int32)


def _diag_onehot(idx_ref):
    rid = lax.broadcasted_iota(jnp.int32, (REL_PAD, CH_DIAG), 0)
    return (rid == idx_ref[...]).astype(F32)


def _bias_expand(rel_bias, *, name):
    nh = rel_bias.shape[0]
    table = jnp.pad(rel_bias, ((0, 0), (0, REL_PAD - N_REL)))

    def body(tab_ref, idx_ref, out_ref):
        diag = lax.dot_general(tab_ref[...], _diag_onehot(idx_ref), (((1,), (0,)), ((), ())),
                               precision=lax.Precision.HIGHEST, preferred_element_type=F32)

        def step(p, _):
            out_ref[p] = pltpu.roll(diag, (CH_DIAG - (CH_BLOCK - 1) + p) % CH_DIAG, 1)[:, :CH_BAND]
            return 0

        lax.fori_loop(0, CH_BLOCK, step, 0)

    out = pl.pallas_call(
        body, out_shape=jax.ShapeDtypeStruct((CH_BLOCK, nh, CH_BAND), F32),
        compiler_params=_cparams(), name=name)(table, jnp.asarray(_diag_index()))
    return out.transpose(1, 0, 2)


def _bias_fold(dbias, *, name):
    nh = dbias.shape[0]

    def body(db_ref, idx_ref, out_ref):
        def step(p, acc):
            wide = jnp.concatenate([db_ref[p], jnp.zeros((nh, CH_BLOCK), F32)], axis=1)
            return acc + pltpu.roll(wide, CH_BLOCK - 1 - p, 1)

        diag = lax.fori_loop(0, CH_BLOCK, step, jnp.zeros((nh, CH_DIAG), F32))
        out_ref[...] = lax.dot_general(diag, _diag_onehot(idx_ref), (((1,), (1,)), ((), ())),
                                       precision=lax.Precision.HIGHEST, preferred_element_type=F32)

    out = pl.pallas_call(
        body, out_shape=jax.ShapeDtypeStruct((nh, REL_PAD), F32),
        compiler_params=_cparams(), name=name)(dbias.transpose(1, 0, 2), jnp.asarray(_diag_index()))
    return out[:, :N_REL]


def _band_valid(i):
    row = lax.broadcasted_iota(jnp.int32, (CH_BLOCK, CH_BAND), 0)
    col = lax.broadcasted_iota(jnp.int32, (CH_BLOCK, CH_BAND), 1)
    first = row < CH_BLOCK // 2
    window = (first & (col < CH_BAND - 64)) | (jnp.logical_not(first) & (col >= 64))
    return window & (col + i * CH_BLOCK >= CH_PAD)


def _ch_load(proj_hbm, hp, s, kw_ref, kraw_scr, kn_scr, vp_scr, sems):
    _copy_in([pltpu.make_async_copy(proj_hbm.at[:, pl.ds(COL_KB + hp * LANES, LANES)], kraw_scr, sems.at[0]),
              pltpu.make_async_copy(proj_hbm.at[:, pl.ds(COL_VB + hp * LANES, LANES)],
                                    vp_scr.at[pl.ds(CH_PAD, s), :], sems.at[1])])
    kn_scr[0:CH_PAD, :] = jnp.zeros((CH_PAD, LANES), BF16)
    vp_scr[0:CH_PAD, :] = jnp.zeros((CH_PAD, LANES), BF16)
    rb = min(512, s)

    def step(t, _):
        r = pl.multiple_of(t * rb, rb)
        kn = _pair_norm(kraw_scr[pl.ds(r, rb), :].astype(F32), kw_ref[...])[0]
        kn_scr[pl.ds(CH_PAD + r, rb), :] = kn.astype(BF16)
        return 0

    lax.fori_loop(0, s // rb, step, 0)


def _ch_probs(qk, bias, valid):
    sc = jnp.where(valid, qk * SCALE + bias, NEG)
    p = jnp.exp(sc - jnp.max(sc, axis=1, keepdims=True))
    return p, 1.0 / jnp.sum(p, axis=1, keepdims=True)


def _ch_fwd(proj, qw2, kw2, bias, hp, *, name):
    s = proj.shape[0]
    nb = s // (CH_BLOCK * CH_STEP_FWD)
    shape = (CH_BLOCK * CH_STEP_FWD, LANES)
    subs = range(CH_STEP_FWD)

    def body(q_ref, proj_hbm, qw_ref, kw_ref, bias_ref, y_ref, kraw_scr, kn_scr, vp_scr, sems):
        i = pl.program_id(0)

        @pl.when(i == 0)
        def _():
            _ch_load(proj_hbm, hp, s, kw_ref, kraw_scr, kn_scr, vp_scr, sems)

        blk = [i * CH_STEP_FWD + t for t in subs]
        part = [slice(t * CH_BLOCK, (t + 1) * CH_BLOCK) for t in subs]
        bands = [pl.ds(pl.multiple_of(b * CH_BLOCK, CH_BLOCK), CH_BAND) for b in blk]
        valid = [_band_valid(b) for b in blk]
        qn = _pair_norm(q_ref[...].astype(F32), qw_ref[...])[0].astype(BF16)
        scores = [_dot_nt(_stack(_split_heads(qn[part[t]])), kn_scr[bands[t], :]) for t in subs]
        probs = {(t, h): _ch_probs(scores[t][h * CH_BLOCK:(h + 1) * CH_BLOCK], bias_ref[h], valid[t])
                 for t in subs for h in range(2)}
        outs = [_dot(_stack([probs[t, h][0].astype(BF16) for h in range(2)]), vp_scr[bands[t], :]) for t in subs]
        for t in subs:
            o = [outs[t][h * CH_BLOCK:(h + 1) * CH_BLOCK] * probs[t, h][1] for h in range(2)]
            y_ref[part[t], :] = jnp.where(_head0_lanes((CH_BLOCK, LANES)), o[0], o[1]).astype(y_ref.dtype)

    return pl.pallas_call(
        body, grid=(nb,),
        in_specs=[pl.BlockSpec(shape, lambda i: (i, COL_QB // LANES + hp)), pl.BlockSpec(memory_space=pl.ANY),
                  pl.BlockSpec((1, LANES), lambda i: (0, 0)), pl.BlockSpec((1, LANES), lambda i: (0, 0)),
                  pl.BlockSpec((2, CH_BLOCK, CH_BAND), lambda i: (hp, 0, 0))],
        out_specs=pl.BlockSpec(shape, lambda i: (i, 0)),
        out_shape=jax.ShapeDtypeStruct((s, LANES), BF16),
        scratch_shapes=[pltpu.VMEM((s, LANES), BF16), pltpu.VMEM((s + CH_PAD, LANES), BF16),
                        pltpu.VMEM((s + CH_PAD, LANES), BF16), pltpu.SemaphoreType.DMA((2,))],
        compiler_params=_cparams(("arbitrary",)), name=name)(proj, proj, qw2, kw2, bias)


def _ch_bwd(proj, dy_all, qw2, kw2, bias, hp, *, name):
    s = proj.shape[0]
    nb = s // (CH_BLOCK * CH_STEP)
    shape = (CH_BLOCK * CH_STEP, LANES)
    subs = range(CH_STEP)
    rb = min(512, s)

    def body(q_ref, dy_ref, proj_hbm, qw_ref, kw_ref, bias_ref, dq_ref, dk_hbm, dv_hbm, dbias_ref, dqw_ref, dkw_ref,
             kraw_scr, kn_scr, vp_scr, dkn_scr, dvp_scr, sems):
        i = pl.program_id(0)

        @pl.when(i == 0)
        def _():
            _ch_load(proj_hbm, hp, s, kw_ref, kraw_scr, kn_scr, vp_scr, sems)
            dkn_scr[...] = jnp.zeros_like(dkn_scr)
            dvp_scr[...] = jnp.zeros_like(dvp_scr)
            dbias_ref[...] = jnp.zeros_like(dbias_ref)
            dqw_ref[...] = jnp.zeros_like(dqw_ref)

        blk = [i * CH_STEP + t for t in subs]
        rows = [slice(t * CH_BLOCK, (t + 1) * CH_BLOCK) for t in subs]
        bands = [pl.ds(pl.multiple_of(b * CH_BLOCK, CH_BLOCK), CH_BAND) for b in blk]
        valid = [_band_valid(b) for b in blk]
        qn, qhat, qinv = _pair_norm(q_ref[...].astype(F32), qw_ref[...])
        qn = qn.astype(BF16)
        dy = dy_ref[...]
        q_both = [_stack(_split_heads(qn[rows[t]])) for t in subs]
        dy_both = [_stack(_split_heads(dy[rows[t]])) for t in subs]
        scores = [_dot_nt(q_both[t], kn_scr[bands[t], :]) for t in subs]
        dprobs = [_dot_nt(dy_both[t], vp_scr[bands[t], :]) for t in subs]
        pb, dsb = {}, {}
        for t in subs:
            for h in range(2):
                part = slice(h * CH_BLOCK, (h + 1) * CH_BLOCK)
                p, rl = _ch_probs(scores[t][part], bias_ref[h], valid[t])
                p = p * rl
                dp = dprobs[t][part]
                ds = p * (dp - jnp.sum(dp * p, axis=1, keepdims=True))
                dbias_ref[h] += ds
                pb[t, h] = p.astype(BF16)
                dsb[t, h] = (ds * SCALE).astype(BF16)
        dsb = [_stack([dsb[t, 0], dsb[t, 1]]) for t in subs]
        dqn = [_dot(dsb[t], kn_scr[bands[t], :]) for t in subs]
        dkc = [_dot_tn(dsb[t], q_both[t]) for t in subs]
        dvc = [_dot_tn(_stack([pb[t, 0], pb[t, 1]]), dy_both[t]) for t in subs]
        for t in subs:
            dkn_scr[bands[t], :] += dkc[t]
            dvp_scr[bands[t], :] += dvc[t]
        dqn = _stack([jnp.where(_head0_lanes((CH_BLOCK, LANES)), d[:CH_BLOCK], d[CH_BLOCK:]) for d in dqn])
        dq_ref[...] = _pair_norm_bwd(dqn, qhat, qinv, qw_ref[...]).astype(dq_ref.dtype)
        dqw_ref[...] += (dqn * qhat).reshape(CH_STEP * CH_BLOCK // 8, 8, LANES).sum(axis=0)

        @pl.when(i == nb - 1)
        def _():
            def step(t, acc):
                r = pl.multiple_of(t * rb, rb)
                rows = pl.ds(CH_PAD + r, rb)
                _, khat, kinv = _pair_norm(kraw_scr[pl.ds(r, rb), :].astype(F32), kw_ref[...])
                dkn = dkn_scr[rows, :]
                dkn_scr[rows, :] = _pair_norm_bwd(dkn, khat, kinv, kw_ref[...])
                return acc + (dkn * khat).reshape(rb // 8, 8, LANES).sum(axis=0)

            dkw_ref[...] = lax.fori_loop(0, s // rb, step, jnp.zeros((8, LANES), F32))
            _copy_in([pltpu.make_async_copy(dkn_scr.at[pl.ds(CH_PAD, s), :], dk_hbm, sems.at[0]),
                      pltpu.make_async_copy(dvp_scr.at[pl.ds(CH_PAD, s), :], dv_hbm, sems.at[1])])

    blk = lambda c0: pl.BlockSpec(shape, lambda i: (i, c0))
    vec = pl.BlockSpec((1, LANES), lambda i: (0, 0))
    part = pl.BlockSpec((8, LANES), lambda i: (0, 0))
    hbm = pl.BlockSpec(memory_space=pl.ANY)
    return pl.pallas_call(
        body, grid=(nb,),
        in_specs=[blk(COL_QB // LANES + hp), blk(D_SB // LANES + hp), hbm, vec, vec,
                  pl.BlockSpec((2, CH_BLOCK, CH_BAND), lambda i: (hp, 0, 0))],
        out_specs=(blk(0), hbm, hbm, pl.BlockSpec((2, CH_BLOCK, CH_BAND), lambda i: (0, 0, 0)), part, part),
        out_shape=(jax.ShapeDtypeStruct((s, LANES), BF16), jax.ShapeDtypeStruct((s, LANES), F32),
                   jax.ShapeDtypeStruct((s, LANES), F32), jax.ShapeDtypeStruct((2, CH_BLOCK, CH_BAND), F32),
                   jax.ShapeDtypeStruct((8, LANES), F32), jax.ShapeDtypeStruct((8, LANES), F32)),
        scratch_shapes=[pltpu.VMEM((s, LANES), BF16), pltpu.VMEM((s + CH_PAD, LANES), BF16),
                        pltpu.VMEM((s + CH_PAD, LANES), BF16), pltpu.VMEM((s + CH_PAD, LANES), F32),
                        pltpu.VMEM((s + CH_PAD, LANES), F32), pltpu.SemaphoreType.DMA((2,))],
        compiler_params=_cparams(("arbitrary",)), name=name)(proj, dy_all, proj, qw2, kw2, bias)


def _shift_down(x, halo, k):
    out = pltpu.roll(x, k, 0)
    row = lax.broadcasted_iota(jnp.int32, x.shape, 0)
    for t in range(k):
        out = jnp.where(row == t, halo[8 - k + t:8 - k + t + 1, :], out)
    return out


def _shift_up(x, halo, k):
    n = x.shape[0]
    out = pltpu.roll(x, n - k, 0)
    row = lax.broadcasted_iota(jnp.int32, x.shape, 0)
    for t in range(k):
        out = jnp.where(row == n - k + t, halo[t:t + 1, :], out)
    return out


def _conv_specs(s, tm):
    cw = D_CV
    tile = lambda c0: pl.BlockSpec((tm, cw), lambda i: (i, c0))
    above = lambda c0: pl.BlockSpec((8, cw), lambda i: (jnp.maximum(i * (tm // 8) - 1, 0), c0))
    below = lambda c0: pl.BlockSpec((8, cw), lambda i: (jnp.minimum((i + 1) * (tm // 8), s // 8 - 1), c0))
    return tile, above, below


def _conv_fwd(proj, w8, *, name):
    s = proj.shape[0]
    tm = min(512, s)
    tile, above, _ = _conv_specs(s, tm)

    def body(gb_ref, gc_ref, xc_ref, gca_ref, xca_ref, w_ref, y_ref):
        hc = gc_ref[...].astype(F32) * xc_ref[...].astype(F32)
        top = jnp.where(pl.program_id(0) > 0, gca_ref[...].astype(F32) * xca_ref[...].astype(F32), 0.0)
        u = w_ref[0:1, :] * _shift_down(hc, top, 2) + w_ref[1:2, :] * _shift_down(hc, top, 1) + w_ref[2:3, :] * hc
        y_ref[...] = (gb_ref[...].astype(F32) * u).astype(y_ref.dtype)

    cb, cc, cx = COL_GB // D_CV, COL_GC // D_CV, COL_XC // D_CV
    return pl.pallas_call(
        body, grid=(s // tm,),
        in_specs=[tile(cb), tile(cc), tile(cx), above(cc), above(cx), pl.BlockSpec((8, D_CV), lambda i: (0, 0))],
        out_specs=pl.BlockSpec((tm, D_CV), lambda i: (i, 0)),
        out_shape=jax.ShapeDtypeStruct((s, D_CV), BF16),
        compiler_params=_cparams(("parallel",)), name=name)(proj, proj, proj, proj, proj, w8)


def _conv_bwd(proj, dy_all, w8, *, name):
    s = proj.shape[0]
    tm = min(512, s)
    nt = s // tm
    tile, above, below = _conv_specs(s, tm)

    def body(gb_ref, gc_ref, xc_ref, dy_ref, gca_ref, xca_ref, gbb_ref, dyb_ref, w_ref,
             dgb_ref, dgc_ref, dxc_ref, dw_ref):
        i = pl.program_id(0)
        gb, gc, xc = gb_ref[...].astype(F32), gc_ref[...].astype(F32), xc_ref[...].astype(F32)
        dy = dy_ref[...].astype(F32)
        hc = gc * xc
        top = jnp.where(i > 0, gca_ref[...].astype(F32) * xca_ref[...].astype(F32), 0.0)
        hc1, hc2 = _shift_down(hc, top, 1), _shift_down(hc, top, 2)
        u = w_ref[0:1, :] * hc2 + w_ref[1:2, :] * hc1 + w_ref[2:3, :] * hc
        du = dy * gb
        bottom = jnp.where(i < nt - 1, dyb_ref[...].astype(F32) * gbb_ref[...].astype(F32), 0.0)
        dhc = w_ref[2:3, :] * du + w_ref[1:2, :] * _shift_up(du, bottom, 1) + w_ref[0:1, :] * _shift_up(du, bottom, 2)
        dgb_ref[...] = (dy * u).astype(dgb_ref.dtype)
        dgc_ref[...] = (dhc * xc).astype(dgc_ref.dtype)
        dxc_ref[...] = (dhc * gc).astype(dxc_ref.dtype)

        @pl.when(i == 0)
        def _():
            dw_ref[...] = jnp.zeros_like(dw_ref)
        dw_ref[0:1, :] += jnp.sum(du * hc2, axis=0, keepdims=True)
        dw_ref[1:2, :] += jnp.sum(du * hc1, axis=0, keepdims=True)
        dw_ref[2:3, :] += jnp.sum(du * hc, axis=0, keepdims=True)

    cb, cc, cx = COL_GB // D_CV, COL_GC // D_CV, COL_XC // D_CV
    cy = (D_SB + D_CH) // D_CV
    out_tile = pl.BlockSpec((tm, D_CV), lambda i: (i, 0))
    act = jax.ShapeDtypeStruct((s, D_CV), BF16)
    return pl.pallas_call(
        body, grid=(nt,),
        in_specs=[tile(cb), tile(cc), tile(cx), tile(cy), above(cc), above(cx), below(cb), below(cy),
                  pl.BlockSpec((8, D_CV), lambda i: (0, 0))],
        out_specs=(out_tile, out_tile, out_tile, pl.BlockSpec((8, D_CV), lambda i: (0, 0))),
        out_shape=(act, act, act, jax.ShapeDtypeStruct((8, D_CV), F32)),
        compiler_params=_cparams(("arbitrary",)), name=name)(proj, proj, proj, dy_all, proj, proj, proj, dy_all, w8)


def _ffn_up(h, wg, wu, *, name):
    s, d = h.shape
    f = wg.shape[1]
    tm, tn = min(512, s), _tile(f, 1408)

    def body(h_ref, wg_ref, wu_ref, g_ref, u_ref, a_ref):
        hv = h_ref[...]
        g = _dot(hv, wg_ref[...])
        u = _dot(hv, wu_ref[...])
        g_ref[...] = g.astype(BF16)
        u_ref[...] = u.astype(BF16)
        a_ref[...] = (g * jax.nn.sigmoid(g) * u).astype(BF16)

    wspec = pl.BlockSpec((d, tn), lambda j, i: (0, j))
    ospec = pl.BlockSpec((tm, tn), lambda j, i: (i, j))
    act = jax.ShapeDtypeStruct((s, f), BF16)
    return pl.pallas_call(
        body, grid=(f // tn, s // tm),
        in_specs=[pl.BlockSpec((tm, d), lambda j, i: (i, 0)), wspec, wspec],
        out_specs=(ospec, ospec, ospec), out_shape=(act, act, act),
        compiler_params=_cparams(("parallel", "parallel")), name=name)(h, wg, wu)


def _ffn_dact(dx, wd, g, u, *, name):
    s, d = dx.shape
    f = wd.shape[0]
    tm, tf = min(512, s), _tile(f, 1408)

    def body(dx_ref, wd_ref, g_ref, u_ref, dg_ref, du_ref):
        da = _dot_nt(dx_ref[...].astype(BF16), wd_ref[...])
        gv, uv = g_ref[...].astype(F32), u_ref[...].astype(F32)
        sg = jax.nn.sigmoid(gv)
        dg_ref[...] = (da * uv * sg * (1.0 + gv * (1.0 - sg))).astype(BF16)
        du_ref[...] = (da * gv * sg).astype(BF16)

    tspec = pl.BlockSpec((tm, tf), lambda j, i: (i, j))
    act = jax.ShapeDtypeStruct((s, f), BF16)
    return pl.pallas_call(
        body, grid=(f // tf, s // tm),
        in_specs=[pl.BlockSpec((tm, d), lambda j, i: (i, 0)), pl.BlockSpec((tf, d), lambda j, i: (j, 0)), tspec, tspec],
        out_specs=(tspec, tspec), out_shape=(act, act),
        compiler_params=_cparams(("parallel", "parallel")), name=name)(dx, wd, g, u)


def _loss_head(y, target, *, name):
    s, d = y.shape
    tm = min(512, s)

    def body(y_ref, t_ref, dy_ref, sq_ref):
        err = y_ref[...] - t_ref[...]
        dy_ref[...] = err * (1.0 / d)

        @pl.when(pl.program_id(0) == 0)
        def _():
            sq_ref[...] = jnp.zeros_like(sq_ref)
        sq_ref[...] += (err * err).reshape(tm // 8, 8, d).sum(axis=0)

    tile = pl.BlockSpec((tm, d), lambda i: (i, 0))
    return pl.pallas_call(
        body, grid=(s // tm,), in_specs=[tile, tile],
        out_specs=(tile, pl.BlockSpec((8, d), lambda i: (0, 0))),
        out_shape=(jax.ShapeDtypeStruct((s, d), F32), jax.ShapeDtypeStruct((8, d), F32)),
        compiler_params=_cparams(("arbitrary",)), name=name)(y, target)


def _adamw(parts, w, m, v, *, name):
    r, c = w.shape
    tr = r if r <= 512 else _tile_rows(r)

    def body(p_ref, w_ref, m_ref, v_ref, g_ref, d_ref, nm_ref, nv_ref):
        g = p_ref[0].astype(F32)
        for dev in range(1, N_DEV):
            g = g + p_ref[dev].astype(F32)
        nm = ADAM_B1 * m_ref[...] + (1.0 - ADAM_B1) * g
        nv = ADAM_B2 * v_ref[...] + (1.0 - ADAM_B2) * (g * g)
        m_hat = nm / (1.0 - ADAM_B1 ** ADAM_STEP)
        v_hat = nv / (1.0 - ADAM_B2 ** ADAM_STEP)
        g_ref[...] = g
        d_ref[...] = -ADAM_LR * (m_hat / (jnp.sqrt(v_hat) + ADAM_EPS) + ADAM_WD * w_ref[...])
        nm_ref[...] = nm
        nv_ref[...] = nv

    tile = pl.BlockSpec((tr, c), lambda i: (i, 0))
    out = jax.ShapeDtypeStruct((r, c), F32)
    return pl.pallas_call(
        body, grid=(r // tr,),
        in_specs=[pl.BlockSpec((N_DEV, tr, c), lambda i: (0, i, 0)), tile, tile, tile],
        out_specs=(tile, tile, tile, tile), out_shape=(out, out, out, out),
        compiler_params=_cparams(("parallel",)), name=name)(parts, w, m, v)


def _tile_rows(r):
    for t in (512, 256, 128, 64, 32, 16, 8):
        if r % t == 0:
            return t
    return r


def _peer(k):
    x, y, c = (lax.axis_index(a) for a in MESH_AXES)
    px = 1 - x if k & 4 else x
    py = 1 - y if k & 2 else y
    pc = 1 - c if k & 1 else c
    return (px, py, pc), 4 * px + 2 * py + pc


def _exchange(src_of, dst_of, local_sem, send_sems, recv_sems):
    x, y, c = (lax.axis_index(a) for a in MESH_AXES)
    me = 4 * x + 2 * y + c
    mine = pltpu.make_async_copy(src_of(me), dst_of(me), local_sem)
    mine.start()
    sends = []
    for k in range(1, N_DEV):
        dev, idx = _peer(k)
        cp = pltpu.make_async_remote_copy(src_ref=src_of(idx), dst_ref=dst_of(me), send_sem=send_sems.at[k - 1],
                                          recv_sem=recv_sems.at[k - 1], device_id=dev,
                                          device_id_type=pl.DeviceIdType.MESH)
        cp.start()
        sends.append(cp)
    for k in range(1, N_DEV):
        dev, idx = _peer(k)
        pltpu.make_async_remote_copy(src_ref=src_of(idx), dst_ref=dst_of(idx), send_sem=send_sems.at[k - 1],
                                     recv_sem=recv_sems.at[k - 1], device_id=dev,
                                     device_id_type=pl.DeviceIdType.MESH).wait_recv()
    for cp in sends:
        cp.wait_send()
    mine.wait()


def _comm_call(body, out_shape, space, name, *args):
    spec = pl.BlockSpec(memory_space=space)
    return pl.pallas_call(
        body, in_specs=[spec] * len(args), out_specs=spec, out_shape=out_shape,
        scratch_shapes=[pltpu.SemaphoreType.DMA, pltpu.SemaphoreType.DMA((N_DEV - 1,)),
                        pltpu.SemaphoreType.DMA((N_DEV - 1,))],
        compiler_params=pltpu.CompilerParams(vmem_limit_bytes=VMEM_LIMIT_MIB * 2 ** 20), name=name)(*args)


def _gather_by_chip(x_ref, out_ref, local_sem, send_sems, recv_sems):
    x, y, c = (lax.axis_index(a) for a in MESH_AXES)
    me, sibling = (x, y, c), (x, y, 1 - c)
    chips = [(1 - x, y), (x, 1 - y), (1 - x, 1 - y)]

    def slab(dev):
        return out_ref.at[4 * dev[0] + 2 * dev[1] + dev[2]]

    def copy(k, owner, to, src=None):
        return pltpu.make_async_remote_copy(
            src_ref=slab(owner) if src is None else src, dst_ref=slab(owner), send_sem=send_sems.at[k],
            recv_sem=recv_sems.at[k], device_id=to, device_id_type=pl.DeviceIdType.MESH)

    mine = pltpu.make_async_copy(x_ref, slab(me), local_sem)
    mine.start()
    first = [copy(0, me, sibling, src=x_ref)] + [copy(1 + j, me, (*chip, c), src=x_ref) for j, chip in enumerate(chips)]
    for cp in first:
        cp.start()
    passed = [copy(4 + j, (*chip, c), sibling) for j, chip in enumerate(chips)]
    for j, chip in enumerate(chips):
        copy(1 + j, (*chip, c), me).wait_recv()
        passed[j].start()
    copy(0, sibling, me).wait_recv()
    for j, chip in enumerate(chips):
        copy(4 + j, (*chip, 1 - c), me).wait_recv()
    for cp in first + passed:
        cp.wait_send()
    mine.wait()


def _all_gather(shard, *, name, space=pl.ANY):
    def body(x_ref, out_ref, local_sem, send_sems, recv_sems):
        _gather_by_chip(x_ref, out_ref, local_sem, send_sems, recv_sems)

    return _comm_call(body, jax.ShapeDtypeStruct((N_DEV,) + shard.shape, shard.dtype), space, name, shard)


def _all_to_all(slabs, *, name):
    def body(x_ref, out_ref, local_sem, send_sems, recv_sems):
        _exchange(lambda d: x_ref.at[d], lambda d: out_ref.at[d], local_sem, send_sems, recv_sems)

    return _comm_call(body, jax.ShapeDtypeStruct(slabs.shape, slabs.dtype), pl.ANY, name, slabs)


def _cols_from_dev(g, depth):
    rows = g.shape[1] // depth
    return g.reshape(N_DEV, depth, rows, g.shape[2]).transpose(1, 2, 0, 3).reshape(depth, rows, -1)


def _rows_from_dev(g, depth):
    r = g.shape[1] // depth
    return g.reshape(N_DEV, depth, r, g.shape[2]).transpose(1, 0, 2, 3).reshape(depth, N_DEV * r, g.shape[2])


def _cols_to_dev(g):
    depth, rows, n = g.shape
    return g.reshape(depth, rows, N_DEV, n // N_DEV).transpose(2, 0, 1, 3).reshape(N_DEV, depth * rows, n // N_DEV)


def _rows_to_dev(g):
    depth, rows, n = g.shape
    return g.reshape(depth, N_DEV, rows // N_DEV, n).transpose(1, 0, 2, 3).reshape(N_DEV, -1, n)


def _pack(arrays):
    flat = jnp.concatenate([a.reshape(-1) for a in arrays])
    pad = (-flat.shape[0]) % (8 * LANES)
    return jnp.pad(flat, (0, pad)).reshape(-1, LANES)


def _unpack(buf, shapes):
    flat = buf.reshape(-1)
    out, pos = [], 0
    for shp in shapes:
        n = int(np.prod(shp))
        out.append(flat[pos:pos + n].reshape(shp))
        pos += n
    return out


def _fold_heads(part8):
    t = part8.sum(axis=(0, 1))
    return t[:HEAD_DIM] + t[HEAD_DIM:]


def _forward_backward(x, target, wts):
    depth = wts["w_in"].shape[0]
    row = lambda a: a.reshape(1, -1)
    saved = []
    for l in range(depth):
        t = f"l{l}_"
        h = _norm_fwd([x], row(wts["attn_norm_w"][l]), group=D_MODEL, name=t + "attn_norm")
        proj = _mm([(h, wts["w_in"][l])], nt=False, name=t + "proj")
        qw2 = row(jnp.tile(wts["q_norm_w"][l], 2))
        kw2 = row(jnp.tile(wts["k_norm_w"][l], 2))
        bias = _bias_expand(wts["rel_bias"][l], name=t + "bias")
        w8 = jnp.pad(wts["conv_w"][l], ((0, 5), (0, 0)))
        sb = [_sb_fwd(proj, hp, name=t + f"sb{hp}") for hp in range(D_SB // LANES)]
        ys = [o[0] for o in sb]
        sb_carries = [o[1:] for o in sb]
        ys += [_ch_fwd(proj, qw2, kw2, bias, hp, name=t + f"ch{hp}") for hp in range(D_CH // LANES)]
        ys.append(_conv_fwd(proj, w8, name=t + "conv"))
        yn = _norm_fwd(ys, row(wts["out_norm_w"][l]), group=HEAD_DIM, name=t + "out_norm")
        x_mid = _mm([(yn, wts["w_out"][l])], nt=False, res=x, out_dtype=F32, name=t + "out_proj")
        h2 = _norm_fwd([x_mid], row(wts["ffn_norm_w"][l]), group=D_MODEL, name=t + "ffn_norm")
        g, u, a = _ffn_up(h2, wts["w_gate"][l], wts["w_up"][l], name=t + "ffn_up")
        x_out = _mm([(a, wts["w_down"][l])], nt=False, res=x_mid, out_dtype=F32, name=t + "ffn_down")
        saved.append(dict(x=x, h=h, proj=proj, sb_carries=sb_carries, qw2=qw2, kw2=kw2, bias=bias, w8=w8, ys=ys, yn=yn, x_mid=x_mid,
                          h2=h2, g=g, u=u, a=a))
        x = x_out

    dx, sq = _loss_head(x, target, name="loss_head")
    grads = {k: [None] * depth for k in
             ("attn_norm_w", "w_in", "q_norm_w", "k_norm_w", "rel_bias", "conv_w", "out_norm_w", "w_out",
              "ffn_norm_w", "w_gate", "w_up", "w_down")}
    for l in reversed(range(depth)):
        t = f"l{l}_b_"
        sv = saved[l]
        grads["w_down"][l] = _mm_tn(sv["a"], dx, name=t + "w_down")
        dg, du = _ffn_dact(dx, wts["w_down"][l], sv["g"], sv["u"], name=t + "ffn_dact")
        grads["w_gate"][l] = _mm_tn(sv["h2"], dg, name=t + "w_gate")
        grads["w_up"][l] = _mm_tn(sv["h2"], du, name=t + "w_up")
        dh2 = _mm([(dg, wts["w_gate"][l]), (du, wts["w_up"][l])], nt=True, name=t + "ffn_dh")
        dx_mid, dw8 = _norm_bwd([sv["x_mid"]], row(wts["ffn_norm_w"][l]), dh2, group=D_MODEL, res=dx,
                                name=t + "ffn_norm")
        grads["ffn_norm_w"][l] = dw8.sum(axis=0)
        grads["w_out"][l] = _mm_tn(sv["yn"], dx_mid, name=t + "w_out")
        dyn = _mm([(dx_mid, wts["w_out"][l])], nt=True, name=t + "out_dy")
        dy, dw8 = _norm_bwd(sv["ys"], row(wts["out_norm_w"][l]), dyn, group=HEAD_DIM, out_dtype=BF16,
                            name=t + "out_norm")
        grads["out_norm_w"][l] = dw8.sum(axis=0)
        proj = sv["proj"]
        n_sb, n_ch = D_SB // LANES, D_CH // LANES
        sb = [_sb_bwd(proj, sv["sb_carries"][hp], dy, hp, name=t + f"sb{hp}") for hp in range(n_sb)]
        ch = [_ch_bwd(proj, dy, sv["qw2"], sv["kw2"], sv["bias"], hp, name=t + f"ch{hp}") for hp in range(n_ch)]
        dgb, dgc, dxc, dcw = _conv_bwd(proj, dy, sv["w8"], name=t + "conv")
        grads["conv_w"][l] = dcw[:3]
        grads["q_norm_w"][l] = _fold_heads(jnp.stack([c[4] for c in ch]))
        grads["k_norm_w"][l] = _fold_heads(jnp.stack([c[5] for c in ch]))
        grads["rel_bias"][l] = _bias_fold(jnp.concatenate([c[3] for c in ch], axis=0), name=t + "bias")
        pieces = ([o[0] for o in sb] + [o[1] for o in sb] + [o[2] for o in sb]
                  + [o[0] for o in ch] + [o[1] for o in ch] + [o[2] for o in ch] + [dgb, dgc, dxc])
        dproj = jnp.concatenate([p.astype(BF16) for p in pieces], axis=1)
        grads["w_in"][l] = _mm_tn(sv["h"], dproj, name=t + "w_in")
        dh = _mm([(dproj, wts["w_in"][l])], nt=True, name=t + "proj_dh")
        dx, dw8 = _norm_bwd([sv["x"]], row(wts["attn_norm_w"][l]), dh, group=D_MODEL, res=dx_mid,
                            name=t + "attn_norm")
        grads["attn_norm_w"][l] = dw8.sum(axis=0)
    return sq, dx, {k: jnp.stack(v) for k, v in grads.items()}


SMALL = ("attn_norm_w", "q_norm_w", "k_norm_w", "rel_bias", "conv_w", "out_norm_w", "ffn_norm_w")
ORDER = ("attn_norm_w", "w_in", "q_norm_w", "k_norm_w", "rel_bias", "conv_w", "out_norm_w", "w_out",
         "ffn_norm_w", "w_gate", "w_up", "w_down")


def kernel(x, attn_norm_w, w_in, q_norm_w, k_norm_w, rel_bias, conv_w, out_norm_w, w_out, ffn_norm_w, w_gate, w_up, w_down, loss_target, m_attn_norm_w, m_w_in, m_q_norm_w, m_k_norm_w, m_rel_bias, m_conv_w, m_out_norm_w, m_w_out, m_ffn_norm_w, m_w_gate, m_w_up, m_w_down, v_attn_norm_w, v_w_in, v_q_norm_w, v_k_norm_w, v_rel_bias, v_conv_w, v_out_norm_w, v_w_out, v_ffn_norm_w, v_w_gate, v_w_up, v_w_down):
    w = dict(attn_norm_w=attn_norm_w, w_in=w_in, q_norm_w=q_norm_w, k_norm_w=k_norm_w, rel_bias=rel_bias,
             conv_w=conv_w, out_norm_w=out_norm_w, w_out=w_out, ffn_norm_w=ffn_norm_w, w_gate=w_gate, w_up=w_up,
             w_down=w_down)
    mom = dict(attn_norm_w=m_attn_norm_w, w_in=m_w_in, q_norm_w=m_q_norm_w, k_norm_w=m_k_norm_w, rel_bias=m_rel_bias,
               conv_w=m_conv_w, out_norm_w=m_out_norm_w, w_out=m_w_out, ffn_norm_w=m_ffn_norm_w, w_gate=m_w_gate,
               w_up=m_w_up, w_down=m_w_down)
    var = dict(attn_norm_w=v_attn_norm_w, w_in=v_w_in, q_norm_w=v_q_norm_w, k_norm_w=v_k_norm_w, rel_bias=v_rel_bias,
               conv_w=v_conv_w, out_norm_w=v_out_norm_w, w_out=v_w_out, ffn_norm_w=v_ffn_norm_w, w_gate=v_w_gate,
               w_up=v_w_up, w_down=v_w_down)
    depth = w_in.shape[0]
    seq = x.shape[1]
    me = 4 * lax.axis_index("x") + 2 * lax.axis_index("y") + lax.axis_index("c")
    flat2 = lambda a: a.reshape(-1, a.shape[-1])

    full = {k: w[k] for k in SMALL if k != "conv_w"}
    for k in ("w_in", "w_gate", "w_up"):
        full[k] = _cols_from_dev(_all_gather(flat2(w[k]).astype(BF16), name="gather_" + k), depth)
    for k in ("w_out", "w_down"):
        full[k] = _rows_from_dev(_all_gather(flat2(w[k]).astype(BF16), name="gather_" + k), depth)
    conv_shard = conv_w.shape[2]
    conv_all = _all_gather(_pack([conv_w]), name="gather_conv_w", space=pltpu.VMEM)
    conv_all = conv_all.reshape(N_DEV, -1)[:, :depth * 3 * conv_shard].reshape(N_DEV, depth, 3, conv_shard)
    full["conv_w"] = conv_all.transpose(1, 2, 0, 3).reshape(depth, 3, N_DEV * conv_shard)

    sq, dx, grads = _forward_backward(x.reshape(seq, D_MODEL), loss_target.reshape(seq, D_MODEL), full)
    loss = lax.psum(0.5 / D_MODEL * jnp.sum(sq), MESH_AXES)

    outs = {}
    for k in ("w_in", "w_gate", "w_up", "w_out", "w_down"):
        slabs = _cols_to_dev(grads[k]) if k in ("w_in", "w_gate", "w_up") else _rows_to_dev(grads[k])
        parts = _all_to_all(slabs.astype(BF16), name="scatter_" + k)
        res = _adamw(parts, flat2(w[k]), flat2(mom[k]), flat2(var[k]), name="adamw_" + k)
        outs[k] = [r.reshape(w[k].shape) for r in res]

    shapes = [grads[k].shape for k in SMALL]
    col0 = me * conv_shard

    def widen(a):
        return lax.dynamic_update_slice(jnp.zeros((depth, 3, N_DEV * conv_shard), F32), a, (0, 0, col0))

    parts = _all_gather(_pack([grads[k] for k in SMALL]), name="gather_small_grads", space=pltpu.VMEM)
    pw = _pack([widen(w[k]) if k == "conv_w" else w[k] for k in SMALL])
    pm = _pack([widen(mom[k]) if k == "conv_w" else mom[k] for k in SMALL])
    pv = _pack([widen(var[k]) if k == "conv_w" else var[k] for k in SMALL])
    res = [_unpack(r, shapes) for r in _adamw(parts, pw, pm, pv, name="adamw_small")]
    for idx, k in enumerate(SMALL):
        vals = [r[idx] for r in res]
        if k == "conv_w":
            vals = [lax.dynamic_slice(a, (0, 0, col0), (depth, 3, conv_shard)) for a in vals]
        outs[k] = vals

    result = [loss, dx.reshape(x.shape)]
    for part in range(4):
        result += [outs[k][part] for k in ORDER]
    return tuple(result)
```
